```python
import jax, jax.numpy as jnp
from jax import lax
import numpy as np

D_MODEL = 4096
BATCH = 4
SEQ = 2048
DEPTH = 2
DEC_BATCH = 128
DEC_SEQ = 8
PAST_LEN = 16384
PAGE_SIZE = 128

MIX_WIDTH = D_MODEL // 2
POOL_GROUPS = 4
POOL_WINDOWS = (2, 4, 8, 16)
POOL_GROUP_DIM = MIX_WIDTH // POOL_GROUPS
POOL_STATE = max(POOL_WINDOWS) - 1
SG_HEADS = 8
SG_HEAD_DIM = MIX_WIDTH // SG_HEADS
CHUNK = 128
CONV_K = 3
CONV_STATE = CONV_K - 1
N_BRANCH = 3
IN_COLS = MIX_WIDTH * 6 + N_BRANCH * D_MODEL
N_EXPERT_GROUPS = 4
EXPERTS_PER_GROUP = 8
N_EXPERTS = N_EXPERT_GROUPS * EXPERTS_PER_GROUP
TOP_K = 2
D_EXPERT = D_MODEL // 4
EXPERT_BLOCK = 128
EPS = 1e-6

kernel_name = "hybrid_pool_sgmlp_shortconv_hmoe_step"


def rmsnorm(x, g):
    xf = x.astype(jnp.float32)
    y = xf * lax.rsqrt(jnp.mean(xf * xf, axis=-1, keepdims=True) + EPS)
    return (y * g.astype(jnp.float32)).astype(x.dtype)


def layernorm(x, g, b):
    xf = x.astype(jnp.float32)
    mu = jnp.mean(xf, axis=-1, keepdims=True)
    xc = xf - mu
    y = xc * lax.rsqrt(jnp.mean(xc * xc, axis=-1, keepdims=True) + EPS)
    return (y * g.astype(jnp.float32) + b.astype(jnp.float32)).astype(x.dtype)


def pool_mixer(a, prev, start_pos, w_grp, scale):
    bn, L, _ = a.shape
    ext = jnp.concatenate([prev.astype(a.dtype), a], axis=1)
    cs = jnp.cumsum(jnp.pad(ext.astype(jnp.float32), ((0, 0), (1, 0), (0, 0))), axis=1)
    pos = start_pos + jnp.arange(L, dtype=jnp.int32)
    hi = POOL_STATE + 1
    parts = []
    for g, w in enumerate(POOL_WINDOWS):
        sl = slice(g * POOL_GROUP_DIM, (g + 1) * POOL_GROUP_DIM)
        s = cs[:, hi:hi + L, sl] - cs[:, hi - w:hi - w + L, sl]
        cnt = jnp.minimum(w, pos + 1).astype(jnp.float32)[None, :, None]
        parts.append(s / cnt)
    pooled = jnp.concatenate(parts, axis=-1) - a.astype(jnp.float32)
    pg = pooled.astype(a.dtype).reshape(bn, L, POOL_GROUPS, POOL_GROUP_DIM)
    out = jnp.einsum('blgc,gcd->blgd', pg, w_grp).reshape(bn, L, MIX_WIDTH) * scale
    return out, ext[:, -POOL_STATE:]


def spatial_gate(u, v, w_s, b_s):
    bn, L, W = v.shape
    lc = min(L, CHUNK)
    n = -(-L // lc)
    pad = n * lc - L
    vp = jnp.pad(v, ((0, 0), (0, pad), (0, 0))).reshape(bn, n, lc, SG_HEADS, SG_HEAD_DIM)
    mask = jnp.tril(jnp.ones((lc, lc), dtype=bool))
    ws = jnp.where(mask[None], w_s[:, :lc, :lc], 0)
    s = jnp.einsum('hts,bnshd->bnthd', ws, vp) + b_s[:, :lc].T[None, None, :, :, None]
    s = s.reshape(bn, n * lc, W)[:, :L]
    return u * s


def short_conv(c_in, c_b, c_c, prev, conv_w):
    z = c_c * c_in
    ext = jnp.concatenate([prev.astype(z.dtype), z], axis=1)
    L = z.shape[1]
    y = conv_w[0] * ext[:, 0:L]
    for k in range(1, CONV_K):
        y = y + conv_w[k] * ext[:, k:k + L]
    return c_b * y, ext[:, -CONV_STATE:]


def mixer_block(xn, pool_prev, conv_prev, start_pos, w_in, w_pool_group, pool_scale,
                sg_norm_g, sg_norm_b, w_spatial, b_spatial, conv_w, w_branch, w_out):
    proj = xn @ w_in
    idx = [MIX_WIDTH * i for i in range(1, 7)]
    a, u, v, c_in, c_b, c_c, gl = jnp.split(proj, idx, axis=-1)
    u = jax.nn.gelu(u)
    v = layernorm(jax.nn.gelu(v), sg_norm_g, sg_norm_b)
    out_a, pool_new = pool_mixer(a, pool_prev, start_pos, w_pool_group, pool_scale)
    out_b = spatial_gate(u, v, w_spatial, b_spatial)
    out_c, conv_new = short_conv(c_in, c_b, c_c, conv_prev, conv_w)
    gates = jax.nn.sigmoid(gl.astype(jnp.float32)).astype(xn.dtype)
    g_a, g_b, g_c = jnp.split(gates, N_BRANCH, axis=-1)
    merged = (g_a * (out_a @ w_branch[0]) + g_b * (out_b @ w_branch[1])
              + g_c * (out_c @ w_branch[2]))
    return merged @ w_out, pool_new, conv_new, v


def grouped_experts(xt, eid, gate, w_ei, w_eo):
    T, D = xt.shape
    A = T * TOP_K
    e_flat = eid.reshape(A)
    tok = (jnp.arange(A, dtype=jnp.int32) // TOP_K)
    g_flat = gate.reshape(A)
    order = jnp.argsort(e_flat)
    e_s, tok_s, g_s = e_flat[order], tok[order], g_flat[order]
    counts = jnp.bincount(e_flat, length=N_EXPERTS)
    padded = ((counts + EXPERT_BLOCK - 1) // EXPERT_BLOCK) * EXPERT_BLOCK
    pad_end = jnp.cumsum(padded)
    pad_start = pad_end - padded
    start = jnp.cumsum(counts) - counts
    dest = pad_start[e_s] + (jnp.arange(A, dtype=jnp.int32) - start[e_s])
    n_blk = -(-A // EXPERT_BLOCK) + N_EXPERTS
    slot_tok = jnp.full((n_blk * EXPERT_BLOCK,), T, dtype=jnp.int32).at[dest].set(tok_s)
    blk_e = jnp.minimum(jnp.searchsorted(pad_end, jnp.arange(n_blk) * EXPERT_BLOCK, side='right'),
                        N_EXPERTS - 1)
    x_pad = jnp.concatenate([xt, jnp.zeros((1, D), xt.dtype)], axis=0)
    xb = x_pad[slot_tok].reshape(n_blk, EXPERT_BLOCK, D)

    def expert_block(args):
        xblk, e = args
        h = xblk @ w_ei[e]
        h1, h3 = jnp.split(h, 2, axis=-1)
        return (jax.nn.silu(h1) * h3) @ w_eo[e]

    yb = lax.map(expert_block, (xb, blk_e)).reshape(n_blk * EXPERT_BLOCK, D)
    y_as = yb[dest] * g_s[:, None].astype(yb.dtype)
    return jax.ops.segment_sum(y_as, tok_s, num_segments=T)


def hier_moe(x, w_rg, b_rg, w_re, b_re, w_ei, w_eo):
    shp = x.shape
    xt = x.reshape(-1, D_MODEL)
    T = xt.shape[0]
    glog = (xt @ w_rg).astype(jnp.float32) + b_rg.astype(jnp.float32)
    gprob = jax.nn.softmax(glog, axis=-1)
    gsel = jnp.argmax(glog, axis=-1).astype(jnp.int32)
    gp = jnp.take_along_axis(gprob, gsel[:, None], axis=1)[:, 0]
    elog = ((xt @ w_re).astype(jnp.float32) + b_re.astype(jnp.float32)).reshape(T, N_EXPERT_GROUPS, EXPERTS_PER_GROUP)
    elog_g = jnp.take_along_axis(elog, gsel[:, None, None], axis=1)[:, 0]
    top_v, top_i = lax.top_k(elog_g, TOP_K)
    gate = gp[:, None] * jax.nn.softmax(top_v, axis=-1)
    eid = gsel[:, None] * EXPERTS_PER_GROUP + top_i.astype(jnp.int32)
    return grouped_experts(xt, eid, gate, w_ei, w_eo).reshape(shp)


def layer(x, pool_prev, conv_prev, start_pos, norm_mix, w_in, w_pool_group, pool_scale,
          sg_norm_g, sg_norm_b, w_spatial, b_spatial, conv_w, w_branch, w_out, norm_ffn,
          w_rg, b_rg, w_re, b_re, w_ei, w_eo):
    h, pool_new, conv_new, v = mixer_block(rmsnorm(x, norm_mix), pool_prev, conv_prev, start_pos,
                                           w_in, w_pool_group, pool_scale, sg_norm_g, sg_norm_b,
                                           w_spatial, b_spatial, conv_w, w_branch, w_out)
    x = x + h
    x = x + hier_moe(rmsnorm(x, norm_ffn), w_rg, b_rg, w_re, b_re, w_ei, w_eo)
    return x, pool_new, conv_new, v


def setup_inputs(seed: int = 0) -> dict:
    key = jax.random.key(seed)
    ks = jax.random.split(key, 24)
    f32 = jnp.float32

    def nrm(k, shape, scale):
        return jax.random.normal(k, shape, f32) * scale

    return {
        "x_prompt": nrm(ks[0], (BATCH, SEQ, D_MODEL), 1.0),
        "x_sample": nrm(ks[1], (DEC_BATCH, DEC_SEQ, D_MODEL), 1.0),
        "state_pool": nrm(ks[2], (DEPTH, DEC_BATCH, POOL_STATE, MIX_WIDTH), 1.0),
        "state_conv": nrm(ks[3], (DEPTH, DEC_BATCH, CONV_STATE, MIX_WIDTH), 1.0),
        "norm_mix": 1.0 + nrm(ks[4], (DEPTH, D_MODEL), 0.1),
        "w_in": nrm(ks[5], (DEPTH, D_MODEL, IN_COLS), D_MODEL ** -0.5),
        "w_pool_group": nrm(ks[6], (DEPTH, POOL_GROUPS, POOL_GROUP_DIM, POOL_GROUP_DIM), POOL_GROUP_DIM ** -0.5),
        "pool_scale": 1.0 + nrm(ks[7], (DEPTH, MIX_WIDTH), 0.1),
        "sg_norm_g": 1.0 + nrm(ks[8], (DEPTH, MIX_WIDTH), 0.1),
        "sg_norm_b": nrm(ks[9], (DEPTH, MIX_WIDTH), 0.02),
        "w_spatial": nrm(ks[10], (DEPTH, SG_HEADS, CHUNK, CHUNK), CHUNK ** -0.5),
        "b_spatial": 1.0 + nrm(ks[11], (DEPTH, SG_HEADS, CHUNK), 0.1),
        "conv_w": nrm(ks[12], (DEPTH, CONV_K, MIX_WIDTH), CONV_K ** -0.5),
        "w_branch": nrm(ks[13], (DEPTH, N_BRANCH, MIX_WIDTH, D_MODEL), MIX_WIDTH ** -0.5),
        "w_out": nrm(ks[14], (DEPTH, D_MODEL, D_MODEL), D_MODEL ** -0.5),
        "norm_ffn": 1.0 + nrm(ks[15], (DEPTH, D_MODEL), 0.1),
        "w_router_group": nrm(ks[16], (DEPTH, D_MODEL, N_EXPERT_GROUPS), D_MODEL ** -0.5),
        "b_router_group": nrm(ks[17], (DEPTH, N_EXPERT_GROUPS), 0.01),
        "w_router_expert": nrm(ks[18], (DEPTH, D_MODEL, N_EXPERTS), D_MODEL ** -0.5),
        "b_router_expert": nrm(ks[19], (DEPTH, N_EXPERTS), 0.01),
        "w_expert_in": nrm(ks[20], (DEPTH, N_EXPERTS, D_MODEL, 2 * D_EXPERT), D_MODEL ** -0.5),
        "w_expert_out": nrm(ks[21], (DEPTH, N_EXPERTS, D_EXPERT, D_MODEL), D_EXPERT ** -0.5),
        "norm_final": 1.0 + nrm(ks[22], (D_MODEL,), 0.1),
    }


def reference(x_prompt, x_sample, state_pool, state_conv, norm_mix, w_in, w_pool_group, pool_scale,
              sg_norm_g, sg_norm_b, w_spatial, b_spatial, conv_w, w_branch, w_out, norm_ffn,
              w_router_group, b_router_group, w_router_expert, b_router_expert,
              w_expert_in, w_expert_out, norm_final):
    y_p, y_s = x_prompt, x_sample
    bp = x_prompt.shape[0]
    pool_zero = jnp.zeros((bp, POOL_STATE, MIX_WIDTH), x_prompt.dtype)
    conv_zero = jnp.zeros((bp, CONV_STATE, MIX_WIDTH), x_prompt.dtype)
    pool_p, pool_s, conv_p, conv_s, v_s = [], [], [], [], []
    for l in range(DEPTH):
        params = (norm_mix[l], w_in[l], w_pool_group[l], pool_scale[l], sg_norm_g[l], sg_norm_b[l],
                  w_spatial[l], b_spatial[l], conv_w[l], w_branch[l], w_out[l], norm_ffn[l],
                  w_router_group[l], b_router_group[l], w_router_expert[l], b_router_expert[l],
                  w_expert_in[l], w_expert_out[l])
        y_p, pp, cp, _ = layer(y_p, pool_zero, conv_zero, 0, *params)
        y_s, ps, cs, vs = layer(y_s, state_pool[l], state_conv[l], PAST_LEN, *params)
        pool_p.append(pp)
        pool_s.append(ps)
        conv_p.append(cp)
        conv_s.append(cs)
        v_s.append(vs)
    y_p = rmsnorm(y_p, norm_final)
    y_s = rmsnorm(y_s, norm_final)
    return (y_p, y_s, jnp.stack(pool_p), jnp.stack(pool_s), jnp.stack(conv_p), jnp.stack(conv_s), jnp.stack(v_s))
```

```python
import functools

import jax
import jax.numpy as jnp
from jax import lax
from jax.experimental import pallas as pl
from jax.experimental.pallas import tpu as pltpu

F32 = jnp.float32
BF16 = jnp.bfloat16

D_MODEL = 4096
MIX_WIDTH = D_MODEL // 2
POOL_GROUPS = 4
POOL_WINDOWS = (2, 4, 8, 16)
POOL_GROUP_DIM = MIX_WIDTH // POOL_GROUPS
POOL_STATE = 15
SG_HEADS = 8
SG_HEAD_DIM = MIX_WIDTH // SG_HEADS
CHUNK = 128
CONV_STATE = 2
IN_COLS = MIX_WIDTH * 6 + 3 * D_MODEL
GATE_COL0 = MIX_WIDTH * 6
N_EXPERT_GROUPS = 4
EXPERTS_PER_GROUP = 8
N_EXPERTS = N_EXPERT_GROUPS * EXPERTS_PER_GROUP
TOP_K = 2
D_EXPERT = D_MODEL // 4
PAST_LEN = 16384
EPS = 1e-6

LANES = 128
SUBLANES = 8
POOL_HALO = 16
CONV_HALO = 8
ROW_TILE = 128
EXPERT_ROWS = 256
VMEM_LIMIT = 56 * 1024 * 1024


def _cparams(sem):
    return pltpu.CompilerParams(dimension_semantics=sem, vmem_limit_bytes=VMEM_LIMIT)


def _gelu_tanh(x):
    c = 0.7978845608028654
    return 0.5 * x * (1.0 + jnp.tanh(c * (x + 0.044715 * (x * x * x))))


def _rms(x, g):
    return x * lax.rsqrt(jnp.mean(x * x, axis=-1, keepdims=True) + EPS) * g


def _norm_in_kernel(n_prompt_tiles, xp_ref, xs_ref, g_ref, o_ref):
    i = pl.program_id(0)

    @pl.when(i < n_prompt_tiles)
    def _():
        o_ref[...] = _rms(xp_ref[...], g_ref[...]).astype(BF16)

    @pl.when(i >= n_prompt_tiles)
    def _():
        o_ref[...] = _rms(xs_ref[...], g_ref[...]).astype(BF16)


def _norm_in(xp, xs, g, tm=256):
    tp, ts = xp.shape[0], xs.shape[0]
    npt, nst = tp // tm, ts // tm
    return pl.pallas_call(
        functools.partial(_norm_in_kernel, npt),
        grid=(npt + nst,),
        in_specs=[
            pl.BlockSpec((tm, D_MODEL), lambda i: (jnp.minimum(i, npt - 1), 0)),
            pl.BlockSpec((tm, D_MODEL), lambda i: (jnp.maximum(i - npt, 0), 0)),
            pl.BlockSpec((1, D_MODEL), lambda i: (0, 0)),
        ],
        out_specs=pl.BlockSpec((tm, D_MODEL), lambda i: (i, 0)),
        out_shape=jax.ShapeDtypeStruct((tp + ts, D_MODEL), BF16),
        compiler_params=_cparams(("arbitrary",)),
        name="norm_in",
    )(xp, xs, g)


def _panel_matmul_kernel(a_ref, w_ref, o_ref, wbf_ref):
    @pl.when(pl.program_id(1) == 0)
    def _():
        wbf_ref[...] = w_ref[...].astype(BF16)

    o_ref[...] = jnp.dot(a_ref[...], wbf_ref[...], preferred_element_type=F32)


def _in_proj(xn, w_in, layer, tm=512, tn=1024):
    t = xn.shape[0]
    return pl.pallas_call(
        _panel_matmul_kernel,
        grid=(IN_COLS // tn, t // tm),
        in_specs=[
            pl.BlockSpec((tm, D_MODEL), lambda j, i: (i, 0)),
            pl.BlockSpec((None, D_MODEL, tn), lambda j, i: (layer, 0, j)),
        ],
        out_specs=pl.BlockSpec((tm, tn), lambda j, i: (i, j)),
        out_shape=jax.ShapeDtypeStruct((t, IN_COLS), F32),
        scratch_shapes=[pltpu.VMEM((D_MODEL, tn), BF16)],
        compiler_params=_cparams(("arbitrary", "arbitrary")),
        name="in_proj",
    )(xn, w_in)


def _window_rows(ext, n_seq, halo, rows):
    if n_seq == 1:
        return ext[halo:, :]
    c = ext.shape[-1]
    return ext.reshape(n_seq, halo + rows, c)[:, halo:, :].reshape(n_seq * rows, c)


def _stack_history(hist, cur, n_seq, halo, rows):
    if n_seq == 1:
        return jnp.concatenate([hist, cur], axis=0)
    c = cur.shape[-1]
    ext = jnp.concatenate([hist.reshape(n_seq, halo, c), cur.reshape(n_seq, rows, c)], axis=1)
    return ext.reshape(n_seq * (halo + rows), c)


def _branch_math(n_seq, rows, pos, a_ref, u_ref, v_ref, ci_ref, cb_ref, cc_ref, pool_hist, conv_hist,
                 wg_ref, ps_ref, lng_ref, lnb_ref, m_ref, bias_ref, cw_ref, oa_ref, ob_ref, oc_ref):
    for g, w in enumerate(POOL_WINDOWS):
        cols = slice(g * POOL_GROUP_DIM, (g + 1) * POOL_GROUP_DIM)
        a_g = a_ref[:, cols]
        s = _stack_history(pool_hist[:, cols], a_g, n_seq, POOL_HALO, rows)
        k = 1
        while k < w:
            s = s + pltpu.roll(s, k, 0)
            k *= 2
        s = _window_rows(s, n_seq, POOL_HALO, rows)
        cnt = jnp.minimum(w, pos + 1).astype(F32)
        pooled = (s / cnt - a_g).astype(BF16)
        out = jnp.dot(pooled, wg_ref[g], preferred_element_type=F32) * ps_ref[:, cols]
        oa_ref[:, cols] = out.astype(BF16)

    vg = _gelu_tanh(v_ref[...])
    mu = jnp.mean(vg, axis=-1, keepdims=True)
    xc = vg - mu
    v = xc * lax.rsqrt(jnp.mean(xc * xc, axis=-1, keepdims=True) + EPS) * lng_ref[...] + lnb_ref[...]
    vb = v.astype(BF16)
    for h in range(SG_HEADS):
        cols = slice(h * SG_HEAD_DIM, (h + 1) * SG_HEAD_DIM)
        sp = jnp.dot(m_ref[h], vb[:, cols], preferred_element_type=F32) + bias_ref[:, cols]
        ob_ref[:, cols] = (_gelu_tanh(u_ref[:, cols]) * sp).astype(BF16)

    z = cc_ref[...] * ci_ref[...]
    e = _stack_history(conv_hist, z, n_seq, CONV_HALO, rows)
    y = cw_ref[0:1, :] * pltpu.roll(e, 2, 0)
    y = y + cw_ref[1:2, :] * pltpu.roll(e, 1, 0)
    y = y + cw_ref[2:3, :] * e
    oc_ref[...] = (cb_ref[...] * _window_rows(y, n_seq, CONV_HALO, rows)).astype(BF16)
    return v, z


def _branch_kernel(tiles_per_seq, n_prompt_tiles, seq_per_tile, sample_len,
                   a_ref, u_ref, v_ref, ci_ref, cb_ref, cc_ref, ph_ref, ch_ref, wg_ref, ps_ref, lng_ref, lnb_ref,
                   m_ref, bias_ref, cw_ref, oa_ref, ob_ref, oc_ref, zl_ref, vo_ref, zo_ref, pool_hist, conv_hist):
    i = pl.program_id(0)
    shared = (wg_ref, ps_ref, lng_ref, lnb_ref, m_ref, bias_ref, cw_ref, oa_ref, ob_ref, oc_ref)

    @pl.when(i < n_prompt_tiles)
    def _():
        t = i % tiles_per_seq

        @pl.when(t == 0)
        def _():
            pool_hist[...] = jnp.zeros_like(pool_hist)
            conv_hist[...] = jnp.zeros_like(conv_hist)

        pos = t * ROW_TILE + lax.broadcasted_iota(jnp.int32, (ROW_TILE, 1), 0)
        _, z = _branch_math(1, ROW_TILE, pos, a_ref, u_ref, v_ref, ci_ref, cb_ref, cc_ref, pool_hist[...],
                            conv_hist[...], *shared)
        pool_hist[...] = a_ref[ROW_TILE - POOL_HALO:, :]
        conv_hist[...] = z[ROW_TILE - CONV_HALO:, :]
        zl_ref[...] = z[ROW_TILE - CONV_HALO:, :]

    @pl.when(i >= n_prompt_tiles)
    def _():
        r = lax.broadcasted_iota(jnp.int32, (seq_per_tile * sample_len, 1), 0)
        pos = PAST_LEN + (r % sample_len)
        v, z = _branch_math(seq_per_tile, sample_len, pos, a_ref, u_ref, v_ref, ci_ref, cb_ref, cc_ref,
                            ph_ref[...], ch_ref[...], *shared)
        vo_ref[...] = v
        zo_ref[...] = z


def _branches(proj, layer, n_prompt_seq, prompt_len, n_sample_seq, sample_len, pool_state, conv_state,
              wg_bf, pool_scale, ln_g, ln_b, gate_mats, gate_bias, conv_w):
    t_total = proj.shape[0]
    tiles_per_seq = prompt_len // ROW_TILE
    npt = n_prompt_seq * tiles_per_seq
    seq_per_tile = ROW_TILE // sample_len
    nst = n_sample_seq // seq_per_tile
    sample_rows = n_sample_seq * sample_len
    act = jax.ShapeDtypeStruct((t_total, MIX_WIDTH), BF16)
    sample_f32 = jax.ShapeDtypeStruct((sample_rows, MIX_WIDTH), F32)

    def sample_tile(i):
        return jnp.maximum(i - npt, 0)

    def kind(i):
        return jnp.where(i >= npt, 1, 0)

    slab_specs = [pl.BlockSpec((ROW_TILE, MIX_WIDTH), functools.partial(lambda c, i: (i, c), c)) for c in range(6)]
    vec_spec = pl.BlockSpec((None, 1, MIX_WIDTH), lambda i: (layer, 0, 0))
    in_specs = slab_specs + [
        pl.BlockSpec((None, seq_per_tile * POOL_HALO, MIX_WIDTH), lambda i: (layer, sample_tile(i), 0)),
        pl.BlockSpec((None, seq_per_tile * CONV_HALO, MIX_WIDTH), lambda i: (layer, sample_tile(i), 0)),
        pl.BlockSpec((None, POOL_GROUPS, POOL_GROUP_DIM, POOL_GROUP_DIM), lambda i: (layer, 0, 0, 0)),
        vec_spec, vec_spec, vec_spec,
        pl.BlockSpec((None, None, SG_HEADS, CHUNK, CHUNK), lambda i: (layer, kind(i), 0, 0, 0)),
        pl.BlockSpec((None, None, CHUNK, MIX_WIDTH), lambda i: (layer, kind(i), 0, 0)),
        pl.BlockSpec((None, 3, MIX_WIDTH), lambda i: (layer, 0, 0)),
    ]
    row_spec = pl.BlockSpec((ROW_TILE, MIX_WIDTH), lambda i: (i, 0))
    out_specs = [
        row_spec, row_spec, row_spec,
        pl.BlockSpec((None, CONV_HALO, MIX_WIDTH), lambda i: (jnp.minimum(i // tiles_per_seq, n_prompt_seq - 1), 0, 0)),
        pl.BlockSpec((ROW_TILE, MIX_WIDTH), lambda i: (sample_tile(i), 0)),
        pl.BlockSpec((ROW_TILE, MIX_WIDTH), lambda i: (sample_tile(i), 0)),
    ]
    return pl.pallas_call(
        functools.partial(_branch_kernel, tiles_per_seq, npt, seq_per_tile, sample_len),
        grid=(npt + nst,),
        in_specs=in_specs,
        out_specs=out_specs,
        out_shape=[act, act, act, jax.ShapeDtypeStruct((n_prompt_seq, CONV_HALO, MIX_WIDTH), F32),
                   sample_f32, sample_f32],
        scratch_shapes=[pltpu.VMEM((POOL_HALO, MIX_WIDTH), F32), pltpu.VMEM((CONV_HALO, MIX_WIDTH), F32)],
        compiler_params=_cparams(("arbitrary",)),
        name="branches",
    )(proj, proj, proj, proj, proj, proj, pool_state, conv_state, wg_bf, pool_scale, ln_g, ln_b,
      gate_mats, gate_bias, conv_w)


def _merge_kernel(oa_ref, ob_ref, oc_ref, wa_ref, wb_ref, wc_ref, ga_ref, gb_ref, gc_ref, o_ref, wbf_ref):
    @pl.when(pl.program_id(1) == 0)
    def _():
        wbf_ref[0] = wa_ref[...].astype(BF16)
        wbf_ref[1] = wb_ref[...].astype(BF16)
        wbf_ref[2] = wc_ref[...].astype(BF16)

    m = jax.nn.sigmoid(ga_ref[...]) * jnp.dot(oa_ref[...], wbf_ref[0], preferred_element_type=F32)
    m = m + jax.nn.sigmoid(gb_ref[...]) * jnp.dot(ob_ref[...], wbf_ref[1], preferred_element_type=F32)
    m = m + jax.nn.sigmoid(gc_ref[...]) * jnp.dot(oc_ref[...], wbf_ref[2], preferred_element_type=F32)
    o_ref[...] = m.astype(BF16)


def _merge(oa, ob, oc, w_branch, proj, layer, tm=512, tn=512):
    t = oa.shape[0]
    act_spec = pl.BlockSpec((tm, MIX_WIDTH), lambda j, i: (i, 0))

    def wspec(b):
        return pl.BlockSpec((None, None, MIX_WIDTH, tn), lambda j, i: (layer, b, 0, j))

    def gspec(b):
        off = (GATE_COL0 + b * D_MODEL) // tn
        return pl.BlockSpec((tm, tn), lambda j, i: (i, off + j))

    return pl.pallas_call(
        _merge_kernel,
        grid=(D_MODEL // tn, t // tm),
        in_specs=[act_spec, act_spec, act_spec, wspec(0), wspec(1), wspec(2), gspec(0), gspec(1), gspec(2)],
        out_specs=pl.BlockSpec((tm, tn), lambda j, i: (i, j)),
        out_shape=jax.ShapeDtypeStruct((t, D_MODEL), BF16),
        scratch_shapes=[pltpu.VMEM((3, MIX_WIDTH, tn), BF16)],
        compiler_params=_cparams(("arbitrary", "arbitrary")),
        name="merge",
    )(oa, ob, oc, w_branch, w_branch, w_branch, proj, proj, proj)


def _out_kernel2(n_prompt_tiles, a_ref, w_ref, xp_ref, xs_ref, o_ref, wbf_ref):
    i = pl.program_id(1)

    @pl.when(i == 0)
    def _():
        wbf_ref[...] = w_ref[...].astype(BF16)

    h = jnp.dot(a_ref[...], wbf_ref[...], preferred_element_type=F32)

    @pl.when(i < n_prompt_tiles)
    def _():
        o_ref[...] = xp_ref[...] + h

    @pl.when(i >= n_prompt_tiles)
    def _():
        o_ref[...] = xs_ref[...] + h


def _out_kernel1(a_ref, w_ref, x_ref, o_ref, wbf_ref):
    @pl.when(pl.program_id(1) == 0)
    def _():
        wbf_ref[...] = w_ref[...].astype(BF16)

    o_ref[...] = x_ref[...] + jnp.dot(a_ref[...], wbf_ref[...], preferred_element_type=F32)


def _out_proj(merged, w_out, layer, x_parts, tm=512, tn=512):
    t = merged.shape[0]
    common = dict(
        grid=(D_MODEL // tn, t // tm),
        out_specs=pl.BlockSpec((tm, tn), lambda j, i: (i, j)),
        out_shape=jax.ShapeDtypeStruct((t, D_MODEL), F32),
        scratch_shapes=[pltpu.VMEM((D_MODEL, tn), BF16)],
        compiler_params=_cparams(("arbitrary", "arbitrary")),
        name="out_proj",
    )
    a_spec = pl.BlockSpec((tm, D_MODEL), lambda j, i: (i, 0))
    w_spec = pl.BlockSpec((None, D_MODEL, tn), lambda j, i: (layer, 0, j))
    if len(x_parts) == 1:
        return pl.pallas_call(
            _out_kernel1,
            in_specs=[a_spec, w_spec, pl.BlockSpec((tm, tn), lambda j, i: (i, j))],
            **common,
        )(merged, w_out, x_parts[0])
    xp, xs = x_parts
    npt = xp.shape[0] // tm
    return pl.pallas_call(
        functools.partial(_out_kernel2, npt),
        in_specs=[
            a_spec, w_spec,
            pl.BlockSpec((tm, tn), lambda j, i: (jnp.minimum(i, npt - 1), j)),
            pl.BlockSpec((tm, tn), lambda j, i: (jnp.maximum(i - npt, 0), j)),
        ],
        **common,
    )(merged, w_out, xp, xs)


def _router_kernel(x_ref, g_ref, wr_ref, br_ref, xn_ref, eid_ref, gate_ref):
    xn = _rms(x_ref[...], g_ref[...])
    xn_ref[...] = xn
    logits = jnp.dot(xn.astype(BF16), wr_ref[...], preferred_element_type=F32) + br_ref[...]
    tm = logits.shape[0]
    lane = lax.broadcasted_iota(jnp.int32, (tm, LANES), 1)
    neg = jnp.float32(-jnp.inf)
    far = jnp.int32(LANES)

    is_g = lane < N_EXPERT_GROUPS
    glog = jnp.where(is_g, logits, neg)
    gmax = jnp.max(glog, axis=1, keepdims=True)
    gsel = jnp.min(jnp.where(glog == gmax, lane, far), axis=1, keepdims=True)
    gsum = jnp.sum(jnp.where(is_g, jnp.exp(glog - gmax), 0.0), axis=1, keepdims=True)
    gp = 1.0 / gsum

    lo = N_EXPERT_GROUPS + gsel * EXPERTS_PER_GROUP
    in_grp = jnp.logical_and(lane >= lo, lane < lo + EXPERTS_PER_GROUP)
    el = jnp.where(in_grp, logits, neg)
    m1 = jnp.max(el, axis=1, keepdims=True)
    i1 = jnp.min(jnp.where(el == m1, lane, far), axis=1, keepdims=True)
    el2 = jnp.where(lane == i1, neg, el)
    m2 = jnp.max(el2, axis=1, keepdims=True)
    i2 = jnp.min(jnp.where(el2 == m2, lane, far), axis=1, keepdims=True)
    e2 = jnp.exp(m2 - m1)
    den = 1.0 + e2
    g1 = gp * (1.0 / den)
    g2 = gp * (e2 / den)

    eid_ref[...] = jnp.where(lane == 0, i1 - N_EXPERT_GROUPS, jnp.where(lane == 1, i2 - N_EXPERT_GROUPS, 0))
    gate_ref[...] = jnp.where(lane == 0, g1, jnp.where(lane == 1, g2, 0.0))


def _router(x1, g, wr, br, tm=256):
    t = x1.shape[0]
    return pl.pallas_call(
        _router_kernel,
        grid=(t // tm,),
        in_specs=[
            pl.BlockSpec((tm, D_MODEL), lambda i: (i, 0)),
            pl.BlockSpec((1, D_MODEL), lambda i: (0, 0)),
            pl.BlockSpec((D_MODEL, LANES), lambda i: (0, 0)),
            pl.BlockSpec((1, LANES), lambda i: (0, 0)),
        ],
        out_specs=[
            pl.BlockSpec((tm, D_MODEL), lambda i: (i, 0)),
            pl.BlockSpec((tm, LANES), lambda i: (i, 0)),
            pl.BlockSpec((tm, LANES), lambda i: (i, 0)),
        ],
        out_shape=[
            jax.ShapeDtypeStruct((t, D_MODEL), F32),
            jax.ShapeDtypeStruct((t, LANES), jnp.int32),
            jax.ShapeDtypeStruct((t, LANES), F32),
        ],
        compiler_params=_cparams(("arbitrary",)),
        name="router",
    )(x1, g, wr, br)


def _dispatch_kernel(tm, dest_ref, x_hbm, xs_in, xs_hbm, sem):
    del xs_in
    base = pl.program_id(0) * tm

    def row_copy(r, k):
        t = base + r
        return pltpu.make_async_copy(x_hbm.at[pl.ds(t, 1)], xs_hbm.at[pl.ds(dest_ref[t * TOP_K + k], 1)], sem)

    def issue(r, c):
        row_copy(r, 0).start()
        row_copy(r, 1).start()
        return c

    lax.fori_loop(0, tm, issue, 0)

    def drain(r, c):
        row_copy(r, 0).wait()
        row_copy(r, 1).wait()
        return c

    lax.fori_loop(0, tm, drain, 0)


def _dispatch(xn, dest_flat, xs_init, tm=512):
    t = xn.shape[0]
    return pl.pallas_call(
        functools.partial(_dispatch_kernel, tm),
        grid_spec=pltpu.PrefetchScalarGridSpec(
            num_scalar_prefetch=1,
            grid=(t // tm,),
            in_specs=[pl.BlockSpec(memory_space=pl.ANY), pl.BlockSpec(memory_space=pl.ANY)],
            out_specs=pl.BlockSpec(memory_space=pl.ANY),
            scratch_shapes=[pltpu.SemaphoreType.DMA],
        ),
        out_shape=jax.ShapeDtypeStruct(xs_init.shape, F32),
        input_output_aliases={2: 0},
        compiler_params=_cparams(("arbitrary",)),
        name="dispatch",
    )(dest_flat, xn, xs_init)


def _new_expert(be_ref, b):
    return jnp.logical_or(b == 0, be_ref[b] != be_ref[jnp.maximum(b - 1, 0)])


def _expert_up_kernel(be_ref, nb_ref, x_ref, w1_ref, w3_ref, o_ref, w1bf, w3bf):
    b = pl.program_id(1)

    @pl.when(_new_expert(be_ref, b))
    def _():
        w1bf[...] = w1_ref[...].astype(BF16)
        w3bf[...] = w3_ref[...].astype(BF16)

    @pl.when(b < nb_ref[0])
    def _():
        x = x_ref[...].astype(BF16)
        h1 = jnp.dot(x, w1bf[...], preferred_element_type=F32)
        h3 = jnp.dot(x, w3bf[...], preferred_element_type=F32)
        o_ref[...] = (h1 * jax.nn.sigmoid(h1) * h3).astype(BF16)

    @pl.when(b >= nb_ref[0])
    def _():
        o_ref[...] = jnp.zeros_like(o_ref)


def _expert_up(xs, w_ei, layer, blk_e, n_blk, tf=512):
    cap = xs.shape[0]
    nb = cap // EXPERT_ROWS
    nf = D_EXPERT // tf

    def row(f, b, be, nbk):
        return (jnp.minimum(b, nbk[0] - 1), 0)

    return pl.pallas_call(
        _expert_up_kernel,
        grid_spec=pltpu.PrefetchScalarGridSpec(
            num_scalar_prefetch=2,
            grid=(nf, nb),
            in_specs=[
                pl.BlockSpec((EXPERT_ROWS, D_MODEL), row),
                pl.BlockSpec((None, None, D_MODEL, tf), lambda f, b, be, nbk: (layer, be[b], 0, f)),
                pl.BlockSpec((None, None, D_MODEL, tf), lambda f, b, be, nbk: (layer, be[b], 0, nf + f)),
            ],
            out_specs=pl.BlockSpec((EXPERT_ROWS, tf), lambda f, b, be, nbk: (b, f)),
            scratch_shapes=[pltpu.VMEM((D_MODEL, tf), BF16), pltpu.VMEM((D_MODEL, tf), BF16)],
        ),
        out_shape=jax.ShapeDtypeStruct((cap, D_EXPERT), BF16),
        compiler_params=_cparams(("arbitrary", "arbitrary")),
        name="expert_up",
    )(blk_e, n_blk, xs, w_ei, w_ei)


def _expert_down_kernel(be_ref, nb_ref, h_ref, w_ref, o_ref, wbf):
    b = pl.program_id(1)

    @pl.when(_new_expert(be_ref, b))
    def _():
        wbf[...] = w_ref[...].astype(BF16)

    @pl.when(b < nb_ref[0])
    def _():
        o_ref[...] = jnp.dot(h_ref[...], wbf[...], preferred_element_type=F32)

    @pl.when(b >= nb_ref[0])
    def _():
        o_ref[...] = jnp.zeros_like(o_ref)


def _expert_down(hs, w_eo, layer, blk_e, n_blk, tn=1024):
    cap = hs.shape[0]
    nb = cap // EXPERT_ROWS
    return pl.pallas_call(
        _expert_down_kernel,
        grid_spec=pltpu.PrefetchScalarGridSpec(
            num_scalar_prefetch=2,
            grid=(D_MODEL // tn, nb),
            in_specs=[
                pl.BlockSpec((EXPERT_ROWS, D_EXPERT), lambda n, b, be, nbk: (jnp.minimum(b, nbk[0] - 1), 0)),
                pl.BlockSpec((None, None, D_EXPERT, tn), lambda n, b, be, nbk: (layer, be[b], 0, n)),
            ],
            out_specs=pl.BlockSpec((EXPERT_ROWS, tn), lambda n, b, be, nbk: (b, n)),
            scratch_shapes=[pltpu.VMEM((D_EXPERT, tn), BF16)],
        ),
        out_shape=jax.ShapeDtypeStruct((cap, D_MODEL), F32),
        compiler_params=_cparams(("arbitrary", "arbitrary")),
        name="expert_down",
    )(blk_e, n_blk, hs, w_eo)


def _combine_kernel(tm, split, dest_ref, x_ref, gate_ref, g_ref, ys_hbm, *rest):
    if split is None:
        x2_ref, xn_ref, ybuf, sem = rest
    else:
        yp_ref, ysm_ref, ybuf, sem = rest
    i = pl.program_id(0)
    base = i * tm

    def row_copy(r, k):
        src = ys_hbm.at[pl.ds(dest_ref[(base + r) * TOP_K + k], 1)]
        return pltpu.make_async_copy(src, ybuf.at[k, pl.ds(r, 1)], sem)

    def issue(r, c):
        row_copy(r, 0).start()
        row_copy(r, 1).start()
        return c

    lax.fori_loop(0, tm, issue, 0)

    def drain(r, c):
        row_copy(r, 0).wait()
        row_copy(r, 1).wait()
        return c

    lax.fori_loop(0, tm, drain, 0)

    gates = gate_ref[...]
    x2 = x_ref[...] + (ybuf[0] * gates[:, 0:1] + ybuf[1] * gates[:, 1:2])
    if split is None:
        x2_ref[...] = x2
        xn_ref[...] = _rms(x2, g_ref[...]).astype(xn_ref.dtype)
    else:
        y = _rms(x2, g_ref[...])

        @pl.when(i < split)
        def _():
            yp_ref[...] = y

        @pl.when(i >= split)
        def _():
            ysm_ref[...] = y


def _combine(x1, gates, dest_flat, ys, g, final_split=None, tm=128):
    t = x1.shape[0]
    in_specs = [
        pl.BlockSpec((tm, D_MODEL), lambda i, d: (i, 0)),
        pl.BlockSpec((tm, LANES), lambda i, d: (i, 0)),
        pl.BlockSpec((1, D_MODEL), lambda i, d: (0, 0)),
        pl.BlockSpec(memory_space=pl.ANY),
    ]
    if final_split is None:
        split = None
        out_specs = [pl.BlockSpec((tm, D_MODEL), lambda i, d: (i, 0)),
                     pl.BlockSpec((tm, D_MODEL), lambda i, d: (i, 0))]
        out_shape = [jax.ShapeDtypeStruct((t, D_MODEL), F32), jax.ShapeDtypeStruct((t, D_MODEL), BF16)]
    else:
        split = final_split // tm
        out_specs = [pl.BlockSpec((tm, D_MODEL), lambda i, d: (jnp.minimum(i, split - 1), 0)),
                     pl.BlockSpec((tm, D_MODEL), lambda i, d: (jnp.maximum(i - split, 0), 0))]
        out_shape = [jax.ShapeDtypeStruct((final_split, D_MODEL), F32),
                     jax.ShapeDtypeStruct((t - final_split, D_MODEL), F32)]
    return pl.pallas_call(
        functools.partial(_combine_kernel, tm, split),
        grid_spec=pltpu.PrefetchScalarGridSpec(
            num_scalar_prefetch=1,
            grid=(t // tm,),
            in_specs=in_specs,
            out_specs=out_specs,
            scratch_shapes=[pltpu.VMEM((TOP_K, tm, D_MODEL), F32), pltpu.SemaphoreType.DMA],
        ),
        out_shape=out_shape,
        compiler_params=_cparams(("arbitrary",)),
        name="combine",
    )(dest_flat, x1, gates, g, ys)


def _routing_tables(eid, n_blocks):
    e_flat = eid.reshape(-1)
    onehot = (e_flat[:, None] == jnp.arange(N_EXPERTS, dtype=jnp.int32)[None, :]).astype(jnp.int32)
    csum = jnp.cumsum(onehot, axis=0)
    rank = jnp.sum(csum * onehot, axis=1) - 1
    counts = csum[-1]
    blocks = (counts + EXPERT_ROWS - 1) // EXPERT_ROWS
    blk_end = jnp.cumsum(blocks)
    seg_start = (blk_end - blocks) * EXPERT_ROWS
    dest = (seg_start[e_flat] + rank).astype(jnp.int32)
    blk_e = jnp.searchsorted(blk_end, jnp.arange(n_blocks, dtype=jnp.int32), side="right")
    n_used = blk_end[-1].astype(jnp.int32)
    last_e = jnp.take(blk_e, jnp.maximum(n_used - 1, 0))
    blk_e = jnp.where(jnp.arange(n_blocks) < n_used, blk_e, last_e).astype(jnp.int32)
    return dest, blk_e, n_used.reshape(1)


def _gate_tables(w_spatial, b_spatial, sample_len):
    depth = w_spatial.shape[0]
    tril = jnp.tril(jnp.ones((CHUNK, CHUNK), dtype=bool))
    full = jnp.where(tril[None, None], w_spatial, 0)
    reps = CHUNK // sample_len
    small = jnp.where(tril[None, None, :sample_len, :sample_len], w_spatial[:, :, :sample_len, :sample_len], 0)
    eye = jnp.eye(reps, dtype=w_spatial.dtype)
    blockdiag = jnp.einsum("ab,lhts->lhatbs", eye, small).reshape(depth, SG_HEADS, CHUNK, CHUNK)
    mats = jnp.stack([full, blockdiag], axis=1).astype(BF16)
    bias_full = jnp.transpose(b_spatial, (0, 2, 1))
    bias_small = jnp.tile(bias_full[:, :sample_len, :], (1, reps, 1))
    bias = jnp.stack([bias_full, bias_small], axis=1)
    bias = jnp.repeat(bias, SG_HEAD_DIM, axis=-1)
    return mats, bias


def kernel(x_prompt, x_sample, state_pool, state_conv, norm_mix, w_in, w_pool_group, pool_scale, sg_norm_g, sg_norm_b, w_spatial, b_spatial, conv_w, w_branch, w_out, norm_ffn, w_router_group, b_router_group, w_router_expert, b_router_expert, w_expert_in, w_expert_out, norm_final):
    depth = w_in.shape[0]
    bp, lp, _ = x_prompt.shape
    bs, ls, _ = x_sample.shape
    tp, ts = bp * lp, bs * ls
    t = tp + ts
    assert lp % ROW_TILE == 0 and ROW_TILE % ls == 0 and bs % (ROW_TILE // ls) == 0 and ls >= CONV_STATE

    xp = x_prompt.reshape(tp, D_MODEL)
    xs = x_sample.reshape(ts, D_MODEL)

    gate_mats, gate_bias = _gate_tables(w_spatial, b_spatial, ls)
    wg_bf = w_pool_group.astype(BF16)
    vec = lambda a: a.reshape(depth, 1, -1)
    wr = jnp.concatenate([w_router_group, w_router_expert,
                          jnp.zeros((depth, D_MODEL, LANES - N_EXPERT_GROUPS - N_EXPERTS), F32)], axis=-1).astype(BF16)
    br = jnp.concatenate([b_router_group, b_router_expert,
                          jnp.zeros((depth, LANES - N_EXPERT_GROUPS - N_EXPERTS), F32)], axis=-1).reshape(depth, 1, LANES)
    pool_hist = jnp.pad(state_pool, ((0, 0), (0, 0), (POOL_HALO - POOL_STATE, 0), (0, 0))).reshape(
        depth, bs * POOL_HALO, MIX_WIDTH)
    conv_hist = jnp.pad(state_conv, ((0, 0), (0, 0), (CONV_HALO - CONV_STATE, 0), (0, 0))).reshape(
        depth, bs * CONV_HALO, MIX_WIDTH)

    n_assign = t * TOP_K
    n_blocks = n_assign // EXPERT_ROWS + N_EXPERTS
    cap = n_blocks * EXPERT_ROWS

    pool_p, pool_s, conv_p, conv_s, v_out = [], [], [], [], []
    x_parts = (xp, xs)
    xn = _norm_in(xp, xs, norm_mix[0].reshape(1, D_MODEL))
    y_p = y_s = None
    for l in range(depth):
        proj = _in_proj(xn, w_in, l)
        oa, ob, oc, z_tail, v_s, z_s = _branches(
            proj, l, bp, lp, bs, ls, pool_hist, conv_hist, wg_bf, vec(pool_scale), vec(sg_norm_g),
            vec(sg_norm_b), gate_mats, gate_bias, conv_w)
        merged = _merge(oa, ob, oc, w_branch, proj, l)
        x1 = _out_proj(merged, w_out, l, x_parts)

        xn2, eid, gates = _router(x1, norm_ffn[l].reshape(1, D_MODEL), wr[l], br[l])
        dest, blk_e, n_used = _routing_tables(eid[:, :TOP_K], n_blocks)
        xs_sorted = _dispatch(xn2, dest, jnp.zeros((cap, D_MODEL), F32))
        hs = _expert_up(xs_sorted, w_expert_in, l, blk_e, n_used)
        ys = _expert_down(hs, w_expert_out, l, blk_e, n_used)
        if l + 1 < depth:
            x2, xn = _combine(x1, gates, dest, ys, norm_mix[l + 1].reshape(1, D_MODEL))
            x_parts = (x2,)
        else:
            y_p, y_s = _combine(x1, gates, dest, ys, norm_final.reshape(1, D_MODEL), final_split=tp)

        a_p = proj[:tp, :MIX_WIDTH].reshape(bp, lp, MIX_WIDTH)
        a_s = proj[tp:, :MIX_WIDTH].reshape(bs, ls, MIX_WIDTH)
        pool_p.append(a_p[:, lp - POOL_STATE:])
        pool_s.append(jnp.concatenate([state_pool[l], a_s], axis=1)[:, -POOL_STATE:])
        conv_p.append(z_tail[:, CONV_HALO - CONV_STATE:])
        conv_s.append(z_s.reshape(bs, ls, MIX_WIDTH)[:, ls - CONV_STATE:])
        v_out.append(v_s.reshape(bs, ls, MIX_WIDTH))

    return (y_p.reshape(bp, lp, D_MODEL), y_s.reshape(bs, ls, D_MODEL), jnp.stack(pool_p), jnp.stack(pool_s),
            jnp.stack(conv_p), jnp.stack(conv_s), jnp.stack(v_out))
```

```python
import functools

import jax
import jax.numpy as jnp
from jax import lax
from jax.experimental import pallas as pl
from jax.experimental.pallas import tpu as pltpu

F32 = jnp.float32
BF16 = jnp.bfloat16

D_MODEL = 4096
MIX_WIDTH = D_MODEL // 2
POOL_GROUPS = 4
POOL_WINDOWS = (2, 4, 8, 16)
POOL_GROUP_DIM = MIX_WIDTH // POOL_GROUPS
POOL_STATE = 15
SG_HEADS = 8
SG_HEAD_DIM = MIX_WIDTH // SG_HEADS
CHUNK = 128
CONV_STATE = 2
IN_COLS = MIX_WIDTH * 6 + 3 * D_MODEL
GATE_COL0 = MIX_WIDTH * 6
N_EXPERT_GROUPS = 4
EXPERTS_PER_GROUP = 8
N_EXPERTS = N_EXPERT_GROUPS * EXPERTS_PER_GROUP
TOP_K = 2
D_EXPERT = D_MODEL // 4
PAST_LEN = 16384
EPS = 1e-6

LANES = 128
SUBLANES = 8
POOL_HALO = 16
CONV_HALO = 8
ROW_TILE = 128
EXPERT_ROWS = 256
VMEM_LIMIT = 56 * 1024 * 1024


def _cparams(sem):
    return pltpu.CompilerParams(dimension_semantics=sem, vmem_limit_bytes=VMEM_LIMIT)


def _gelu_tanh(x):
    c = 0.7978845608028654
    return 0.5 * x * (1.0 + jnp.tanh(c * (x + 0.044715 * (x * x * x))))


def _rms(x, g):
    return x * lax.rsqrt(jnp.mean(x * x, axis=-1, keepdims=True) + EPS) * g


def _norm_in_kernel(n_prompt_tiles, xp_ref, xs_ref, g_ref, o_ref):
    i = pl.program_id(0)

    @pl.when(i < n_prompt_tiles)
    def _():
        o_ref[...] = _rms(xp_ref[...], g_ref[...]).astype(BF16)

    @pl.when(i >= n_prompt_tiles)
    def _():
        o_ref[...] = _rms(xs_ref[...], g_ref[...]).astype(BF16)


def _norm_in(xp, xs, g, tm=256):
    tp, ts = xp.shape[0], xs.shape[0]
    npt, nst = tp // tm, ts // tm
    return pl.pallas_call(
        functools.partial(_norm_in_kernel, npt),
        grid=(npt + nst,),
        in_specs=[
            pl.BlockSpec((tm, D_MODEL), lambda i: (jnp.minimum(i, npt - 1), 0)),
            pl.BlockSpec((tm, D_MODEL), lambda i: (jnp.maximum(i - npt, 0), 0)),
            pl.BlockSpec((1, D_MODEL), lambda i: (0, 0)),
        ],
        out_specs=pl.BlockSpec((tm, D_MODEL), lambda i: (i, 0)),
        out_shape=jax.ShapeDtypeStruct((tp + ts, D_MODEL), BF16),
        compiler_params=_cparams(("arbitrary",)),
        name="norm_in",
    )(xp, xs, g)


def _panel_matmul_kernel(a_ref, w_ref, o_ref, wbf_ref):
    @pl.when(pl.program_id(1) == 0)
    def _():
        wbf_ref[...] = w_ref[...].astype(BF16)

    o_ref[...] = jnp.dot(a_ref[...], wbf_ref[...], preferred_element_type=F32)


def _in_proj(xn, w_in, layer, tm=512, tn=1024):
    t = xn.shape[0]
    return pl.pallas_call(
        _panel_matmul_kernel,
        grid=(IN_COLS // tn, t // tm),
        in_specs=[
            pl.BlockSpec((tm, D_MODEL), lambda j, i: (i, 0)),
            pl.BlockSpec((None, D_MODEL, tn), lambda j, i: (layer, 0, j)),
        ],
        out_specs=pl.BlockSpec((tm, tn), lambda j, i: (i, j)),
        out_shape=jax.ShapeDtypeStruct((t, IN_COLS), F32),
        scratch_shapes=[pltpu.VMEM((D_MODEL, tn), BF16)],
        compiler_params=_cparams(("arbitrary", "arbitrary")),
        name="in_proj",
    )(xn, w_in)


def _window_rows(ext, n_seq, halo, rows):
    if n_seq == 1:
        return ext[halo:, :]
    c = ext.shape[-1]
    return ext.reshape(n_seq, halo + rows, c)[:, halo:, :].reshape(n_seq * rows, c)


def _stack_history(hist, cur, n_seq, halo, rows):
    if n_seq == 1:
        return jnp.concatenate([hist, cur], axis=0)
    c = cur.shape[-1]
    ext = jnp.concatenate([hist.reshape(n_seq, halo, c), cur.reshape(n_seq, rows, c)], axis=1)
    return ext.reshape(n_seq * (halo + rows), c)


def _branch_math(n_seq, rows, pos, a_ref, u_ref, v_ref, ci_ref, cb_ref, cc_ref, pool_hist, conv_hist,
                 wg_ref, ps_ref, lng_ref, lnb_ref, m_ref, bias_ref, cw_ref, oa_ref, ob_ref, oc_ref):
    for g, w in enumerate(POOL_WINDOWS):
        cols = slice(g * POOL_GROUP_DIM, (g + 1) * POOL_GROUP_DIM)
        a_g = a_ref[:, cols]
        s = _stack_history(pool_hist[:, cols], a_g, n_seq, POOL_HALO, rows)
        k = 1
        while k < w:
            s = s + pltpu.roll(s, k, 0)
            k *= 2
        s = _window_rows(s, n_seq, POOL_HALO, rows)
        cnt = jnp.minimum(w, pos + 1).astype(F32)
        pooled = (s / cnt - a_g).astype(BF16)
        out = jnp.dot(pooled, wg_ref[g], preferred_element_type=F32) * ps_ref[:, cols]
        oa_ref[:, cols] = out.astype(BF16)

    vg = _gelu_tanh(v_ref[...])
    mu = jnp.mean(vg, axis=-1, keepdims=True)
    xc = vg - mu
    v = xc * lax.rsqrt(jnp.mean(xc * xc, axis=-1, keepdims=True) + EPS) * lng_ref[...] + lnb_ref[...]
    vb = v.astype(BF16)
    for h in range(SG_HEADS):
        cols = slice(h * SG_HEAD_DIM, (h + 1) * SG_HEAD_DIM)
        sp = jnp.dot(m_ref[h], vb[:, cols], preferred_element_type=F32) + bias_ref[:, cols]
        ob_ref[:, cols] = (_gelu_tanh(u_ref[:, cols]) * sp).astype(BF16)

    z = cc_ref[...] * ci_ref[...]
    e = _stack_history(conv_hist, z, n_seq, CONV_HALO, rows)
    y = cw_ref[0:1, :] * pltpu.roll(e, 2, 0)
    y = y + cw_ref[1:2, :] * pltpu.roll(e, 1, 0)
    y = y + cw_ref[2:3, :] * e
    oc_ref[...] = (cb_ref[...] * _window_rows(y, n_seq, CONV_HALO, rows)).astype(BF16)
    return v, z


def _branch_kernel(tiles_per_seq, n_prompt_tiles, seq_per_tile, sample_len,
                   a_ref, u_ref, v_ref, ci_ref, cb_ref, cc_ref, ph_ref, ch_ref, wg_ref, ps_ref, lng_ref, lnb_ref,
                   m_ref, bias_ref, cw_ref, oa_ref, ob_ref, oc_ref, zl_ref, vo_ref, zo_ref, pool_hist, conv_hist):
    i = pl.program_id(0)
    shared = (wg_ref, ps_ref, lng_ref, lnb_ref, m_ref, bias_ref, cw_ref, oa_ref, ob_ref, oc_ref)

    @pl.when(i < n_prompt_tiles)
    def _():
        t = i % tiles_per_seq

        @pl.when(t == 0)
        def _():
            pool_hist[...] = jnp.zeros_like(pool_hist)
            conv_hist[...] = jnp.zeros_like(conv_hist)

        pos = t * ROW_TILE + lax.broadcasted_iota(jnp.int32, (ROW_TILE, 1), 0)
        _, z = _branch_math(1, ROW_TILE, pos, a_ref, u_ref, v_ref, ci_ref, cb_ref, cc_ref, pool_hist[...],
                            conv_hist[...], *shared)
        pool_hist[...] = a_ref[ROW_TILE - POOL_HALO:, :]
        conv_hist[...] = z[ROW_TILE - CONV_HALO:, :]
        zl_ref[...] = z[ROW_TILE - CONV_HALO:, :]

    @pl.when(i >= n_prompt_tiles)
    def _():
        r = lax.broadcasted_iota(jnp.int32, (seq_per_tile * sample_len, 1), 0)
        pos = PAST_LEN + (r % sample_len)
        v, z = _branch_math(seq_per_tile, sample_len, pos, a_ref, u_ref, v_ref, ci_ref, cb_ref, cc_ref,
                            ph_ref[...], ch_ref[...], *shared)
        vo_ref[...] = v
        zo_ref[...] = z


def _branches(proj, layer, n_prompt_seq, prompt_len, n_sample_seq, sample_len, pool_state, conv_state,
              wg_bf, pool_scale, ln_g, ln_b, gate_mats, gate_bias, conv_w):
    t_total = proj.shape[0]
    tiles_per_seq = prompt_len // ROW_TILE
    npt = n_prompt_seq * tiles_per_seq
    seq_per_tile = ROW_TILE // sample_len
    nst = n_sample_seq // seq_per_tile
    sample_rows = n_sample_seq * sample_len
    act = jax.ShapeDtypeStruct((t_total, MIX_WIDTH), BF16)
    sample_f32 = jax.ShapeDtypeStruct((sample_rows, MIX_WIDTH), F32)

    def sample_tile(i):
        return jnp.maximum(i - npt, 0)

    def kind(i):
        return jnp.where(i >= npt, 1, 0)

    slab_specs = [pl.BlockSpec((ROW_TILE, MIX_WIDTH), functools.partial(lambda c, i: (i, c), c)) for c in range(6)]
    vec_spec = pl.BlockSpec((None, 1, MIX_WIDTH), lambda i: (layer, 0, 0))
    in_specs = slab_specs + [
        pl.BlockSpec((None, seq_per_tile * POOL_HALO, MIX_WIDTH), lambda i: (layer, sample_tile(i), 0)),
        pl.BlockSpec((None, seq_per_tile * CONV_HALO, MIX_WIDTH), lambda i: (layer, sample_tile(i), 0)),
        pl.BlockSpec((None, POOL_GROUPS, POOL_GROUP_DIM, POOL_GROUP_DIM), lambda i: (layer, 0, 0, 0)),
        vec_spec, vec_spec, vec_spec,
        pl.BlockSpec((None, None, SG_HEADS, CHUNK, CHUNK), lambda i: (layer, kind(i), 0, 0, 0)),
        pl.BlockSpec((None, None, CHUNK, MIX_WIDTH), lambda i: (layer, kind(i), 0, 0)),
        pl.BlockSpec((None, 3, MIX_WIDTH), lambda i: (layer, 0, 0)),
    ]
    row_spec = pl.BlockSpec((ROW_TILE, MIX_WIDTH), lambda i: (i, 0))
    out_specs = [
        row_spec, row_spec, row_spec,
        pl.BlockSpec((None, CONV_HALO, MIX_WIDTH), lambda i: (jnp.minimum(i // tiles_per_seq, n_prompt_seq - 1), 0, 0)),
        pl.BlockSpec((ROW_TILE, MIX_WIDTH), lambda i: (sample_tile(i), 0)),
        pl.BlockSpec((ROW_TILE, MIX_WIDTH), lambda i: (sample_tile(i), 0)),
    ]
    return pl.pallas_call(
        functools.partial(_branch_kernel, tiles_per_seq, npt, seq_per_tile, sample_len),
        grid=(npt + nst,),
        in_specs=in_specs,
        out_specs=out_specs,
        out_shape=[act, act, act, jax.ShapeDtypeStruct((n_prompt_seq, CONV_HALO, MIX_WIDTH), F32),
                   sample_f32, sample_f32],
        scratch_shapes=[pltpu.VMEM((POOL_HALO, MIX_WIDTH), F32), pltpu.VMEM((CONV_HALO, MIX_WIDTH), F32)],
        compiler_params=_cparams(("arbitrary",)),
        name="branches",
    )(proj, proj, proj, proj, proj, proj, pool_state, conv_state, wg_bf, pool_scale, ln_g, ln_b,
      gate_mats, gate_bias, conv_w)


def _merge_kernel(oa_ref, ob_ref, oc_ref, wa_ref, wb_ref, wc_ref, ga_ref, gb_ref, gc_ref, o_ref, wbf_ref):
    @pl.when(pl.program_id(1) == 0)
    def _():
        wbf_ref[0] = wa_ref[...].astype(BF16)
        wbf_ref[1] = wb_ref[...].astype(BF16)
        wbf_ref[2] = wc_ref[...].astype(BF16)

    m = jax.nn.sigmoid(ga_ref[...]) * jnp.dot(oa_ref[...], wbf_ref[0], preferred_element_type=F32)
    m = m + jax.nn.sigmoid(gb_ref[...]) * jnp.dot(ob_ref[...], wbf_ref[1], preferred_element_type=F32)
    m = m + jax.nn.sigmoid(gc_ref[...]) * jnp.dot(oc_ref[...], wbf_ref[2], preferred_element_type=F32)
    o_ref[...] = m.astype(BF16)


def _merge(oa, ob, oc, w_branch, proj, layer, tm=512, tn=512):
    t = oa.shape[0]
    act_spec = pl.BlockSpec((tm, MIX_WIDTH), lambda j, i: (i, 0))

    def wspec(b):
        return pl.BlockSpec((None, None, MIX_WIDTH, tn), lambda j, i: (layer, b, 0, j))

    def gspec(b):
        off = (GATE_COL0 + b * D_MODEL) // tn
        return pl.BlockSpec((tm, tn), lambda j, i: (i, off + j))

    return pl.pallas_call(
        _merge_kernel,
        grid=(D_MODEL // tn, t // tm),
        in_specs=[act_spec, act_spec, act_spec, wspec(0), wspec(1), wspec(2), gspec(0), gspec(1), gspec(2)],
        out_specs=pl.BlockSpec((tm, tn), lambda j, i: (i, j)),
        out_shape=jax.ShapeDtypeStruct((t, D_MODEL), BF16),
        scratch_shapes=[pltpu.VMEM((3, MIX_WIDTH, tn), BF16)],
        compiler_params=_cparams(("arbitrary", "arbitrary")),
        name="merge",
    )(oa, ob, oc, w_branch, w_branch, w_branch, proj, proj, proj)


def _out_kernel2(n_prompt_tiles, a_ref, w_ref, xp_ref, xs_ref, o_ref, wbf_ref):
    i = pl.program_id(1)

    @pl.when(i == 0)
    def _():
        wbf_ref[...] = w_ref[...].astype(BF16)

    h = jnp.dot(a_ref[...], wbf_ref[...], preferred_element_type=F32)

    @pl.when(i < n_prompt_tiles)
    def _():
        o_ref[...] = xp_ref[...] + h

    @pl.when(i >= n_prompt_tiles)
    def _():
        o_ref[...] = xs_ref[...] + h


def _out_kernel1(a_ref, w_ref, x_ref, o_ref, wbf_ref):
    @pl.when(pl.program_id(1) == 0)
    def _():
        wbf_ref[...] = w_ref[...].astype(BF16)

    o_ref[...] = x_ref[...] + jnp.dot(a_ref[...], wbf_ref[...], preferred_element_type=F32)


def _out_proj(merged, w_out, layer, x_parts, tm=512, tn=512):
    t = merged.shape[0]
    common = dict(
        grid=(D_MODEL // tn, t // tm),
        out_specs=pl.BlockSpec((tm, tn), lambda j, i: (i, j)),
        out_shape=jax.ShapeDtypeStruct((t, D_MODEL), F32),
        scratch_shapes=[pltpu.VMEM((D_MODEL, tn), BF16)],
        compiler_params=_cparams(("arbitrary", "arbitrary")),
        name="out_proj",
    )
    a_spec = pl.BlockSpec((tm, D_MODEL), lambda j, i: (i, 0))
    w_spec = pl.BlockSpec((None, D_MODEL, tn), lambda j, i: (layer, 0, j))
    if len(x_parts) == 1:
        return pl.pallas_call(
            _out_kernel1,
            in_specs=[a_spec, w_spec, pl.BlockSpec((tm, tn), lambda j, i: (i, j))],
            **common,
        )(merged, w_out, x_parts[0])
    xp, xs = x_parts
    npt = xp.shape[0] // tm
    return pl.pallas_call(
        functools.partial(_out_kernel2, npt),
        in_specs=[
            a_spec, w_spec,
            pl.BlockSpec((tm, tn), lambda j, i: (jnp.minimum(i, npt - 1), j)),
            pl.BlockSpec((tm, tn), lambda j, i: (jnp.maximum(i - npt, 0), j)),
        ],
        **common,
    )(merged, w_out, xp, xs)


def _router_kernel(x_ref, g_ref, wr_ref, br_ref, xn_ref, eid_ref, gate_ref):
    xn = _rms(x_ref[...], g_ref[...])
    xn_ref[...] = xn
    logits = jnp.dot(xn.astype(BF16), wr_ref[...], preferred_element_type=F32) + br_ref[...]
    tm = logits.shape[0]
    lane = lax.broadcasted_iota(jnp.int32, (tm, LANES), 1)
    neg = jnp.float32(-jnp.inf)
    far = jnp.int32(LANES)

    is_g = lane < N_EXPERT_GROUPS
    glog = jnp.where(is_g, logits, neg)
    gmax = jnp.max(glog, axis=1, keepdims=True)
    gsel = jnp.min(jnp.where(glog == gmax, lane, far), axis=1, keepdims=True)
    gsum = jnp.sum(jnp.where(is_g, jnp.exp(glog - gmax), 0.0), axis=1, keepdims=True)
    gp = 1.0 / gsum

    lo = N_EXPERT_GROUPS + gsel * EXPERTS_PER_GROUP
    in_grp = jnp.logical_and(lane >= lo, lane < lo + EXPERTS_PER_GROUP)
    el = jnp.where(in_grp, logits, neg)
    m1 = jnp.max(el, axis=1, keepdims=True)
    i1 = jnp.min(jnp.where(el == m1, lane, far), axis=1, keepdims=True)
    el2 = jnp.where(lane == i1, neg, el)
    m2 = jnp.max(el2, axis=1, keepdims=True)
    i2 = jnp.min(jnp.where(el2 == m2, lane, far), axis=1, keepdims=True)
    e2 = jnp.exp(m2 - m1)
    den = 1.0 + e2
    g1 = gp * (1.0 / den)
    g2 = gp * (e2 / den)

    eid_ref[...] = jnp.where(lane == 0, i1 - N_EXPERT_GROUPS, jnp.where(lane == 1, i2 - N_EXPERT_GROUPS, 0))
    gate_ref[...] = jnp.where(lane == 0, g1, jnp.where(lane == 1, g2, 0.0))


def _router(x1, g, wr, br, tm=256):
    t = x1.shape[0]
    return pl.pallas_call(
        _router_kernel,
        grid=(t // tm,),
        in_specs=[
            pl.BlockSpec((tm, D_MODEL), lambda i: (i, 0)),
            pl.BlockSpec((1, D_MODEL), lambda i: (0, 0)),
            pl.BlockSpec((D_MODEL, LANES), lambda i: (0, 0)),
            pl.BlockSpec((1, LANES), lambda i: (0, 0)),
        ],
        out_specs=[
            pl.BlockSpec((tm, D_MODEL), lambda i: (i, 0)),
            pl.BlockSpec((tm, LANES), lambda i: (i, 0)),
            pl.BlockSpec((tm, LANES), lambda i: (i, 0)),
        ],
        out_shape=[
            jax.ShapeDtypeStruct((t, D_MODEL), F32),
            jax.ShapeDtypeStruct((t, LANES), jnp.int32),
            jax.ShapeDtypeStruct((t, LANES), F32),
        ],
        compiler_params=_cparams(("arbitrary",)),
        name="router",
    )(x1, g, wr, br)


def _gather_rows(idx_ref, idx_base, idx_stride, n_rows, src_hbm, dst, sem):
    def body(r, c):
        row = idx_ref[idx_base + r * idx_stride]
        pltpu.make_async_copy(src_hbm.at[pl.ds(row, 1)], dst.at[pl.ds(r, 1)], sem).start()
        return c

    lax.fori_loop(0, n_rows, body, 0, unroll=8)


def _wait_rows(n_rows, src_hbm, dst, sem):
    pltpu.make_async_copy(src_hbm.at[pl.ds(0, n_rows)], dst, sem).wait()


def _dispatch_kernel(tok_ref, nb_ref, x_hbm, o_ref, gbuf, sems):
    b = pl.program_id(0)
    nb = nb_ref[0]

    def issue(blk):
        slot = blk % 2
        _gather_rows(tok_ref, blk * EXPERT_ROWS, 1, EXPERT_ROWS, x_hbm, gbuf.at[slot], sems.at[slot])

    @pl.when(jnp.logical_and(b == 0, nb > 0))
    def _():
        issue(b)

    @pl.when(b + 1 < nb)
    def _():
        issue(b + 1)

    @pl.when(b < nb)
    def _():
        slot = b % 2
        _wait_rows(EXPERT_ROWS, x_hbm, gbuf.at[slot], sems.at[slot])
        o_ref[...] = gbuf[slot].astype(BF16)

    @pl.when(b >= nb)
    def _():
        o_ref[...] = jnp.zeros_like(o_ref)


def _dispatch(xn, tok_sorted, n_blk):
    cap = tok_sorted.shape[0]
    return pl.pallas_call(
        _dispatch_kernel,
        grid_spec=pltpu.PrefetchScalarGridSpec(
            num_scalar_prefetch=2,
            grid=(cap // EXPERT_ROWS,),
            in_specs=[pl.BlockSpec(memory_space=pl.ANY)],
            out_specs=pl.BlockSpec((EXPERT_ROWS, D_MODEL), lambda b, tok, nbk: (b, 0)),
            scratch_shapes=[pltpu.VMEM((2, EXPERT_ROWS, D_MODEL), F32), pltpu.SemaphoreType.DMA((2,))],
        ),
        out_shape=jax.ShapeDtypeStruct((cap, D_MODEL), BF16),
        compiler_params=_cparams(("arbitrary",)),
        name="dispatch",
    )(tok_sorted, n_blk, xn)


def _new_expert(be_ref, b):
    return jnp.logical_or(b == 0, be_ref[b] != be_ref[jnp.maximum(b - 1, 0)])


def _expert_up_kernel(be_ref, nb_ref, x_ref, w1_ref, w3_ref, o_ref, w1bf, w3bf):
    b = pl.program_id(1)

    @pl.when(_new_expert(be_ref, b))
    def _():
        w1bf[...] = w1_ref[...].astype(BF16)
        w3bf[...] = w3_ref[...].astype(BF16)

    @pl.when(b < nb_ref[0])
    def _():
        x = x_ref[...]
        h1 = jnp.dot(x, w1bf[...], preferred_element_type=F32)
        h3 = jnp.dot(x, w3bf[...], preferred_element_type=F32)
        o_ref[...] = (h1 * jax.nn.sigmoid(h1) * h3).astype(BF16)

    @pl.when(b >= nb_ref[0])
    def _():
        o_ref[...] = jnp.zeros_like(o_ref)


def _expert_up(xs, w_ei, layer, blk_e, n_blk, tf=512):
    cap = xs.shape[0]
    nb = cap // EXPERT_ROWS
    nf = D_EXPERT // tf

    def row(f, b, be, nbk):
        return (jnp.minimum(b, nbk[0] - 1), 0)

    return pl.pallas_call(
        _expert_up_kernel,
        grid_spec=pltpu.PrefetchScalarGridSpec(
            num_scalar_prefetch=2,
            grid=(nf, nb),
            in_specs=[
                pl.BlockSpec((EXPERT_ROWS, D_MODEL), row),
                pl.BlockSpec((None, None, D_MODEL, tf), lambda f, b, be, nbk: (layer, be[b], 0, f)),
                pl.BlockSpec((None, None, D_MODEL, tf), lambda f, b, be, nbk: (layer, be[b], 0, nf + f)),
            ],
            out_specs=pl.BlockSpec((EXPERT_ROWS, tf), lambda f, b, be, nbk: (b, f)),
            scratch_shapes=[pltpu.VMEM((D_MODEL, tf), BF16), pltpu.VMEM((D_MODEL, tf), BF16)],
        ),
        out_shape=jax.ShapeDtypeStruct((cap, D_EXPERT), BF16),
        compiler_params=_cparams(("arbitrary", "arbitrary")),
        name="expert_up",
    )(blk_e, n_blk, xs, w_ei, w_ei)


def _expert_down_kernel(be_ref, nb_ref, h_ref, w_ref, o_ref, wbf):
    b = pl.program_id(1)

    @pl.when(_new_expert(be_ref, b))
    def _():
        wbf[...] = w_ref[...].astype(BF16)

    @pl.when(b < nb_ref[0])
    def _():
        o_ref[...] = jnp.dot(h_ref[...], wbf[...], preferred_element_type=F32)

    @pl.when(b >= nb_ref[0])
    def _():
        o_ref[...] = jnp.zeros_like(o_ref)


def _expert_down(hs, w_eo, layer, blk_e, n_blk, tn=D_MODEL):
    cap = hs.shape[0]
    nb = cap // EXPERT_ROWS
    return pl.pallas_call(
        _expert_down_kernel,
        grid_spec=pltpu.PrefetchScalarGridSpec(
            num_scalar_prefetch=2,
            grid=(D_MODEL // tn, nb),
            in_specs=[
                pl.BlockSpec((EXPERT_ROWS, D_EXPERT), lambda n, b, be, nbk: (jnp.minimum(b, nbk[0] - 1), 0)),
                pl.BlockSpec((None, None, D_EXPERT, tn), lambda n, b, be, nbk: (layer, be[b], 0, n)),
            ],
            out_specs=pl.BlockSpec((EXPERT_ROWS, tn), lambda n, b, be, nbk: (b, n)),
            scratch_shapes=[pltpu.VMEM((D_EXPERT, tn), BF16)],
        ),
        out_shape=jax.ShapeDtypeStruct((cap, D_MODEL), F32),
        compiler_params=_cparams(("arbitrary", "arbitrary")),
        name="expert_down",
    )(blk_e, n_blk, hs, w_eo)


def _combine_kernel(tm, split, dest_ref, x_ref, gate_ref, g_ref, ys_hbm, *rest):
    if split is None:
        x2_ref, xn_ref, ybuf, sem = rest
    else:
        yp_ref, ysm_ref, ybuf, sem = rest
    i = pl.program_id(0)

    def issue(tile):
        slot = tile % 2
        for k in range(TOP_K):
            _gather_rows(dest_ref, tile * tm * TOP_K + k, TOP_K, tm, ys_hbm, ybuf.at[slot, k], sem.at[slot])

    @pl.when(i == 0)
    def _():
        issue(i)

    @pl.when(i + 1 < pl.num_programs(0))
    def _():
        issue(i + 1)

    slot = i % 2
    for k in range(TOP_K):
        _wait_rows(tm, ys_hbm, ybuf.at[slot, k], sem.at[slot])
    gates = gate_ref[...]
    x2 = x_ref[...] + (ybuf[slot, 0] * gates[:, 0:1] + ybuf[slot, 1] * gates[:, 1:2])
    if split is None:
        x2_ref[...] = x2
        xn_ref[...] = _rms(x2, g_ref[...]).astype(xn_ref.dtype)
    else:
        y = _rms(x2, g_ref[...])

        @pl.when(i < split)
        def _():
            yp_ref[...] = y

        @pl.when(i >= split)
        def _():
            ysm_ref[...] = y


def _combine(x1, gates, dest_flat, ys, g, final_split=None, tm=128):
    t = x1.shape[0]
    in_specs = [
        pl.BlockSpec((tm, D_MODEL), lambda i, d: (i, 0)),
        pl.BlockSpec((tm, LANES), lambda i, d: (i, 0)),
        pl.BlockSpec((1, D_MODEL), lambda i, d: (0, 0)),
        pl.BlockSpec(memory_space=pl.ANY),
    ]
    if final_split is None:
        split = None
        out_specs = [pl.BlockSpec((tm, D_MODEL), lambda i, d: (i, 0)),
                     pl.BlockSpec((tm, D_MODEL), lambda i, d: (i, 0))]
        out_shape = [jax.ShapeDtypeStruct((t, D_MODEL), F32), jax.ShapeDtypeStruct((t, D_MODEL), BF16)]
    else:
        split = final_split // tm
        out_specs = [pl.BlockSpec((tm, D_MODEL), lambda i, d: (jnp.minimum(i, split - 1), 0)),
                     pl.BlockSpec((tm, D_MODEL), lambda i, d: (jnp.maximum(i - split, 0), 0))]
        out_shape = [jax.ShapeDtypeStruct((final_split, D_MODEL), F32),
                     jax.ShapeDtypeStruct((t - final_split, D_MODEL), F32)]
    return pl.pallas_call(
        functools.partial(_combine_kernel, tm, split),
        grid_spec=pltpu.PrefetchScalarGridSpec(
            num_scalar_prefetch=1,
            grid=(t // tm,),
            in_specs=in_specs,
            out_specs=out_specs,
            scratch_shapes=[pltpu.VMEM((2, TOP_K, tm, D_MODEL), F32), pltpu.SemaphoreType.DMA((2,))],
        ),
        out_shape=out_shape,
        compiler_params=_cparams(("arbitrary",)),
        name="combine",
    )(dest_flat, x1, gates, g, ys)


def _routing_tables(eid, n_blocks):
    e_flat = eid.reshape(-1)
    onehot = (e_flat[:, None] == jnp.arange(N_EXPERTS, dtype=jnp.int32)[None, :]).astype(jnp.int32)
    csum = jnp.cumsum(onehot, axis=0)
    rank = jnp.sum(csum * onehot, axis=1) - 1
    counts = csum[-1]
    blocks = (counts + EXPERT_ROWS - 1) // EXPERT_ROWS
    blk_end = jnp.cumsum(blocks)
    seg_start = (blk_end - blocks) * EXPERT_ROWS
    dest = (seg_start[e_flat] + rank).astype(jnp.int32)
    blk_e = jnp.searchsorted(blk_end, jnp.arange(n_blocks, dtype=jnp.int32), side="right")
    n_used = blk_end[-1].astype(jnp.int32)
    last_e = jnp.take(blk_e, jnp.maximum(n_used - 1, 0))
    blk_e = jnp.where(jnp.arange(n_blocks) < n_used, blk_e, last_e).astype(jnp.int32)
    tok_sorted = jnp.zeros((n_blocks * EXPERT_ROWS,), jnp.int32).at[dest].set(
        jnp.arange(dest.shape[0], dtype=jnp.int32) // TOP_K, unique_indices=True)
    return dest, tok_sorted, blk_e, n_used.reshape(1)


def _gate_tables(w_spatial, b_spatial, sample_len):
    depth = w_spatial.shape[0]
    tril = jnp.tril(jnp.ones((CHUNK, CHUNK), dtype=bool))
    full = jnp.where(tril[None, None], w_spatial, 0)
    reps = CHUNK // sample_len
    small = jnp.where(tril[None, None, :sample_len, :sample_len], w_spatial[:, :, :sample_len, :sample_len], 0)
    eye = jnp.eye(reps, dtype=w_spatial.dtype)
    blockdiag = jnp.einsum("ab,lhts->lhatbs", eye, small).reshape(depth, SG_HEADS, CHUNK, CHUNK)
    mats = jnp.stack([full, blockdiag], axis=1).astype(BF16)
    bias_full = jnp.transpose(b_spatial, (0, 2, 1))
    bias_small = jnp.tile(bias_full[:, :sample_len, :], (1, reps, 1))
    bias = jnp.stack([bias_full, bias_small], axis=1)
    bias = jnp.repeat(bias, SG_HEAD_DIM, axis=-1)
    return mats, bias


def kernel(x_prompt, x_sample, state_pool, state_conv, norm_mix, w_in, w_pool_group, pool_scale, sg_norm_g, sg_norm_b, w_spatial, b_spatial, conv_w, w_branch, w_out, norm_ffn, w_router_group, b_router_group, w_router_expert, b_router_expert, w_expert_in, w_expert_out, norm_final):
    depth = w_in.shape[0]
    bp, lp, _ = x_prompt.shape
    bs, ls, _ = x_sample.shape
    tp, ts = bp * lp, bs * ls
    t = tp + ts
    assert lp % ROW_TILE == 0 and ROW_TILE % ls == 0 and bs % (ROW_TILE // ls) == 0 and ls >= CONV_STATE

    xp = x_prompt.reshape(tp, D_MODEL)
    xs = x_sample.reshape(ts, D_MODEL)

    gate_mats, gate_bias = _gate_tables(w_spatial, b_spatial, ls)
    wg_bf = w_pool_group.astype(BF16)
    vec = lambda a: a.reshape(depth, 1, -1)
    wr = jnp.concatenate([w_router_group, w_router_expert,
                          jnp.zeros((depth, D_MODEL, LANES - N_EXPERT_GROUPS - N_EXPERTS), F32)], axis=-1).astype(BF16)
    br = jnp.concatenate([b_router_group, b_router_expert,
                          jnp.zeros((depth, LANES - N_EXPERT_GROUPS - N_EXPERTS), F32)], axis=-1).reshape(depth, 1, LANES)
    pool_hist = jnp.pad(state_pool, ((0, 0), (0, 0), (POOL_HALO - POOL_STATE, 0), (0, 0))).reshape(
        depth, bs * POOL_HALO, MIX_WIDTH)
    conv_hist = jnp.pad(state_conv, ((0, 0), (0, 0), (CONV_HALO - CONV_STATE, 0), (0, 0))).reshape(
        depth, bs * CONV_HALO, MIX_WIDTH)

    n_assign = t * TOP_K
    n_blocks = n_assign // EXPERT_ROWS + N_EXPERTS
    cap = n_blocks * EXPERT_ROWS

    pool_p, pool_s, conv_p, conv_s, v_out = [], [], [], [], []
    x_parts = (xp, xs)
    xn = _norm_in(xp, xs, norm_mix[0].reshape(1, D_MODEL))
    y_p = y_s = None
    for l in range(depth):
        proj = _in_proj(xn, w_in, l)
        oa, ob, oc, z_tail, v_s, z_s = _branches(
            proj, l, bp, lp, bs, ls, pool_hist, conv_hist, wg_bf, vec(pool_scale), vec(sg_norm_g),
            vec(sg_norm_b), gate_mats, gate_bias, conv_w)
        merged = _merge(oa, ob, oc, w_branch, proj, l)
        x1 = _out_proj(merged, w_out, l, x_parts)

        xn2, eid, gates = _router(x1, norm_ffn[l].reshape(1, D_MODEL), wr[l], br[l])
        dest, tok_sorted, blk_e, n_used = _routing_tables(eid[:, :TOP_K], n_blocks)
        xs_sorted = _dispatch(xn2, tok_sorted, n_used)
        hs = _expert_up(xs_sorted, w_expert_in, l, blk_e, n_used)
        ys = _expert_down(hs, w_expert_out, l, blk_e, n_used)
        if l + 1 < depth:
            x2, xn = _combine(x1, gates, dest, ys, norm_mix[l + 1].reshape(1, D_MODEL))
            x_parts = (x2,)
        else:
            y_p, y_s = _combine(x1, gates, dest, ys, norm_final.reshape(1, D_MODEL), final_split=tp)

        a_p = proj[:tp, :MIX_WIDTH].reshape(bp, lp, MIX_WIDTH)
        a_s = proj[tp:, :MIX_WIDTH].reshape(bs, ls, MIX_WIDTH)
        pool_p.append(a_p[:, lp - POOL_STATE:])
        pool_s.append(jnp.concatenate([state_pool[l], a_s], axis=1)[:, -POOL_STATE:])
        conv_p.append(z_tail[:, CONV_HALO - CONV_STATE:])
        conv_s.append(z_s.reshape(bs, ls, MIX_WIDTH)[:, ls - CONV_STATE:])
        v_out.append(v_s.reshape(bs, ls, MIX_WIDTH))

    return (y_p.reshape(bp, lp, D_MODEL), y_s.reshape(bs, ls, D_MODEL), jnp.stack(pool_p), jnp.stack(pool_s),
            jnp.stack(conv_p), jnp.stack(conv_s), jnp.stack(v_out))
```

```python
import functools

import jax
import jax.numpy as jnp
from jax import lax
from jax.experimental import pallas as pl
from jax.experimental.pallas import tpu as pltpu

F32 = jnp.float32
BF16 = jnp.bfloat16

D_MODEL = 4096
MIX_WIDTH = D_MODEL // 2
POOL_GROUPS = 4
POOL_WINDOWS = (2, 4, 8, 16)
POOL_GROUP_DIM = MIX_WIDTH // POOL_GROUPS
POOL_STATE = 15
SG_HEADS = 8
SG_HEAD_DIM = MIX_WIDTH // SG_HEADS
CHUNK = 128
CONV_STATE = 2
IN_COLS = MIX_WIDTH * 6 + 3 * D_MODEL
GATE_COL0 = MIX_WIDTH * 6
N_EXPERT_GROUPS = 4
EXPERTS_PER_GROUP = 8
N_EXPERTS = N_EXPERT_GROUPS * EXPERTS_PER_GROUP
TOP_K = 2
D_EXPERT = D_MODEL // 4
PAST_LEN = 16384
EPS = 1e-6

LANES = 128
SUBLANES = 8
POOL_HALO = 16
CONV_HALO = 8
ROW_TILE = 128
EXPERT_ROWS = 256
PACKED_WIDTH = D_MODEL // 2
VMEM_LIMIT = 56 * 1024 * 1024


def _cparams(sem):
    return pltpu.CompilerParams(dimension_semantics=sem, vmem_limit_bytes=VMEM_LIMIT)


def _gelu_tanh(x):
    c = 0.7978845608028654
    return 0.5 * x * (1.0 + jnp.tanh(c * (x + 0.044715 * (x * x * x))))


def _rms(x, g):
    return x * lax.rsqrt(jnp.mean(x * x, axis=-1, keepdims=True) + EPS) * g


def _norm_in_kernel(n_prompt_tiles, xp_ref, xs_ref, g_ref, o_ref):
    i = pl.program_id(0)

    @pl.when(i < n_prompt_tiles)
    def _():
        o_ref[...] = _rms(xp_ref[...], g_ref[...]).astype(BF16)

    @pl.when(i >= n_prompt_tiles)
    def _():
        o_ref[...] = _rms(xs_ref[...], g_ref[...]).astype(BF16)


def _norm_in(xp, xs, g, tm=256):
    tp, ts = xp.shape[0], xs.shape[0]
    npt, nst = tp // tm, ts // tm
    return pl.pallas_call(
        functools.partial(_norm_in_kernel, npt),
        grid=(npt + nst,),
        in_specs=[
            pl.BlockSpec((tm, D_MODEL), lambda i: (jnp.minimum(i, npt - 1), 0)),
            pl.BlockSpec((tm, D_MODEL), lambda i: (jnp.maximum(i - npt, 0), 0)),
            pl.BlockSpec((1, D_MODEL), lambda i: (0, 0)),
        ],
        out_specs=pl.BlockSpec((tm, D_MODEL), lambda i: (i, 0)),
        out_shape=jax.ShapeDtypeStruct((tp + ts, D_MODEL), BF16),
        compiler_params=_cparams(("arbitrary",)),
        name="norm_in",
    )(xp, xs, g)


def _panel_matmul_kernel(a_ref, w_ref, o_ref, wbf_ref):
    @pl.when(pl.program_id(1) == 0)
    def _():
        wbf_ref[...] = w_ref[...].astype(BF16)

    o_ref[...] = jnp.dot(a_ref[...], wbf_ref[...], preferred_element_type=F32)


def _in_proj(xn, w_in, layer, tm=512, tn=1024):
    t = xn.shape[0]
    return pl.pallas_call(
        _panel_matmul_kernel,
        grid=(IN_COLS // tn, t // tm),
        in_specs=[
            pl.BlockSpec((tm, D_MODEL), lambda j, i: (i, 0)),
            pl.BlockSpec((None, D_MODEL, tn), lambda j, i: (layer, 0, j)),
        ],
        out_specs=pl.BlockSpec((tm, tn), lambda j, i: (i, j)),
        out_shape=jax.ShapeDtypeStruct((t, IN_COLS), F32),
        scratch_shapes=[pltpu.VMEM((D_MODEL, tn), BF16)],
        compiler_params=_cparams(("arbitrary", "arbitrary")),
        name="in_proj",
    )(xn, w_in)


def _window_rows(ext, n_seq, halo, rows):
    if n_seq == 1:
        return ext[halo:, :]
    c = ext.shape[-1]
    return ext.reshape(n_seq, halo + rows, c)[:, halo:, :].reshape(n_seq * rows, c)


def _stack_history(hist, cur, n_seq, halo, rows):
    if n_seq == 1:
        return jnp.concatenate([hist, cur], axis=0)
    c = cur.shape[-1]
    ext = jnp.concatenate([hist.reshape(n_seq, halo, c), cur.reshape(n_seq, rows, c)], axis=1)
    return ext.reshape(n_seq * (halo + rows), c)


def _branch_math(n_seq, rows, pos, a_ref, u_ref, v_ref, ci_ref, cb_ref, cc_ref, pool_hist, conv_hist,
                 wg_ref, ps_ref, lng_ref, lnb_ref, m_ref, bias_ref, cw_ref, oa_ref, ob_ref, oc_ref):
    for g, w in enumerate(POOL_WINDOWS):
        cols = slice(g * POOL_GROUP_DIM, (g + 1) * POOL_GROUP_DIM)
        a_g = a_ref[:, cols]
        s = _stack_history(pool_hist[:, cols], a_g, n_seq, POOL_HALO, rows)
        k = 1
        while k < w:
            s = s + pltpu.roll(s, k, 0)
            k *= 2
        s = _window_rows(s, n_seq, POOL_HALO, rows)
        cnt = jnp.minimum(w, pos + 1).astype(F32)
        pooled = (s / cnt - a_g).astype(BF16)
        out = jnp.dot(pooled, wg_ref[g], preferred_element_type=F32) * ps_ref[:, cols]
        oa_ref[:, cols] = out.astype(BF16)

    vg = _gelu_tanh(v_ref[...])
    mu = jnp.mean(vg, axis=-1, keepdims=True)
    xc = vg - mu
    v = xc * lax.rsqrt(jnp.mean(xc * xc, axis=-1, keepdims=True) + EPS) * lng_ref[...] + lnb_ref[...]
    vb = v.astype(BF16)
    for h in range(SG_HEADS):
        cols = slice(h * SG_HEAD_DIM, (h + 1) * SG_HEAD_DIM)
        sp = jnp.dot(m_ref[h], vb[:, cols], preferred_element_type=F32) + bias_ref[:, cols]
        ob_ref[:, cols] = (_gelu_tanh(u_ref[:, cols]) * sp).astype(BF16)

    z = cc_ref[...] * ci_ref[...]
    e = _stack_history(conv_hist, z, n_seq, CONV_HALO, rows)
    y = cw_ref[0:1, :] * pltpu.roll(e, 2, 0)
    y = y + cw_ref[1:2, :] * pltpu.roll(e, 1, 0)
    y = y + cw_ref[2:3, :] * e
    oc_ref[...] = (cb_ref[...] * _window_rows(y, n_seq, CONV_HALO, rows)).astype(BF16)
    return v, z


def _branch_kernel(tiles_per_seq, n_prompt_tiles, seq_per_tile, sample_len,
                   a_ref, u_ref, v_ref, ci_ref, cb_ref, cc_ref, ph_ref, ch_ref, wg_ref, ps_ref, lng_ref, lnb_ref,
                   m_ref, bias_ref, cw_ref, oa_ref, ob_ref, oc_ref, zl_ref, al_ref, vo_ref, zo_ref, pool_hist, conv_hist):
    i = pl.program_id(0)
    shared = (wg_ref, ps_ref, lng_ref, lnb_ref, m_ref, bias_ref, cw_ref, oa_ref, ob_ref, oc_ref)

    @pl.when(i < n_prompt_tiles)
    def _():
        t = i % tiles_per_seq

        @pl.when(t == 0)
        def _():
            pool_hist[...] = jnp.zeros_like(pool_hist)
            conv_hist[...] = jnp.zeros_like(conv_hist)

        pos = t * ROW_TILE + lax.broadcasted_iota(jnp.int32, (ROW_TILE, 1), 0)
        _, z = _branch_math(1, ROW_TILE, pos, a_ref, u_ref, v_ref, ci_ref, cb_ref, cc_ref, pool_hist[...],
                            conv_hist[...], *shared)
        pool_hist[...] = a_ref[ROW_TILE - POOL_HALO:, :]
        conv_hist[...] = z[ROW_TILE - CONV_HALO:, :]
        zl_ref[...] = z[ROW_TILE - CONV_HALO:, :]
        al_ref[...] = a_ref[ROW_TILE - POOL_HALO:, :]

    @pl.when(i >= n_prompt_tiles)
    def _():
        r = lax.broadcasted_iota(jnp.int32, (seq_per_tile * sample_len, 1), 0)
        pos = PAST_LEN + (r % sample_len)
        v, z = _branch_math(seq_per_tile, sample_len, pos, a_ref, u_ref, v_ref, ci_ref, cb_ref, cc_ref,
                            ph_ref[...], ch_ref[...], *shared)
        vo_ref[...] = v
        zo_ref[...] = z


def _branches(proj, layer, n_prompt_seq, prompt_len, n_sample_seq, sample_len, pool_state, conv_state,
              wg_bf, pool_scale, ln_g, ln_b, gate_mats, gate_bias, conv_w):
    t_total = proj.shape[0]
    tiles_per_seq = prompt_len // ROW_TILE
    npt = n_prompt_seq * tiles_per_seq
    seq_per_tile = ROW_TILE // sample_len
    nst = n_sample_seq // seq_per_tile
    sample_rows = n_sample_seq * sample_len
    act = jax.ShapeDtypeStruct((t_total, MIX_WIDTH), BF16)
    sample_f32 = jax.ShapeDtypeStruct((sample_rows, MIX_WIDTH), F32)

    def sample_tile(i):
        return jnp.maximum(i - npt, 0)

    def kind(i):
        return jnp.where(i >= npt, 1, 0)

    slab_specs = [pl.BlockSpec((ROW_TILE, MIX_WIDTH), functools.partial(lambda c, i: (i, c), c)) for c in range(6)]
    vec_spec = pl.BlockSpec((None, 1, MIX_WIDTH), lambda i: (layer, 0, 0))
    in_specs = slab_specs + [
        pl.BlockSpec((None, seq_per_tile * POOL_HALO, MIX_WIDTH), lambda i: (layer, sample_tile(i), 0)),
        pl.BlockSpec((None, seq_per_tile * CONV_HALO, MIX_WIDTH), lambda i: (layer, sample_tile(i), 0)),
        pl.BlockSpec((None, POOL_GROUPS, POOL_GROUP_DIM, POOL_GROUP_DIM), lambda i: (layer, 0, 0, 0)),
        vec_spec, vec_spec, vec_spec,
        pl.BlockSpec((None, None, SG_HEADS, CHUNK, CHUNK), lambda i: (layer, kind(i), 0, 0, 0)),
        pl.BlockSpec((None, None, CHUNK, MIX_WIDTH), lambda i: (layer, kind(i), 0, 0)),
        pl.BlockSpec((None, 3, MIX_WIDTH), lambda i: (layer, 0, 0)),
    ]
    row_spec = pl.BlockSpec((ROW_TILE, MIX_WIDTH), lambda i: (i, 0))
    out_specs = [
        row_spec, row_spec, row_spec,
        pl.BlockSpec((None, CONV_HALO, MIX_WIDTH), lambda i: (jnp.minimum(i // tiles_per_seq, n_prompt_seq - 1), 0, 0)),
        pl.BlockSpec((None, POOL_HALO, MIX_WIDTH), lambda i: (jnp.minimum(i // tiles_per_seq, n_prompt_seq - 1), 0, 0)),
        pl.BlockSpec((ROW_TILE, MIX_WIDTH), lambda i: (sample_tile(i), 0)),
        pl.BlockSpec((ROW_TILE, MIX_WIDTH), lambda i: (sample_tile(i), 0)),
    ]
    return pl.pallas_call(
        functools.partial(_branch_kernel, tiles_per_seq, npt, seq_per_tile, sample_len),
        grid=(npt + nst,),
        in_specs=in_specs,
        out_specs=out_specs,
        out_shape=[act, act, act, jax.ShapeDtypeStruct((n_prompt_seq, CONV_HALO, MIX_WIDTH), F32),
                   jax.ShapeDtypeStruct((n_prompt_seq, POOL_HALO, MIX_WIDTH), F32), sample_f32, sample_f32],
        scratch_shapes=[pltpu.VMEM((POOL_HALO, MIX_WIDTH), F32), pltpu.VMEM((CONV_HALO, MIX_WIDTH), F32)],
        compiler_params=_cparams(("arbitrary",)),
        name="branches",
    )(proj, proj, proj, proj, proj, proj, pool_state, conv_state, wg_bf, pool_scale, ln_g, ln_b,
      gate_mats, gate_bias, conv_w)


def _merge_kernel(oa_ref, ob_ref, oc_ref, wa_ref, wb_ref, wc_ref, ga_ref, gb_ref, gc_ref, o_ref, wbf_ref):
    @pl.when(pl.program_id(1) == 0)
    def _():
        wbf_ref[0] = wa_ref[...].astype(BF16)
        wbf_ref[1] = wb_ref[...].astype(BF16)
        wbf_ref[2] = wc_ref[...].astype(BF16)

    m = jax.nn.sigmoid(ga_ref[...]) * jnp.dot(oa_ref[...], wbf_ref[0], preferred_element_type=F32)
    m = m + jax.nn.sigmoid(gb_ref[...]) * jnp.dot(ob_ref[...], wbf_ref[1], preferred_element_type=F32)
    m = m + jax.nn.sigmoid(gc_ref[...]) * jnp.dot(oc_ref[...], wbf_ref[2], preferred_element_type=F32)
    o_ref[...] = m.astype(BF16)


def _merge(oa, ob, oc, w_branch, proj, layer, tm=512, tn=512):
    t = oa.shape[0]
    act_spec = pl.BlockSpec((tm, MIX_WIDTH), lambda j, i: (i, 0))

    def wspec(b):
        return pl.BlockSpec((None, None, MIX_WIDTH, tn), lambda j, i: (layer, b, 0, j))

    def gspec(b):
        off = (GATE_COL0 + b * D_MODEL) // tn
        return pl.BlockSpec((tm, tn), lambda j, i: (i, off + j))

    return pl.pallas_call(
        _merge_kernel,
        grid=(D_MODEL // tn, t // tm),
        in_specs=[act_spec, act_spec, act_spec, wspec(0), wspec(1), wspec(2), gspec(0), gspec(1), gspec(2)],
        out_specs=pl.BlockSpec((tm, tn), lambda j, i: (i, j)),
        out_shape=jax.ShapeDtypeStruct((t, D_MODEL), BF16),
        scratch_shapes=[pltpu.VMEM((3, MIX_WIDTH, tn), BF16)],
        compiler_params=_cparams(("arbitrary", "arbitrary")),
        name="merge",
    )(oa, ob, oc, w_branch, w_branch, w_branch, proj, proj, proj)


def _out_kernel2(n_prompt_tiles, a_ref, w_ref, xp_ref, xs_ref, o_ref, wbf_ref):
    i = pl.program_id(1)

    @pl.when(i == 0)
    def _():
        wbf_ref[...] = w_ref[...].astype(BF16)

    h = jnp.dot(a_ref[...], wbf_ref[...], preferred_element_type=F32)

    @pl.when(i < n_prompt_tiles)
    def _():
        o_ref[...] = xp_ref[...] + h

    @pl.when(i >= n_prompt_tiles)
    def _():
        o_ref[...] = xs_ref[...] + h


def _out_kernel1(a_ref, w_ref, x_ref, o_ref, wbf_ref):
    @pl.when(pl.program_id(1) == 0)
    def _():
        wbf_ref[...] = w_ref[...].astype(BF16)

    o_ref[...] = x_ref[...] + jnp.dot(a_ref[...], wbf_ref[...], preferred_element_type=F32)


def _out_proj(merged, w_out, layer, x_parts, tm=512, tn=512):
    t = merged.shape[0]
    common = dict(
        grid=(D_MODEL // tn, t // tm),
        out_specs=pl.BlockSpec((tm, tn), lambda j, i: (i, j)),
        out_shape=jax.ShapeDtypeStruct((t, D_MODEL), F32),
        scratch_shapes=[pltpu.VMEM((D_MODEL, tn), BF16)],
        compiler_params=_cparams(("arbitrary", "arbitrary")),
        name="out_proj",
    )
    a_spec = pl.BlockSpec((tm, D_MODEL), lambda j, i: (i, 0))
    w_spec = pl.BlockSpec((None, D_MODEL, tn), lambda j, i: (layer, 0, j))
    if len(x_parts) == 1:
        return pl.pallas_call(
            _out_kernel1,
            in_specs=[a_spec, w_spec, pl.BlockSpec((tm, tn), lambda j, i: (i, j))],
            **common,
        )(merged, w_out, x_parts[0])
    xp, xs = x_parts
    npt = xp.shape[0] // tm
    return pl.pallas_call(
        functools.partial(_out_kernel2, npt),
        in_specs=[
            a_spec, w_spec,
            pl.BlockSpec((tm, tn), lambda j, i: (jnp.minimum(i, npt - 1), j)),
            pl.BlockSpec((tm, tn), lambda j, i: (jnp.maximum(i - npt, 0), j)),
        ],
        **common,
    )(merged, w_out, xp, xs)


def _router_kernel(x_ref, g_ref, wr_ref, br_ref, xn_ref, eid_ref, gate_ref):
    xb = _rms(x_ref[...], g_ref[...]).astype(BF16)
    lo = pltpu.bitcast(xb[:, :PACKED_WIDTH].astype(F32), jnp.uint32) >> 16
    hi = pltpu.bitcast(xb[:, PACKED_WIDTH:].astype(F32), jnp.uint32) & jnp.uint32(0xFFFF0000)
    xn_ref[...] = hi | lo
    logits = jnp.dot(xb, wr_ref[...], preferred_element_type=F32) + br_ref[...]
    tm = logits.shape[0]
    lane = lax.broadcasted_iota(jnp.int32, (tm, LANES), 1)
    neg = jnp.float32(-jnp.inf)
    far = jnp.int32(LANES)

    is_g = lane < N_EXPERT_GROUPS
    glog = jnp.where(is_g, logits, neg)
    gmax = jnp.max(glog, axis=1, keepdims=True)
    gsel = jnp.min(jnp.where(glog == gmax, lane, far), axis=1, keepdims=True)
    gsum = jnp.sum(jnp.where(is_g, jnp.exp(glog - gmax), 0.0), axis=1, keepdims=True)
    gp = 1.0 / gsum

    lo = N_EXPERT_GROUPS + gsel * EXPERTS_PER_GROUP
    in_grp = jnp.logical_and(lane >= lo, lane < lo + EXPERTS_PER_GROUP)
    el = jnp.where(in_grp, logits, neg)
    m1 = jnp.max(el, axis=1, keepdims=True)
    i1 = jnp.min(jnp.where(el == m1, lane, far), axis=1, keepdims=True)
    el2 = jnp.where(lane == i1, neg, el)
    m2 = jnp.max(el2, axis=1, keepdims=True)
    i2 = jnp.min(jnp.where(el2 == m2, lane, far), axis=1, keepdims=True)
    e2 = jnp.exp(m2 - m1)
    den = 1.0 + e2
    g1 = gp * (1.0 / den)
    g2 = gp * (e2 / den)

    eid_ref[...] = jnp.where(lane == 0, i1 - N_EXPERT_GROUPS, jnp.where(lane == 1, i2 - N_EXPERT_GROUPS, 0))
    gate_ref[...] = jnp.where(lane == 0, g1, jnp.where(lane == 1, g2, 0.0))


def _router(x1, g, wr, br, tm=256):
    t = x1.shape[0]
    return pl.pallas_call(
        _router_kernel,
        grid=(t // tm,),
        in_specs=[
            pl.BlockSpec((tm, D_MODEL), lambda i: (i, 0)),
            pl.BlockSpec((1, D_MODEL), lambda i: (0, 0)),
            pl.BlockSpec((D_MODEL, LANES), lambda i: (0, 0)),
            pl.BlockSpec((1, LANES), lambda i: (0, 0)),
        ],
        out_specs=[
            pl.BlockSpec((tm, PACKED_WIDTH), lambda i: (i, 0)),
            pl.BlockSpec((tm, LANES), lambda i: (i, 0)),
            pl.BlockSpec((tm, LANES), lambda i: (i, 0)),
        ],
        out_shape=[
            jax.ShapeDtypeStruct((t, PACKED_WIDTH), jnp.uint32),
            jax.ShapeDtypeStruct((t, LANES), jnp.int32),
            jax.ShapeDtypeStruct((t, LANES), F32),
        ],
        compiler_params=_cparams(("arbitrary",)),
        name="router",
    )(x1, g, wr, br)


def _gather_rows(idx_ref, idx_base, idx_stride, n_rows, src_hbm, dst, sem):
    def body(r, c):
        row = idx_ref[idx_base + r * idx_stride]
        pltpu.make_async_copy(src_hbm.at[pl.ds(row, 1)], dst.at[pl.ds(r, 1)], sem).start()
        return c

    lax.fori_loop(0, n_rows, body, 0, unroll=8)


def _wait_rows(n_rows, src_hbm, dst, sem):
    pltpu.make_async_copy(src_hbm.at[pl.ds(0, n_rows)], dst, sem).wait()


def _dispatch_kernel(tok_ref, nb_ref, x_hbm, o_ref, gbuf, sems):
    b = pl.program_id(0)
    nb = nb_ref[0]

    def issue(blk):
        slot = blk % 2
        _gather_rows(tok_ref, blk * EXPERT_ROWS, 1, EXPERT_ROWS, x_hbm, gbuf.at[slot], sems.at[slot])

    @pl.when(jnp.logical_and(b == 0, nb > 0))
    def _():
        issue(b)

    @pl.when(b + 1 < nb)
    def _():
        issue(b + 1)

    @pl.when(b < nb)
    def _():
        slot = b % 2
        _wait_rows(EXPERT_ROWS, x_hbm, gbuf.at[slot], sems.at[slot])
        words = gbuf[slot]
        o_ref[:, :PACKED_WIDTH] = pltpu.bitcast(words << 16, F32).astype(BF16)
        o_ref[:, PACKED_WIDTH:] = pltpu.bitcast(words & jnp.uint32(0xFFFF0000), F32).astype(BF16)

    @pl.when(b >= nb)
    def _():
        o_ref[...] = jnp.zeros_like(o_ref)


def _dispatch(xn, tok_sorted, n_blk):
    cap = tok_sorted.shape[0]
    return pl.pallas_call(
        _dispatch_kernel,
        grid_spec=pltpu.PrefetchScalarGridSpec(
            num_scalar_prefetch=2,
            grid=(cap // EXPERT_ROWS,),
            in_specs=[pl.BlockSpec(memory_space=pl.ANY)],
            out_specs=pl.BlockSpec((EXPERT_ROWS, D_MODEL), lambda b, tok, nbk: (b, 0)),
            scratch_shapes=[pltpu.VMEM((2, EXPERT_ROWS, PACKED_WIDTH), jnp.uint32), pltpu.SemaphoreType.DMA((2,))],
        ),
        out_shape=jax.ShapeDtypeStruct((cap, D_MODEL), BF16),
        compiler_params=_cparams(("arbitrary",)),
        name="dispatch",
    )(tok_sorted, n_blk, xn)


def _stream_panels(first_ref, pidx_ref, meta_ref, pass_idx, n_pass, b, panel_copies, on_arrival):
    n_distinct = meta_ref[1]
    n_panels = n_pass * n_distinct

    def start(q):
        for c in panel_copies(q // n_distinct, q % n_distinct, q % 2):
            c.start()

    @pl.when(jnp.logical_and(pass_idx == 0, b == 0))
    def _():
        start(0)

        @pl.when(n_panels > 1)
        def _():
            start(1)

    @pl.when(first_ref[b] == 1)
    def _():
        p = pass_idx * n_distinct + pidx_ref[b]
        slot = p % 2
        for c in panel_copies(pass_idx, pidx_ref[b], slot):
            c.wait()
        on_arrival(slot)

        @pl.when(p + 2 < n_panels)
        def _():
            start(p + 2)


def _cast_rows(src, dst, chunk=512):
    def body(i, c):
        r = pl.multiple_of(i * chunk, chunk)
        dst[pl.ds(r, chunk), :] = src[pl.ds(r, chunk), :].astype(BF16)
        return c

    lax.fori_loop(0, src.shape[0] // chunk, body, 0)


def _expert_up_kernel(layer, tf, first_ref, pidx_ref, pe_ref, meta_ref, x_ref, w_hbm, o_ref, stage, wbf, sems):
    f = pl.program_id(0)
    b = pl.program_id(1)

    def panel_copies(fq, k, slot):
        e = pe_ref[k]
        col = pl.multiple_of(fq * tf, tf)
        return [pltpu.make_async_copy(w_hbm.at[layer, e, :, pl.ds(half * D_EXPERT + col, tf)],
                                      stage.at[slot, half], sems.at[slot]) for half in range(2)]

    def on_arrival(slot):
        for half in range(2):
            _cast_rows(stage.at[slot, half], wbf.at[half])

    _stream_panels(first_ref, pidx_ref, meta_ref, f, pl.num_programs(0), b, panel_copies, on_arrival)

    @pl.when(b < meta_ref[0])
    def _():
        x = x_ref[...]
        h1 = jnp.dot(x, wbf[0], preferred_element_type=F32)
        h3 = jnp.dot(x, wbf[1], preferred_element_type=F32)
        o_ref[...] = (h1 * jax.nn.sigmoid(h1) * h3).astype(BF16)

    @pl.when(b >= meta_ref[0])
    def _():
        o_ref[...] = jnp.zeros_like(o_ref)


def _expert_up(xs, w_ei, layer, sched, tf=512):
    cap = xs.shape[0]
    nb = cap // EXPERT_ROWS
    nf = D_EXPERT // tf
    return pl.pallas_call(
        functools.partial(_expert_up_kernel, layer, tf),
        grid_spec=pltpu.PrefetchScalarGridSpec(
            num_scalar_prefetch=4,
            grid=(nf, nb),
            in_specs=[
                pl.BlockSpec((EXPERT_ROWS, D_MODEL), lambda f, b, fi, pi, pe, meta: (jnp.minimum(b, meta[0] - 1), 0)),
                pl.BlockSpec(memory_space=pl.ANY),
            ],
            out_specs=pl.BlockSpec((EXPERT_ROWS, tf), lambda f, b, fi, pi, pe, meta: (b, f)),
            scratch_shapes=[pltpu.VMEM((2, 2, D_MODEL, tf), F32), pltpu.VMEM((2, D_MODEL, tf), BF16),
                            pltpu.SemaphoreType.DMA((2,))],
        ),
        out_shape=jax.ShapeDtypeStruct((cap, D_EXPERT), BF16),
        compiler_params=_cparams(("arbitrary", "arbitrary")),
        name="expert_up",
    )(*sched, xs, w_ei)


def _expert_down_kernel(layer, first_ref, pidx_ref, pe_ref, meta_ref, h_ref, w_hbm, o_ref, stage, wbf, sems):
    b = pl.program_id(0)

    def panel_copies(fq, k, slot):
        del fq
        return [pltpu.make_async_copy(w_hbm.at[layer, pe_ref[k]], stage.at[slot], sems.at[slot])]

    def on_arrival(slot):
        _cast_rows(stage.at[slot], wbf, chunk=128)

    _stream_panels(first_ref, pidx_ref, meta_ref, 0, 1, b, panel_copies, on_arrival)

    @pl.when(b < meta_ref[0])
    def _():
        o_ref[...] = jnp.dot(h_ref[...], wbf[...], preferred_element_type=F32)

    @pl.when(b >= meta_ref[0])
    def _():
        o_ref[...] = jnp.zeros_like(o_ref)


def _expert_down(hs, w_eo, layer, sched):
    cap = hs.shape[0]
    nb = cap // EXPERT_ROWS
    return pl.pallas_call(
        functools.partial(_expert_down_kernel, layer),
        grid_spec=pltpu.PrefetchScalarGridSpec(
            num_scalar_prefetch=4,
            grid=(nb,),
            in_specs=[
                pl.BlockSpec((EXPERT_ROWS, D_EXPERT), lambda b, fi, pi, pe, meta: (jnp.minimum(b, meta[0] - 1), 0)),
                pl.BlockSpec(memory_space=pl.ANY),
            ],
            out_specs=pl.BlockSpec((EXPERT_ROWS, D_MODEL), lambda b, fi, pi, pe, meta: (b, 0)),
            scratch_shapes=[pltpu.VMEM((2, D_EXPERT, D_MODEL), F32), pltpu.VMEM((D_EXPERT, D_MODEL), BF16),
                            pltpu.SemaphoreType.DMA((2,))],
        ),
        out_shape=jax.ShapeDtypeStruct((cap, D_MODEL), F32),
        compiler_params=_cparams(("arbitrary",)),
        name="expert_down",
    )(*sched, hs, w_eo)


def _combine_kernel(tm, split, dest_ref, x_ref, gate_ref, g_ref, ys_hbm, *rest):
    if split is None:
        x2_ref, xn_ref, ybuf, sem = rest
    else:
        yp_ref, ysm_ref, ybuf, sem = rest
    i = pl.program_id(0)

    def issue(tile):
        slot = tile % 2
        for k in range(TOP_K):
            _gather_rows(dest_ref, tile * tm * TOP_K + k, TOP_K, tm, ys_hbm, ybuf.at[slot, k], sem.at[slot])

    @pl.when(i == 0)
    def _():
        issue(i)

    @pl.when(i + 1 < pl.num_programs(0))
    def _():
        issue(i + 1)

    slot = i % 2
    for k in range(TOP_K):
        _wait_rows(tm, ys_hbm, ybuf.at[slot, k], sem.at[slot])
    gates = gate_ref[...]
    x2 = x_ref[...] + (ybuf[slot, 0] * gates[:, 0:1] + ybuf[slot, 1] * gates[:, 1:2])
    if split is None:
        x2_ref[...] = x2
        xn_ref[...] = _rms(x2, g_ref[...]).astype(xn_ref.dtype)
    else:
        y = _rms(x2, g_ref[...])

        @pl.when(i < split)
        def _():
            yp_ref[...] = y

        @pl.when(i >= split)
        def _():
            ysm_ref[...] = y


def _combine(x1, gates, dest_flat, ys, g, final_split=None, tm=128):
    t = x1.shape[0]
    in_specs = [
        pl.BlockSpec((tm, D_MODEL), lambda i, d: (i, 0)),
        pl.BlockSpec((tm, LANES), lambda i, d: (i, 0)),
        pl.BlockSpec((1, D_MODEL), lambda i, d: (0, 0)),
        pl.BlockSpec(memory_space=pl.ANY),
    ]
    if final_split is None:
        split = None
        out_specs = [pl.BlockSpec((tm, D_MODEL), lambda i, d: (i, 0)),
                     pl.BlockSpec((tm, D_MODEL), lambda i, d: (i, 0))]
        out_shape = [jax.ShapeDtypeStruct((t, D_MODEL), F32), jax.ShapeDtypeStruct((t, D_MODEL), BF16)]
    else:
        split = final_split // tm
        out_specs = [pl.BlockSpec((tm, D_MODEL), lambda i, d: (jnp.minimum(i, split - 1), 0)),
                     pl.BlockSpec((tm, D_MODEL), lambda i, d: (jnp.maximum(i - split, 0), 0))]
        out_shape = [jax.ShapeDtypeStruct((final_split, D_MODEL), F32),
                     jax.ShapeDtypeStruct((t - final_split, D_MODEL), F32)]
    return pl.pallas_call(
        functools.partial(_combine_kernel, tm, split),
        grid_spec=pltpu.PrefetchScalarGridSpec(
            num_scalar_prefetch=1,
            grid=(t // tm,),
            in_specs=in_specs,
            out_specs=out_specs,
            scratch_shapes=[pltpu.VMEM((2, TOP_K, tm, D_MODEL), F32), pltpu.SemaphoreType.DMA((2,))],
        ),
        out_shape=out_shape,
        compiler_params=_cparams(("arbitrary",)),
        name="combine",
    )(dest_flat, x1, gates, g, ys)


def _routing_tables(eid, n_blocks):
    e_flat = eid.reshape(-1)
    onehot = (e_flat[:, None] == jnp.arange(N_EXPERTS, dtype=jnp.int32)[None, :]).astype(jnp.int32)
    csum = jnp.cumsum(onehot, axis=0)
    rank = jnp.sum(csum * onehot, axis=1) - 1
    counts = csum[-1]
    blocks = (counts + EXPERT_ROWS - 1) // EXPERT_ROWS
    blk_end = jnp.cumsum(blocks)
    blk_start = blk_end - blocks
    dest = (blk_start[e_flat] * EXPERT_ROWS + rank).astype(jnp.int32)
    n_used = blk_end[-1]
    tok_sorted = jnp.zeros((n_blocks * EXPERT_ROWS,), jnp.int32).at[dest].set(
        jnp.arange(dest.shape[0], dtype=jnp.int32) // TOP_K, unique_indices=True)

    has = blocks > 0
    order = jnp.cumsum(has.astype(jnp.int32)) - 1
    panel_expert = jnp.argsort(jnp.logical_not(has), stable=True)
    blk_ids = jnp.arange(n_blocks, dtype=jnp.int32)
    blk_e = jnp.minimum(jnp.searchsorted(blk_end, blk_ids, side="right"), N_EXPERTS - 1)
    used = blk_ids < n_used
    first = jnp.logical_and(used, blk_ids == blk_start[blk_e])
    pidx = jnp.where(used, order[blk_e], 0)
    meta = jnp.stack([n_used, jnp.sum(has.astype(jnp.int32))])
    i32 = lambda a: a.astype(jnp.int32)
    return dest, tok_sorted, (i32(first), i32(pidx), i32(panel_expert), i32(meta))


def _gate_tables(w_spatial, b_spatial, sample_len):
    depth = w_spatial.shape[0]
    tril = jnp.tril(jnp.ones((CHUNK, CHUNK), dtype=bool))
    full = jnp.where(tril[None, None], w_spatial, 0)
    reps = CHUNK // sample_len
    small = jnp.where(tril[None, None, :sample_len, :sample_len], w_spatial[:, :, :sample_len, :sample_len], 0)
    eye = jnp.eye(reps, dtype=w_spatial.dtype)
    blockdiag = jnp.einsum("ab,lhts->lhatbs", eye, small).reshape(depth, SG_HEADS, CHUNK, CHUNK)
    mats = jnp.stack([full, blockdiag], axis=1).astype(BF16)
    bias_full = jnp.transpose(b_spatial, (0, 2, 1))
    bias_small = jnp.tile(bias_full[:, :sample_len, :], (1, reps, 1))
    bias = jnp.stack([bias_full, bias_small], axis=1)
    bias = jnp.repeat(bias, SG_HEAD_DIM, axis=-1)
    return mats, bias


def kernel(x_prompt, x_sample, state_pool, state_conv, norm_mix, w_in, w_pool_group, pool_scale, sg_norm_g, sg_norm_b, w_spatial, b_spatial, conv_w, w_branch, w_out, norm_ffn, w_router_group, b_router_group, w_router_expert, b_router_expert, w_expert_in, w_expert_out, norm_final):
    depth = w_in.shape[0]
    bp, lp, _ = x_prompt.shape
    bs, ls, _ = x_sample.shape
    tp, ts = bp * lp, bs * ls
    t = tp + ts
    assert lp % ROW_TILE == 0 and ROW_TILE % ls == 0 and bs % (ROW_TILE // ls) == 0 and ls >= CONV_STATE

    xp = x_prompt.reshape(tp, D_MODEL)
    xs = x_sample.reshape(ts, D_MODEL)

    gate_mats, gate_bias = _gate_tables(w_spatial, b_spatial, ls)
    wg_bf = w_pool_group.astype(BF16)
    vec = lambda a: a.reshape(depth, 1, -1)
    wr = jnp.concatenate([w_router_group, w_router_expert,
                          jnp.zeros((depth, D_MODEL, LANES - N_EXPERT_GROUPS - N_EXPERTS), F32)], axis=-1).astype(BF16)
    br = jnp.concatenate([b_router_group, b_router_expert,
                          jnp.zeros((depth, LANES - N_EXPERT_GROUPS - N_EXPERTS), F32)], axis=-1).reshape(depth, 1, LANES)
    pool_hist = jnp.pad(state_pool, ((0, 0), (0, 0), (POOL_HALO - POOL_STATE, 0), (0, 0))).reshape(
        depth, bs * POOL_HALO, MIX_WIDTH)
    conv_hist = jnp.pad(state_conv, ((0, 0), (0, 0), (CONV_HALO - CONV_STATE, 0), (0, 0))).reshape(
        depth, bs * CONV_HALO, MIX_WIDTH)

    n_assign = t * TOP_K
    n_blocks = n_assign // EXPERT_ROWS + N_EXPERTS

    pool_p, pool_s, conv_p, conv_s, v_out = [], [], [], [], []
    x_parts = (xp, xs)
    xn = _norm_in(xp, xs, norm_mix[0].reshape(1, D_MODEL))
    y_p = y_s = None
    for l in range(depth):
        proj = _in_proj(xn, w_in, l)
        oa, ob, oc, z_tail, a_tail, v_s, z_s = _branches(
            proj, l, bp, lp, bs, ls, pool_hist, conv_hist, wg_bf, vec(pool_scale), vec(sg_norm_g),
            vec(sg_norm_b), gate_mats, gate_bias, conv_w)
        merged = _merge(oa, ob, oc, w_branch, proj, l)
        x1 = _out_proj(merged, w_out, l, x_parts)

        xn2, eid, gates = _router(x1, norm_ffn[l].reshape(1, D_MODEL), wr[l], br[l])
        dest, tok_sorted, sched = _routing_tables(eid[:, :TOP_K], n_blocks)
        xs_sorted = _dispatch(xn2, tok_sorted, sched[3])
        hs = _expert_up(xs_sorted, w_expert_in, l, sched)
        ys = _expert_down(hs, w_expert_out, l, sched)
        if l + 1 < depth:
            x2, xn = _combine(x1, gates, dest, ys, norm_mix[l + 1].reshape(1, D_MODEL))
            x_parts = (x2,)
        else:
            y_p, y_s = _combine(x1, gates, dest, ys, norm_final.reshape(1, D_MODEL), final_split=tp)

        a_s = proj[tp:, :MIX_WIDTH].reshape(bs, ls, MIX_WIDTH)
        pool_p.append(a_tail[:, POOL_HALO - POOL_STATE:])
        pool_s.append(jnp.concatenate([state_pool[l], a_s], axis=1)[:, -POOL_STATE:])
        conv_p.append(z_tail[:, CONV_HALO - CONV_STATE:])
        conv_s.append(z_s.reshape(bs, ls, MIX_WIDTH)[:, ls - CONV_STATE:])
        v_out.append(v_s.reshape(bs, ls, MIX_WIDTH))

    return (y_p.reshape(bp, lp, D_MODEL), y_s.reshape(bs, ls, D_MODEL), jnp.stack(pool_p), jnp.stack(pool_s),
            jnp.stack(conv_p), jnp.stack(conv_s), jnp.stack(v_out))
```

```python
import functools

import jax
import jax.numpy as jnp
from jax import lax
from jax.experimental import pallas as pl
from jax.experimental.pallas import tpu as pltpu

F32 = jnp.float32
BF16 = jnp.bfloat16

D_MODEL = 4096
MIX_WIDTH = D_MODEL // 2
POOL_GROUPS = 4
POOL_WINDOWS = (2, 4, 8, 16)
POOL_GROUP_DIM = MIX_WIDTH // POOL_GROUPS
POOL_STATE = 15
SG_HEADS = 8
SG_HEAD_DIM = MIX_WIDTH // SG_HEADS
CHUNK = 128
CONV_STATE = 2
IN_COLS = MIX_WIDTH * 6 + 3 * D_MODEL
GATE_COL0 = MIX_WIDTH * 6
N_EXPERT_GROUPS = 4
EXPERTS_PER_GROUP = 8
N_EXPERTS = N_EXPERT_GROUPS * EXPERTS_PER_GROUP
TOP_K = 2
D_EXPERT = D_MODEL // 4
PAST_LEN = 16384
EPS = 1e-6

LANES = 128
SUBLANES = 8
POOL_HALO = 16
CONV_HALO = 8
ROW_TILE = 128
EXPERT_ROWS = 256
PACKED_WIDTH = D_MODEL // 2
VMEM_LIMIT = 56 * 1024 * 1024


def _cparams(sem):
    return pltpu.CompilerParams(dimension_semantics=sem, vmem_limit_bytes=VMEM_LIMIT)


def _gelu_tanh(x):
    c = 0.7978845608028654
    return 0.5 * x * (1.0 + jnp.tanh(c * (x + 0.044715 * (x * x * x))))


def _rms(x, g):
    return x * lax.rsqrt(jnp.mean(x * x, axis=-1, keepdims=True) + EPS) * g


def _norm_in_kernel(n_prompt_tiles, xp_ref, xs_ref, g_ref, o_ref):
    i = pl.program_id(0)

    @pl.when(i < n_prompt_tiles)
    def _():
        o_ref[...] = _rms(xp_ref[...], g_ref[...]).astype(BF16)

    @pl.when(i >= n_prompt_tiles)
    def _():
        o_ref[...] = _rms(xs_ref[...], g_ref[...]).astype(BF16)


def _norm_in(xp, xs, g, tm=256):
    tp, ts = xp.shape[0], xs.shape[0]
    npt, nst = tp // tm, ts // tm
    return pl.pallas_call(
        functools.partial(_norm_in_kernel, npt),
        grid=(npt + nst,),
        in_specs=[
            pl.BlockSpec((tm, D_MODEL), lambda i: (jnp.minimum(i, npt - 1), 0)),
            pl.BlockSpec((tm, D_MODEL), lambda i: (jnp.maximum(i - npt, 0), 0)),
            pl.BlockSpec((1, D_MODEL), lambda i: (0, 0)),
        ],
        out_specs=pl.BlockSpec((tm, D_MODEL), lambda i: (i, 0)),
        out_shape=jax.ShapeDtypeStruct((tp + ts, D_MODEL), BF16),
        compiler_params=_cparams(("arbitrary",)),
        name="norm_in",
    )(xp, xs, g)


def _mm(a, w):
    return lax.dot_general(a, w, (((1,), (0,)), ((), ())), preferred_element_type=F32)


def _panel_matmul_kernel(a_ref, w_ref, o_ref):
    o_ref[...] = _mm(a_ref[...], w_ref[...])


def _in_proj(xn, w_in, layer, tm=768, tn=1024):
    t = xn.shape[0]
    return pl.pallas_call(
        _panel_matmul_kernel,
        grid=(IN_COLS // tn, t // tm),
        in_specs=[
            pl.BlockSpec((tm, D_MODEL), lambda j, i: (i, 0)),
            pl.BlockSpec((None, D_MODEL, tn), lambda j, i: (layer, 0, j)),
        ],
        out_specs=pl.BlockSpec((tm, tn), lambda j, i: (i, j)),
        out_shape=jax.ShapeDtypeStruct((t, IN_COLS), F32),
        compiler_params=_cparams(("arbitrary", "arbitrary")),
        name="in_proj",
    )(xn, w_in)


def _window_rows(ext, n_seq, halo, rows):
    if n_seq == 1:
        return ext[halo:, :]
    c = ext.shape[-1]
    return ext.reshape(n_seq, halo + rows, c)[:, halo:, :].reshape(n_seq * rows, c)


def _stack_history(hist, cur, n_seq, halo, rows):
    if n_seq == 1:
        return jnp.concatenate([hist, cur], axis=0)
    c = cur.shape[-1]
    ext = jnp.concatenate([hist.reshape(n_seq, halo, c), cur.reshape(n_seq, rows, c)], axis=1)
    return ext.reshape(n_seq * (halo + rows), c)


def _branch_math(n_seq, rows, pos, a_ref, u_ref, v_ref, ci_ref, cb_ref, cc_ref, pool_hist, conv_hist,
                 wg_ref, ps_ref, lng_ref, lnb_ref, m_ref, bias_ref, cw_ref, oa_ref, ob_ref, oc_ref):
    for g, w in enumerate(POOL_WINDOWS):
        cols = slice(g * POOL_GROUP_DIM, (g + 1) * POOL_GROUP_DIM)
        a_g = a_ref[:, cols]
        s = _stack_history(pool_hist[:, cols], a_g, n_seq, POOL_HALO, rows)
        k = 1
        while k < w:
            s = s + pltpu.roll(s, k, 0)
            k *= 2
        s = _window_rows(s, n_seq, POOL_HALO, rows)
        cnt = jnp.minimum(w, pos + 1).astype(F32)
        pooled = (s / cnt - a_g).astype(BF16)
        out = jnp.dot(pooled, wg_ref[g], preferred_element_type=F32) * ps_ref[:, cols]
        oa_ref[:, cols] = out.astype(BF16)

    vg = _gelu_tanh(v_ref[...])
    mu = jnp.mean(vg, axis=-1, keepdims=True)
    xc = vg - mu
    v = xc * lax.rsqrt(jnp.mean(xc * xc, axis=-1, keepdims=True) + EPS) * lng_ref[...] + lnb_ref[...]
    vb = v.astype(BF16)
    for h in range(SG_HEADS):
        cols = slice(h * SG_HEAD_DIM, (h + 1) * SG_HEAD_DIM)
        sp = jnp.dot(m_ref[h], vb[:, cols], preferred_element_type=F32) + bias_ref[:, cols]
        ob_ref[:, cols] = (_gelu_tanh(u_ref[:, cols]) * sp).astype(BF16)

    z = cc_ref[...] * ci_ref[...]
    e = _stack_history(conv_hist, z, n_seq, CONV_HALO, rows)
    y = cw_ref[0:1, :] * pltpu.roll(e, 2, 0)
    y = y + cw_ref[1:2, :] * pltpu.roll(e, 1, 0)
    y = y + cw_ref[2:3, :] * e
    oc_ref[...] = (cb_ref[...] * _window_rows(y, n_seq, CONV_HALO, rows)).astype(BF16)
    return v, z


def _branch_kernel(tiles_per_seq, n_prompt_tiles, seq_per_tile, sample_len,
                   a_ref, u_ref, v_ref, ci_ref, cb_ref, cc_ref, ph_ref, ch_ref, wg_ref, ps_ref, lng_ref, lnb_ref,
                   m_ref, bias_ref, cw_ref, oa_ref, ob_ref, oc_ref, zl_ref, al_ref, vo_ref, zo_ref, pool_hist, conv_hist):
    i = pl.program_id(0)
    shared = (wg_ref, ps_ref, lng_ref, lnb_ref, m_ref, bias_ref, cw_ref, oa_ref, ob_ref, oc_ref)

    @pl.when(i < n_prompt_tiles)
    def _():
        t = i % tiles_per_seq

        @pl.when(t == 0)
        def _():
            pool_hist[...] = jnp.zeros_like(pool_hist)
            conv_hist[...] = jnp.zeros_like(conv_hist)

        pos = t * ROW_TILE + lax.broadcasted_iota(jnp.int32, (ROW_TILE, 1), 0)
        _, z = _branch_math(1, ROW_TILE, pos, a_ref, u_ref, v_ref, ci_ref, cb_ref, cc_ref, pool_hist[...],
                            conv_hist[...], *shared)
        pool_hist[...] = a_ref[ROW_TILE - POOL_HALO:, :]
        conv_hist[...] = z[ROW_TILE - CONV_HALO:, :]
        zl_ref[...] = z[ROW_TILE - CONV_HALO:, :]
        al_ref[...] = a_ref[ROW_TILE - POOL_HALO:, :]

    @pl.when(i >= n_prompt_tiles)
    def _():
        r = lax.broadcasted_iota(jnp.int32, (seq_per_tile * sample_len, 1), 0)
        pos = PAST_LEN + (r % sample_len)
        v, z = _branch_math(seq_per_tile, sample_len, pos, a_ref, u_ref, v_ref, ci_ref, cb_ref, cc_ref,
                            ph_ref[...], ch_ref[...], *shared)
        vo_ref[...] = v
        zo_ref[...] = z


def _branches(proj, layer, n_prompt_seq, prompt_len, n_sample_seq, sample_len, pool_state, conv_state,
              wg_bf, pool_scale, ln_g, ln_b, gate_mats, gate_bias, conv_w):
    t_total = proj.shape[0]
    tiles_per_seq = prompt_len // ROW_TILE
    npt = n_prompt_seq * tiles_per_seq
    seq_per_tile = ROW_TILE // sample_len
    nst = n_sample_seq // seq_per_tile
    sample_rows = n_sample_seq * sample_len
    act = jax.ShapeDtypeStruct((t_total, MIX_WIDTH), BF16)
    sample_f32 = jax.ShapeDtypeStruct((sample_rows, MIX_WIDTH), F32)

    def sample_tile(i):
        return jnp.maximum(i - npt, 0)

    def kind(i):
        return jnp.where(i >= npt, 1, 0)

    slab_specs = [pl.BlockSpec((ROW_TILE, MIX_WIDTH), functools.partial(lambda c, i: (i, c), c)) for c in range(6)]
    vec_spec = pl.BlockSpec((None, 1, MIX_WIDTH), lambda i: (layer, 0, 0))
    in_specs = slab_specs + [
        pl.BlockSpec((None, seq_per_tile * POOL_HALO, MIX_WIDTH), lambda i: (layer, sample_tile(i), 0)),
        pl.BlockSpec((None, seq_per_tile * CONV_HALO, MIX_WIDTH), lambda i: (layer, sample_tile(i), 0)),
        pl.BlockSpec((None, POOL_GROUPS, POOL_GROUP_DIM, POOL_GROUP_DIM), lambda i: (layer, 0, 0, 0)),
        vec_spec, vec_spec, vec_spec,
        pl.BlockSpec((None, None, SG_HEADS, CHUNK, CHUNK), lambda i: (layer, kind(i), 0, 0, 0)),
        pl.BlockSpec((None, None, CHUNK, MIX_WIDTH), lambda i: (layer, kind(i), 0, 0)),
        pl.BlockSpec((None, 3, MIX_WIDTH), lambda i: (layer, 0, 0)),
    ]
    row_spec = pl.BlockSpec((ROW_TILE, MIX_WIDTH), lambda i: (i, 0))
    out_specs = [
        row_spec, row_spec, row_spec,
        pl.BlockSpec((None, CONV_HALO, MIX_WIDTH), lambda i: (jnp.minimum(i // tiles_per_seq, n_prompt_seq - 1), 0, 0)),
        pl.BlockSpec((None, POOL_HALO, MIX_WIDTH), lambda i: (jnp.minimum(i // tiles_per_seq, n_prompt_seq - 1), 0, 0)),
        pl.BlockSpec((ROW_TILE, MIX_WIDTH), lambda i: (sample_tile(i), 0)),
        pl.BlockSpec((ROW_TILE, MIX_WIDTH), lambda i: (sample_tile(i), 0)),
    ]
    return pl.pallas_call(
        functools.partial(_branch_kernel, tiles_per_seq, npt, seq_per_tile, sample_len),
        grid=(npt + nst,),
        in_specs=in_specs,
        out_specs=out_specs,
        out_shape=[act, act, act, jax.ShapeDtypeStruct((n_prompt_seq, CONV_HALO, MIX_WIDTH), F32),
                   jax.ShapeDtypeStruct((n_prompt_seq, POOL_HALO, MIX_WIDTH), F32), sample_f32, sample_f32],
        scratch_shapes=[pltpu.VMEM((POOL_HALO, MIX_WIDTH), F32), pltpu.VMEM((CONV_HALO, MIX_WIDTH), F32)],
        compiler_params=_cparams(("arbitrary",)),
        name="branches",
    )(proj, proj, proj, proj, proj, proj, pool_state, conv_state, wg_bf, pool_scale, ln_g, ln_b,
      gate_mats, gate_bias, conv_w)


def _merge_kernel(oa_ref, ob_ref, oc_ref, wa_ref, wb_ref, wc_ref, ga_ref, gb_ref, gc_ref, o_ref):
    m = jax.nn.sigmoid(ga_ref[...]) * _mm(oa_ref[...], wa_ref[...])
    m = m + jax.nn.sigmoid(gb_ref[...]) * _mm(ob_ref[...], wb_ref[...])
    m = m + jax.nn.sigmoid(gc_ref[...]) * _mm(oc_ref[...], wc_ref[...])
    o_ref[...] = m.astype(BF16)


def _merge(oa, ob, oc, w_branch, proj, layer, tm=768, tn=512):
    t = oa.shape[0]
    act_spec = pl.BlockSpec((tm, MIX_WIDTH), lambda j, i: (i, 0))

    def wspec(b):
        return pl.BlockSpec((None, None, MIX_WIDTH, tn), lambda j, i: (layer, b, 0, j))

    def gspec(b):
        off = (GATE_COL0 + b * D_MODEL) // tn
        return pl.BlockSpec((tm, tn), lambda j, i: (i, off + j))

    return pl.pallas_call(
        _merge_kernel,
        grid=(D_MODEL // tn, t // tm),
        in_specs=[act_spec, act_spec, act_spec, wspec(0), wspec(1), wspec(2), gspec(0), gspec(1), gspec(2)],
        out_specs=pl.BlockSpec((tm, tn), lambda j, i: (i, j)),
        out_shape=jax.ShapeDtypeStruct((t, D_MODEL), BF16),
        compiler_params=_cparams(("arbitrary", "arbitrary")),
        name="merge",
    )(oa, ob, oc, w_branch, w_branch, w_branch, proj, proj, proj)


def _out_kernel2(n_prompt_tiles, a_ref, w_ref, xp_ref, xs_ref, o_ref):
    i = pl.program_id(1)
    h = _mm(a_ref[...], w_ref[...])

    @pl.when(i < n_prompt_tiles)
    def _():
        o_ref[...] = xp_ref[...] + h

    @pl.when(i >= n_prompt_tiles)
    def _():
        o_ref[...] = xs_ref[...] + h


def _out_kernel1(a_ref, w_ref, x_ref, o_ref):
    o_ref[...] = x_ref[...] + _mm(a_ref[...], w_ref[...])


def _out_proj(merged, w_out, layer, x_parts, tm=1024, tn=512):
    t = merged.shape[0]
    common = dict(
        grid=(D_MODEL // tn, t // tm),
        out_specs=pl.BlockSpec((tm, tn), lambda j, i: (i, j)),
        out_shape=jax.ShapeDtypeStruct((t, D_MODEL), F32),
        compiler_params=_cparams(("arbitrary", "arbitrary")),
        name="out_proj",
    )
    a_spec = pl.BlockSpec((tm, D_MODEL), lambda j, i: (i, 0))
    w_spec = pl.BlockSpec((None, D_MODEL, tn), lambda j, i: (layer, 0, j))
    if len(x_parts) == 1:
        return pl.pallas_call(
            _out_kernel1,
            in_specs=[a_spec, w_spec, pl.BlockSpec((tm, tn), lambda j, i: (i, j))],
            **common,
        )(merged, w_out, x_parts[0])
    xp, xs = x_parts
    npt = xp.shape[0] // tm
    return pl.pallas_call(
        functools.partial(_out_kernel2, npt),
        in_specs=[
            a_spec, w_spec,
            pl.BlockSpec((tm, tn), lambda j, i: (jnp.minimum(i, npt - 1), j)),
            pl.BlockSpec((tm, tn), lambda j, i: (jnp.maximum(i - npt, 0), j)),
        ],
        **common,
    )(merged, w_out, xp, xs)


def _router_kernel(x_ref, g_ref, wr_ref, br_ref, xn_ref, eid_ref, gate_ref):
    xb = _rms(x_ref[...], g_ref[...]).astype(BF16)
    lo = pltpu.bitcast(xb[:, :PACKED_WIDTH].astype(F32), jnp.uint32) >> 16
    hi = pltpu.bitcast(xb[:, PACKED_WIDTH:].astype(F32), jnp.uint32) & jnp.uint32(0xFFFF0000)
    xn_ref[...] = hi | lo
    logits = jnp.dot(xb, wr_ref[...], preferred_element_type=F32) + br_ref[...]
    tm = logits.shape[0]
    lane = lax.broadcasted_iota(jnp.int32, (tm, LANES), 1)
    neg = jnp.float32(-jnp.inf)
    far = jnp.int32(LANES)

    is_g = lane < N_EXPERT_GROUPS
    glog = jnp.where(is_g, logits, neg)
    gmax = jnp.max(glog, axis=1, keepdims=True)
    gsel = jnp.min(jnp.where(glog == gmax, lane, far), axis=1, keepdims=True)
    gsum = jnp.sum(jnp.where(is_g, jnp.exp(glog - gmax), 0.0), axis=1, keepdims=True)
    gp = 1.0 / gsum

    lo = N_EXPERT_GROUPS + gsel * EXPERTS_PER_GROUP
    in_grp = jnp.logical_and(lane >= lo, lane < lo + EXPERTS_PER_GROUP)
    el = jnp.where(in_grp, logits, neg)
    m1 = jnp.max(el, axis=1, keepdims=True)
    i1 = jnp.min(jnp.where(el == m1, lane, far), axis=1, keepdims=True)
    el2 = jnp.where(lane == i1, neg, el)
    m2 = jnp.max(el2, axis=1, keepdims=True)
    i2 = jnp.min(jnp.where(el2 == m2, lane, far), axis=1, keepdims=True)
    e2 = jnp.exp(m2 - m1)
    den = 1.0 + e2
    g1 = gp * (1.0 / den)
    g2 = gp * (e2 / den)

    eid_ref[...] = jnp.where(lane == 0, i1 - N_EXPERT_GROUPS, jnp.where(lane == 1, i2 - N_EXPERT_GROUPS, 0))
    gate_ref[...] = jnp.where(lane == 0, g1, jnp.where(lane == 1, g2, 0.0))


def _router(x1, g, wr, br, tm=256):
    t = x1.shape[0]
    return pl.pallas_call(
        _router_kernel,
        grid=(t // tm,),
        in_specs=[
            pl.BlockSpec((tm, D_MODEL), lambda i: (i, 0)),
            pl.BlockSpec((1, D_MODEL), lambda i: (0, 0)),
            pl.BlockSpec((D_MODEL, LANES), lambda i: (0, 0)),
            pl.BlockSpec((1, LANES), lambda i: (0, 0)),
        ],
        out_specs=[
            pl.BlockSpec((tm, PACKED_WIDTH), lambda i: (i, 0)),
            pl.BlockSpec((tm, LANES), lambda i: (i, 0)),
            pl.BlockSpec((tm, LANES), lambda i: (i, 0)),
        ],
        out_shape=[
            jax.ShapeDtypeStruct((t, PACKED_WIDTH), jnp.uint32),
            jax.ShapeDtypeStruct((t, LANES), jnp.int32),
            jax.ShapeDtypeStruct((t, LANES), F32),
        ],
        compiler_params=_cparams(("arbitrary",)),
        name="router",
    )(x1, g, wr, br)


def _gather_rows(idx_ref, idx_base, idx_stride, n_rows, src_hbm, dst, sem):
    def body(r, c):
        row = idx_ref[idx_base + r * idx_stride]
        pltpu.make_async_copy(src_hbm.at[pl.ds(row, 1)], dst.at[pl.ds(r, 1)], sem).start()
        return c

    lax.fori_loop(0, n_rows, body, 0, unroll=8)


def _wait_rows(n_rows, src_hbm, dst, sem):
    pltpu.make_async_copy(src_hbm.at[pl.ds(0, n_rows)], dst, sem).wait()


def _dispatch_kernel(tok_ref, nb_ref, x_hbm, o_ref, gbuf, sems):
    b = pl.program_id(0)
    nb = nb_ref[0]

    def issue(blk):
        slot = blk % 2
        _gather_rows(tok_ref, blk * EXPERT_ROWS, 1, EXPERT_ROWS, x_hbm, gbuf.at[slot], sems.at[slot])

    @pl.when(jnp.logical_and(b == 0, nb > 0))
    def _():
        issue(b)

    @pl.when(b + 1 < nb)
    def _():
        issue(b + 1)

    @pl.when(b < nb)
    def _():
        slot = b % 2
        _wait_rows(EXPERT_ROWS, x_hbm, gbuf.at[slot], sems.at[slot])
        words = gbuf[slot]
        o_ref[:, :PACKED_WIDTH] = pltpu.bitcast(words << 16, F32).astype(BF16)
        o_ref[:, PACKED_WIDTH:] = pltpu.bitcast(words & jnp.uint32(0xFFFF0000), F32).astype(BF16)

    @pl.when(b >= nb)
    def _():
        o_ref[...] = jnp.zeros_like(o_ref)


def _dispatch(xn, tok_sorted, n_blk):
    cap = tok_sorted.shape[0]
    return pl.pallas_call(
        _dispatch_kernel,
        grid_spec=pltpu.PrefetchScalarGridSpec(
            num_scalar_prefetch=2,
            grid=(cap // EXPERT_ROWS,),
            in_specs=[pl.BlockSpec(memory_space=pl.ANY)],
            out_specs=pl.BlockSpec((EXPERT_ROWS, D_MODEL), lambda b, tok, nbk: (b, 0)),
            scratch_shapes=[pltpu.VMEM((2, EXPERT_ROWS, PACKED_WIDTH), jnp.uint32), pltpu.SemaphoreType.DMA((2,))],
        ),
        out_shape=jax.ShapeDtypeStruct((cap, D_MODEL), BF16),
        compiler_params=_cparams(("arbitrary",)),
        name="dispatch",
    )(tok_sorted, n_blk, xn)


def _stream_panels(first_ref, pidx_ref, meta_ref, pass_idx, n_pass, b, n_slots, panel_copies):
    n_distinct = meta_ref[1]
    n_panels = n_pass * n_distinct
    ahead = n_slots - 1

    def start(q):
        for c in panel_copies(q // n_distinct, q % n_distinct, q % n_slots):
            c.start()

    @pl.when(jnp.logical_and(pass_idx == 0, b == 0))
    def _():
        for q in range(ahead):
            @pl.when(q < n_panels)
            def _():
                start(q)

    p = pass_idx * n_distinct + pidx_ref[b]
    slot = p % n_slots

    @pl.when(first_ref[b] == 1)
    def _():
        for c in panel_copies(pass_idx, pidx_ref[b], slot):
            c.wait()

        @pl.when(p + ahead < n_panels)
        def _():
            start(p + ahead)

    return slot


UP_SLOTS = 3
DOWN_SLOTS = 2


def _expert_up_kernel(layer, tf, first_ref, pidx_ref, pe_ref, meta_ref, x_ref, w_hbm, o_ref, stage, sems):
    f = pl.program_id(0)
    b = pl.program_id(1)

    def panel_copies(fq, k, slot):
        e = pe_ref[k]
        col = pl.multiple_of(fq * tf, tf)
        return [pltpu.make_async_copy(w_hbm.at[layer, e, :, pl.ds(half * D_EXPERT + col, tf)],
                                      stage.at[slot, half], sems.at[slot]) for half in range(2)]

    slot = _stream_panels(first_ref, pidx_ref, meta_ref, f, pl.num_programs(0), b, UP_SLOTS, panel_copies)

    @pl.when(b < meta_ref[0])
    def _():
        x = x_ref[...]
        h1 = _mm(x, stage[slot, 0])
        h3 = _mm(x, stage[slot, 1])
        o_ref[...] = (h1 * jax.nn.sigmoid(h1) * h3).astype(BF16)

    @pl.when(b >= meta_ref[0])
    def _():
        o_ref[...] = jnp.zeros_like(o_ref)


def _expert_up(xs, w_ei, layer, sched, tf=512):
    cap = xs.shape[0]
    nb = cap // EXPERT_ROWS
    nf = D_EXPERT // tf
    return pl.pallas_call(
        functools.partial(_expert_up_kernel, layer, tf),
        grid_spec=pltpu.PrefetchScalarGridSpec(
            num_scalar_prefetch=4,
            grid=(nf, nb),
            in_specs=[
                pl.BlockSpec((EXPERT_ROWS, D_MODEL), lambda f, b, fi, pi, pe, meta: (jnp.minimum(b, meta[0] - 1), 0)),
                pl.BlockSpec(memory_space=pl.ANY),
            ],
            out_specs=pl.BlockSpec((EXPERT_ROWS, tf), lambda f, b, fi, pi, pe, meta: (b, f)),
            scratch_shapes=[pltpu.VMEM((UP_SLOTS, 2, D_MODEL, tf), F32), pltpu.SemaphoreType.DMA((UP_SLOTS,))],
        ),
        out_shape=jax.ShapeDtypeStruct((cap, D_EXPERT), BF16),
        compiler_params=_cparams(("arbitrary", "arbitrary")),
        name="expert_up",
    )(*sched, xs, w_ei)


def _expert_down_kernel(layer, first_ref, pidx_ref, pe_ref, meta_ref, h_ref, w_hbm, o_ref, stage, sems):
    b = pl.program_id(0)

    def panel_copies(fq, k, slot):
        del fq
        return [pltpu.make_async_copy(w_hbm.at[layer, pe_ref[k]], stage.at[slot], sems.at[slot])]

    slot = _stream_panels(first_ref, pidx_ref, meta_ref, 0, 1, b, DOWN_SLOTS, panel_copies)

    @pl.when(b < meta_ref[0])
    def _():
        o_ref[...] = _mm(h_ref[...], stage[slot])

    @pl.when(b >= meta_ref[0])
    def _():
        o_ref[...] = jnp.zeros_like(o_ref)


def _expert_down(hs, w_eo, layer, sched):
    cap = hs.shape[0]
    nb = cap // EXPERT_ROWS
    return pl.pallas_call(
        functools.partial(_expert_down_kernel, layer),
        grid_spec=pltpu.PrefetchScalarGridSpec(
            num_scalar_prefetch=4,
            grid=(nb,),
            in_specs=[
                pl.BlockSpec((EXPERT_ROWS, D_EXPERT), lambda b, fi, pi, pe, meta: (jnp.minimum(b, meta[0] - 1), 0)),
                pl.BlockSpec(memory_space=pl.ANY),
            ],
            out_specs=pl.BlockSpec((EXPERT_ROWS, D_MODEL), lambda b, fi, pi, pe, meta: (b, 0)),
            scratch_shapes=[pltpu.VMEM((DOWN_SLOTS, D_EXPERT, D_MODEL), F32), pltpu.SemaphoreType.DMA((DOWN_SLOTS,))],
        ),
        out_shape=jax.ShapeDtypeStruct((cap, D_MODEL), F32),
        compiler_params=_cparams(("arbitrary",)),
        name="expert_down",
    )(*sched, hs, w_eo)


def _combine_kernel(tm, split, dest_ref, x_ref, gate_ref, g_ref, ys_hbm, *rest):
    if split is None:
        x2_ref, xn_ref, ybuf, sem = rest
    else:
        yp_ref, ysm_ref, ybuf, sem = rest
    i = pl.program_id(0)

    def issue(tile):
        slot = tile % 2
        for k in range(TOP_K):
            _gather_rows(dest_ref, tile * tm * TOP_K + k, TOP_K, tm, ys_hbm, ybuf.at[slot, k], sem.at[slot])

    @pl.when(i == 0)
    def _():
        issue(i)

    @pl.when(i + 1 < pl.num_programs(0))
    def _():
        issue(i + 1)

    slot = i % 2
    for k in range(TOP_K):
        _wait_rows(tm, ys_hbm, ybuf.at[slot, k], sem.at[slot])
    gates = gate_ref[...]
    x2 = x_ref[...] + (ybuf[slot, 0] * gates[:, 0:1] + ybuf[slot, 1] * gates[:, 1:2])
    if split is None:
        x2_ref[...] = x2
        xn_ref[...] = _rms(x2, g_ref[...]).astype(xn_ref.dtype)
    else:
        y = _rms(x2, g_ref[...])

        @pl.when(i < split)
        def _():
            yp_ref[...] = y

        @pl.when(i >= split)
        def _():
            ysm_ref[...] = y


def _combine(x1, gates, dest_flat, ys, g, final_split=None, tm=128):
    t = x1.shape[0]
    in_specs = [
        pl.BlockSpec((tm, D_MODEL), lambda i, d: (i, 0)),
        pl.BlockSpec((tm, LANES), lambda i, d: (i, 0)),
        pl.BlockSpec((1, D_MODEL), lambda i, d: (0, 0)),
        pl.BlockSpec(memory_space=pl.ANY),
    ]
    if final_split is None:
        split = None
        out_specs = [pl.BlockSpec((tm, D_MODEL), lambda i, d: (i, 0)),
                     pl.BlockSpec((tm, D_MODEL), lambda i, d: (i, 0))]
        out_shape = [jax.ShapeDtypeStruct((t, D_MODEL), F32), jax.ShapeDtypeStruct((t, D_MODEL), BF16)]
    else:
        split = final_split // tm
        out_specs = [pl.BlockSpec((tm, D_MODEL), lambda i, d: (jnp.minimum(i, split - 1), 0)),
                     pl.BlockSpec((tm, D_MODEL), lambda i, d: (jnp.maximum(i - split, 0), 0))]
        out_shape = [jax.ShapeDtypeStruct((final_split, D_MODEL), F32),
                     jax.ShapeDtypeStruct((t - final_split, D_MODEL), F32)]
    return pl.pallas_call(
        functools.partial(_combine_kernel, tm, split),
        grid_spec=pltpu.PrefetchScalarGridSpec(
            num_scalar_prefetch=1,
            grid=(t // tm,),
            in_specs=in_specs,
            out_specs=out_specs,
            scratch_shapes=[pltpu.VMEM((2, TOP_K, tm, D_MODEL), F32), pltpu.SemaphoreType.DMA((2,))],
        ),
        out_shape=out_shape,
        compiler_params=_cparams(("arbitrary",)),
        name="combine",
    )(dest_flat, x1, gates, g, ys)


def _routing_tables(eid, n_blocks):
    e_flat = eid.reshape(-1)
    onehot = (e_flat[:, None] == jnp.arange(N_EXPERTS, dtype=jnp.int32)[None, :]).astype(jnp.int32)
    csum = jnp.cumsum(onehot, axis=0)
    rank = jnp.sum(csum * onehot, axis=1) - 1
    counts = csum[-1]
    blocks = (counts + EXPERT_ROWS - 1) // EXPERT_ROWS
    blk_end = jnp.cumsum(blocks)
    blk_start = blk_end - blocks
    dest = (blk_start[e_flat] * EXPERT_ROWS + rank).astype(jnp.int32)
    n_used = blk_end[-1]
    tok_sorted = jnp.zeros((n_blocks * EXPERT_ROWS,), jnp.int32).at[dest].set(
        jnp.arange(dest.shape[0], dtype=jnp.int32) // TOP_K, unique_indices=True)

    has = blocks > 0
    order = jnp.cumsum(has.astype(jnp.int32)) - 1
    panel_expert = jnp.argsort(jnp.logical_not(has), stable=True)
    blk_ids = jnp.arange(n_blocks, dtype=jnp.int32)
    blk_e = jnp.minimum(jnp.searchsorted(blk_end, blk_ids, side="right"), N_EXPERTS - 1)
    used = blk_ids < n_used
    first = jnp.logical_and(used, blk_ids == blk_start[blk_e])
    pidx = jnp.where(used, order[blk_e], 0)
    meta = jnp.stack([n_used, jnp.sum(has.astype(jnp.int32))])
    i32 = lambda a: a.astype(jnp.int32)
    return dest, tok_sorted, (i32(first), i32(pidx), i32(panel_expert), i32(meta))


def _gate_tables(w_spatial, b_spatial, sample_len):
    depth = w_spatial.shape[0]
    tril = jnp.tril(jnp.ones((CHUNK, CHUNK), dtype=bool))
    full = jnp.where(tril[None, None], w_spatial, 0)
    reps = CHUNK // sample_len
    small = jnp.where(tril[None, None, :sample_len, :sample_len], w_spatial[:, :, :sample_len, :sample_len], 0)
    eye = jnp.eye(reps, dtype=w_spatial.dtype)
    blockdiag = jnp.einsum("ab,lhts->lhatbs", eye, small).reshape(depth, SG_HEADS, CHUNK, CHUNK)
    mats = jnp.stack([full, blockdiag], axis=1).astype(BF16)
    bias_full = jnp.transpose(b_spatial, (0, 2, 1))
    bias_small = jnp.tile(bias_full[:, :sample_len, :], (1, reps, 1))
    bias = jnp.stack([bias_full, bias_small], axis=1)
    bias = jnp.repeat(bias, SG_HEAD_DIM, axis=-1)
    return mats, bias


def kernel(x_prompt, x_sample, state_pool, state_conv, norm_mix, w_in, w_pool_group, pool_scale, sg_norm_g, sg_norm_b, w_spatial, b_spatial, conv_w, w_branch, w_out, norm_ffn, w_router_group, b_router_group, w_router_expert, b_router_expert, w_expert_in, w_expert_out, norm_final):
    depth = w_in.shape[0]
    bp, lp, _ = x_prompt.shape
    bs, ls, _ = x_sample.shape
    tp, ts = bp * lp, bs * ls
    t = tp + ts
    assert lp % ROW_TILE == 0 and ROW_TILE % ls == 0 and bs % (ROW_TILE // ls) == 0 and ls >= CONV_STATE

    xp = x_prompt.reshape(tp, D_MODEL)
    xs = x_sample.reshape(ts, D_MODEL)

    gate_mats, gate_bias = _gate_tables(w_spatial, b_spatial, ls)
    wg_bf = w_pool_group.astype(BF16)
    vec = lambda a: a.reshape(depth, 1, -1)
    wr = jnp.concatenate([w_router_group, w_router_expert,
                          jnp.zeros((depth, D_MODEL, LANES - N_EXPERT_GROUPS - N_EXPERTS), F32)], axis=-1).astype(BF16)
    br = jnp.concatenate([b_router_group, b_router_expert,
                          jnp.zeros((depth, LANES - N_EXPERT_GROUPS - N_EXPERTS), F32)], axis=-1).reshape(depth, 1, LANES)
    pool_hist = jnp.pad(state_pool, ((0, 0), (0, 0), (POOL_HALO - POOL_STATE, 0), (0, 0))).reshape(
        depth, bs * POOL_HALO, MIX_WIDTH)
    conv_hist = jnp.pad(state_conv, ((0, 0), (0, 0), (CONV_HALO - CONV_STATE, 0), (0, 0))).reshape(
        depth, bs * CONV_HALO, MIX_WIDTH)

    n_assign = t * TOP_K
    n_blocks = n_assign // EXPERT_ROWS + N_EXPERTS

    pool_p, pool_s, conv_p, conv_s, v_out = [], [], [], [], []
    x_parts = (xp, xs)
    xn = _norm_in(xp, xs, norm_mix[0].reshape(1, D_MODEL))
    y_p = y_s = None
    for l in range(depth):
        proj = _in_proj(xn, w_in, l)
        oa, ob, oc, z_tail, a_tail, v_s, z_s = _branches(
            proj, l, bp, lp, bs, ls, pool_hist, conv_hist, wg_bf, vec(pool_scale), vec(sg_norm_g),
            vec(sg_norm_b), gate_mats, gate_bias, conv_w)
        merged = _merge(oa, ob, oc, w_branch, proj, l)
        x1 = _out_proj(merged, w_out, l, x_parts)

        xn2, eid, gates = _router(x1, norm_ffn[l].reshape(1, D_MODEL), wr[l], br[l])
        dest, tok_sorted, sched = _routing_tables(eid[:, :TOP_K], n_blocks)
        xs_sorted = _dispatch(xn2, tok_sorted, sched[3])
        hs = _expert_up(xs_sorted, w_expert_in, l, sched)
        ys = _expert_down(hs, w_expert_out, l, sched)
        if l + 1 < depth:
            x2, xn = _combine(x1, gates, dest, ys, norm_mix[l + 1].reshape(1, D_MODEL))
            x_parts = (x2,)
        else:
            y_p, y_s = _combine(x1, gates, dest, ys, norm_final.reshape(1, D_MODEL), final_split=tp)

        a_s = proj[tp:, :MIX_WIDTH].reshape(bs, ls, MIX_WIDTH)
        pool_p.append(a_tail[:, POOL_HALO - POOL_STATE:])
        pool_s.append(jnp.concatenate([state_pool[l], a_s], axis=1)[:, -POOL_STATE:])
        conv_p.append(z_tail[:, CONV_HALO - CONV_STATE:])
        conv_s.append(z_s.reshape(bs, ls, MIX_WIDTH)[:, ls - CONV_STATE:])
        v_out.append(v_s.reshape(bs, ls, MIX_WIDTH))

    return (y_p.reshape(bp, lp, D_MODEL), y_s.reshape(bs, ls, D_MODEL), jnp.stack(pool_p), jnp.stack(pool_s),
            jnp.stack(conv_p), jnp.stack(conv_s), jnp.stack(v_out))
```

```python
import functools

import jax
import jax.numpy as jnp
from jax import lax
from jax.experimental import pallas as pl
from jax.experimental.pallas import tpu as pltpu

F32 = jnp.float32
BF16 = jnp.bfloat16

D_MODEL = 4096
MIX_WIDTH = D_MODEL // 2
POOL_GROUPS = 4
POOL_WINDOWS = (2, 4, 8, 16)
POOL_GROUP_DIM = MIX_WIDTH // POOL_GROUPS
POOL_STATE = 15
SG_HEADS = 8
SG_HEAD_DIM = MIX_WIDTH // SG_HEADS
CHUNK = 128
CONV_STATE = 2
IN_COLS = MIX_WIDTH * 6 + 3 * D_MODEL
GATE_COL0 = MIX_WIDTH * 6
N_EXPERT_GROUPS = 4
EXPERTS_PER_GROUP = 8
N_EXPERTS = N_EXPERT_GROUPS * EXPERTS_PER_GROUP
TOP_K = 2
D_EXPERT = D_MODEL // 4
PAST_LEN = 16384
EPS = 1e-6

LANES = 128
SUBLANES = 8
POOL_HALO = 16
CONV_HALO = 8
ROW_TILE = 128
EXPERT_ROWS = 256
PACKED_WIDTH = D_MODEL // 2
DMA_QUEUES = 2
VMEM_LIMIT = 56 * 1024 * 1024


def _cparams(sem):
    return pltpu.CompilerParams(dimension_semantics=sem, vmem_limit_bytes=VMEM_LIMIT)


def _gelu_tanh(x):
    c = 0.7978845608028654
    return 0.5 * x * (1.0 + jnp.tanh(c * (x + 0.044715 * (x * x * x))))


def _rms(x, g):
    return x * lax.rsqrt(jnp.mean(x * x, axis=-1, keepdims=True) + EPS) * g


def _norm_in_kernel(n_prompt_tiles, xp_ref, xs_ref, g_ref, o_ref):
    i = pl.program_id(0)

    @pl.when(i < n_prompt_tiles)
    def _():
        o_ref[...] = _rms(xp_ref[...], g_ref[...]).astype(BF16)

    @pl.when(i >= n_prompt_tiles)
    def _():
        o_ref[...] = _rms(xs_ref[...], g_ref[...]).astype(BF16)


def _norm_in(xp, xs, g, tm=256):
    tp, ts = xp.shape[0], xs.shape[0]
    npt, nst = tp // tm, ts // tm
    return pl.pallas_call(
        functools.partial(_norm_in_kernel, npt),
        grid=(npt + nst,),
        in_specs=[
            pl.BlockSpec((tm, D_MODEL), lambda i: (jnp.minimum(i, npt - 1), 0)),
            pl.BlockSpec((tm, D_MODEL), lambda i: (jnp.maximum(i - npt, 0), 0)),
            pl.BlockSpec((1, D_MODEL), lambda i: (0, 0)),
        ],
        out_specs=pl.BlockSpec((tm, D_MODEL), lambda i: (i, 0)),
        out_shape=jax.ShapeDtypeStruct((tp + ts, D_MODEL), BF16),
        compiler_params=_cparams(("arbitrary",)),
        name="norm_in",
    )(xp, xs, g)


def _mm(a, w):
    return lax.dot_general(a, w, (((1,), (0,)), ((), ())), preferred_element_type=F32)


def _panel_matmul_kernel(a_ref, w_ref, o_ref):
    o_ref[...] = _mm(a_ref[...], w_ref[...])


def _in_proj(xn, w_in, layer, tm=768, tn=1024):
    t = xn.shape[0]
    return pl.pallas_call(
        _panel_matmul_kernel,
        grid=(IN_COLS // tn, t // tm),
        in_specs=[
            pl.BlockSpec((tm, D_MODEL), lambda j, i: (i, 0)),
            pl.BlockSpec((None, D_MODEL, tn), lambda j, i: (layer, 0, j)),
        ],
        out_specs=pl.BlockSpec((tm, tn), lambda j, i: (i, j)),
        out_shape=jax.ShapeDtypeStruct((t, IN_COLS), F32),
        compiler_params=_cparams(("arbitrary", "arbitrary")),
        name="in_proj",
    )(xn, w_in)


def _window_rows(ext, n_seq, halo, rows):
    if n_seq == 1:
        return ext[halo:, :]
    c = ext.shape[-1]
    return ext.reshape(n_seq, halo + rows, c)[:, halo:, :].reshape(n_seq * rows, c)


def _stack_history(hist, cur, n_seq, halo, rows):
    if n_seq == 1:
        return jnp.concatenate([hist, cur], axis=0)
    c = cur.shape[-1]
    ext = jnp.concatenate([hist.reshape(n_seq, halo, c), cur.reshape(n_seq, rows, c)], axis=1)
    return ext.reshape(n_seq * (halo + rows), c)


def _branch_math(n_seq, rows, pos, a_ref, u_ref, v_ref, ci_ref, cb_ref, cc_ref, pool_hist, conv_hist,
                 wg_ref, ps_ref, lng_ref, lnb_ref, m_ref, bias_ref, cw_ref, oa_ref, ob_ref, oc_ref):
    for g, w in enumerate(POOL_WINDOWS):
        cols = slice(g * POOL_GROUP_DIM, (g + 1) * POOL_GROUP_DIM)
        a_g = a_ref[:, cols]
        s = _stack_history(pool_hist[:, cols], a_g, n_seq, POOL_HALO, rows)
        k = 1
        while k < w:
            s = s + pltpu.roll(s, k, 0)
            k *= 2
        s = _window_rows(s, n_seq, POOL_HALO, rows)
        cnt = jnp.minimum(w, pos + 1).astype(F32)
        pooled = (s / cnt - a_g).astype(BF16)
        out = jnp.dot(pooled, wg_ref[g], preferred_element_type=F32) * ps_ref[:, cols]
        oa_ref[:, cols] = out.astype(BF16)

    vg = _gelu_tanh(v_ref[...])
    mu = jnp.mean(vg, axis=-1, keepdims=True)
    xc = vg - mu
    v = xc * lax.rsqrt(jnp.mean(xc * xc, axis=-1, keepdims=True) + EPS) * lng_ref[...] + lnb_ref[...]
    vb = v.astype(BF16)
    for h in range(SG_HEADS):
        cols = slice(h * SG_HEAD_DIM, (h + 1) * SG_HEAD_DIM)
        sp = jnp.dot(m_ref[h], vb[:, cols], preferred_element_type=F32) + bias_ref[:, cols]
        ob_ref[:, cols] = (_gelu_tanh(u_ref[:, cols]) * sp).astype(BF16)

    z = cc_ref[...] * ci_ref[...]
    e = _stack_history(conv_hist, z, n_seq, CONV_HALO, rows)
    y = cw_ref[0:1, :] * pltpu.roll(e, 2, 0)
    y = y + cw_ref[1:2, :] * pltpu.roll(e, 1, 0)
    y = y + cw_ref[2:3, :] * e
    oc_ref[...] = (cb_ref[...] * _window_rows(y, n_seq, CONV_HALO, rows)).astype(BF16)
    return v, z


def _branch_kernel(tiles_per_seq, n_prompt_tiles, seq_per_tile, sample_len,
                   a_ref, u_ref, v_ref, ci_ref, cb_ref, cc_ref, ph_ref, ch_ref, wg_ref, ps_ref, lng_ref, lnb_ref,
                   m_ref, bias_ref, cw_ref, oa_ref, ob_ref, oc_ref, zl_ref, al_ref, vo_ref, zo_ref, pool_hist, conv_hist):
    i = pl.program_id(0)
    shared = (wg_ref, ps_ref, lng_ref, lnb_ref, m_ref, bias_ref, cw_ref, oa_ref, ob_ref, oc_ref)

    @pl.when(i < n_prompt_tiles)
    def _():
        t = i % tiles_per_seq

        @pl.when(t == 0)
        def _():
            pool_hist[...] = jnp.zeros_like(pool_hist)
            conv_hist[...] = jnp.zeros_like(conv_hist)

        pos = t * ROW_TILE + lax.broadcasted_iota(jnp.int32, (ROW_TILE, 1), 0)
        _, z = _branch_math(1, ROW_TILE, pos, a_ref, u_ref, v_ref, ci_ref, cb_ref, cc_ref, pool_hist[...],
                            conv_hist[...], *shared)
        pool_hist[...] = a_ref[ROW_TILE - POOL_HALO:, :]
        conv_hist[...] = z[ROW_TILE - CONV_HALO:, :]
        zl_ref[...] = z[ROW_TILE - CONV_HALO:, :]
        al_ref[...] = a_ref[ROW_TILE - POOL_HALO:, :]

    @pl.when(i >= n_prompt_tiles)
    def _():
        r = lax.broadcasted_iota(jnp.int32, (seq_per_tile * sample_len, 1), 0)
        pos = PAST_LEN + (r % sample_len)
        v, z = _branch_math(seq_per_tile, sample_len, pos, a_ref, u_ref, v_ref, ci_ref, cb_ref, cc_ref,
                            ph_ref[...], ch_ref[...], *shared)
        vo_ref[...] = v
        zo_ref[...] = z


def _branches(proj, layer, n_prompt_seq, prompt_len, n_sample_seq, sample_len, pool_state, conv_state,
              wg_bf, pool_scale, ln_g, ln_b, gate_mats, gate_bias, conv_w):
    t_total = proj.shape[0]
    tiles_per_seq = prompt_len // ROW_TILE
    npt = n_prompt_seq * tiles_per_seq
    seq_per_tile = ROW_TILE // sample_len
    nst = n_sample_seq // seq_per_tile
    sample_rows = n_sample_seq * sample_len
    act = jax.ShapeDtypeStruct((t_total, MIX_WIDTH), BF16)
    sample_f32 = jax.ShapeDtypeStruct((sample_rows, MIX_WIDTH), F32)

    def sample_tile(i):
        return jnp.maximum(i - npt, 0)

    def kind(i):
        return jnp.where(i >= npt, 1, 0)

    slab_specs = [pl.BlockSpec((ROW_TILE, MIX_WIDTH), functools.partial(lambda c, i: (i, c), c)) for c in range(6)]
    vec_spec = pl.BlockSpec((None, 1, MIX_WIDTH), lambda i: (layer, 0, 0))
    in_specs = slab_specs + [
        pl.BlockSpec((None, seq_per_tile * POOL_HALO, MIX_WIDTH), lambda i: (layer, sample_tile(i), 0)),
        pl.BlockSpec((None, seq_per_tile * CONV_HALO, MIX_WIDTH), lambda i: (layer, sample_tile(i), 0)),
        pl.BlockSpec((None, POOL_GROUPS, POOL_GROUP_DIM, POOL_GROUP_DIM), lambda i: (layer, 0, 0, 0)),
        vec_spec, vec_spec, vec_spec,
        pl.BlockSpec((None, None, SG_HEADS, CHUNK, CHUNK), lambda i: (layer, kind(i), 0, 0, 0)),
        pl.BlockSpec((None, None, CHUNK, MIX_WIDTH), lambda i: (layer, kind(i), 0, 0)),
        pl.BlockSpec((None, 3, MIX_WIDTH), lambda i: (layer, 0, 0)),
    ]
    row_spec = pl.BlockSpec((ROW_TILE, MIX_WIDTH), lambda i: (i, 0))
    out_specs = [
        row_spec, row_spec, row_spec,
        pl.BlockSpec((None, CONV_HALO, MIX_WIDTH), lambda i: (jnp.minimum(i // tiles_per_seq, n_prompt_seq - 1), 0, 0)),
        pl.BlockSpec((None, POOL_HALO, MIX_WIDTH), lambda i: (jnp.minimum(i // tiles_per_seq, n_prompt_seq - 1), 0, 0)),
        pl.BlockSpec((ROW_TILE, MIX_WIDTH), lambda i: (sample_tile(i), 0)),
        pl.BlockSpec((ROW_TILE, MIX_WIDTH), lambda i: (sample_tile(i), 0)),
    ]
    return pl.pallas_call(
        functools.partial(_branch_kernel, tiles_per_seq, npt, seq_per_tile, sample_len),
        grid=(npt + nst,),
        in_specs=in_specs,
        out_specs=out_specs,
        out_shape=[act, act, act, jax.ShapeDtypeStruct((n_prompt_seq, CONV_HALO, MIX_WIDTH), F32),
                   jax.ShapeDtypeStruct((n_prompt_seq, POOL_HALO, MIX_WIDTH), F32), sample_f32, sample_f32],
        scratch_shapes=[pltpu.VMEM((POOL_HALO, MIX_WIDTH), F32), pltpu.VMEM((CONV_HALO, MIX_WIDTH), F32)],
        compiler_params=_cparams(("arbitrary",)),
        name="branches",
    )(proj, proj, proj, proj, proj, proj, pool_state, conv_state, wg_bf, pool_scale, ln_g, ln_b,
      gate_mats, gate_bias, conv_w)


def _merge_kernel(oa_ref, ob_ref, oc_ref, wa_ref, wb_ref, wc_ref, ga_ref, gb_ref, gc_ref, o_ref):
    m = jax.nn.sigmoid(ga_ref[...]) * _mm(oa_ref[...], wa_ref[...])
    m = m + jax.nn.sigmoid(gb_ref[...]) * _mm(ob_ref[...], wb_ref[...])
    m = m + jax.nn.sigmoid(gc_ref[...]) * _mm(oc_ref[...], wc_ref[...])
    o_ref[...] = m.astype(BF16)


def _merge(oa, ob, oc, w_branch, proj, layer, tm=768, tn=512):
    t = oa.shape[0]
    act_spec = pl.BlockSpec((tm, MIX_WIDTH), lambda j, i: (i, 0))

    def wspec(b):
        return pl.BlockSpec((None, None, MIX_WIDTH, tn), lambda j, i: (layer, b, 0, j))

    def gspec(b):
        off = (GATE_COL0 + b * D_MODEL) // tn
        return pl.BlockSpec((tm, tn), lambda j, i: (i, off + j))

    return pl.pallas_call(
        _merge_kernel,
        grid=(D_MODEL // tn, t // tm),
        in_specs=[act_spec, act_spec, act_spec, wspec(0), wspec(1), wspec(2), gspec(0), gspec(1), gspec(2)],
        out_specs=pl.BlockSpec((tm, tn), lambda j, i: (i, j)),
        out_shape=jax.ShapeDtypeStruct((t, D_MODEL), BF16),
        compiler_params=_cparams(("arbitrary", "arbitrary")),
        name="merge",
    )(oa, ob, oc, w_branch, w_branch, w_branch, proj, proj, proj)


def _out_kernel2(n_prompt_tiles, a_ref, w_ref, xp_ref, xs_ref, o_ref):
    i = pl.program_id(1)
    h = _mm(a_ref[...], w_ref[...])

    @pl.when(i < n_prompt_tiles)
    def _():
        o_ref[...] = xp_ref[...] + h

    @pl.when(i >= n_prompt_tiles)
    def _():
        o_ref[...] = xs_ref[...] + h


def _out_kernel1(a_ref, w_ref, x_ref, o_ref):
    o_ref[...] = x_ref[...] + _mm(a_ref[...], w_ref[...])


def _out_proj(merged, w_out, layer, x_parts, tm=1024, tn=512):
    t = merged.shape[0]
    common = dict(
        grid=(D_MODEL // tn, t // tm),
        out_specs=pl.BlockSpec((tm, tn), lambda j, i: (i, j)),
        out_shape=jax.ShapeDtypeStruct((t, D_MODEL), F32),
        compiler_params=_cparams(("arbitrary", "arbitrary")),
        name="out_proj",
    )
    a_spec = pl.BlockSpec((tm, D_MODEL), lambda j, i: (i, 0))
    w_spec = pl.BlockSpec((None, D_MODEL, tn), lambda j, i: (layer, 0, j))
    if len(x_parts) == 1:
        return pl.pallas_call(
            _out_kernel1,
            in_specs=[a_spec, w_spec, pl.BlockSpec((tm, tn), lambda j, i: (i, j))],
            **common,
        )(merged, w_out, x_parts[0])
    xp, xs = x_parts
    npt = xp.shape[0] // tm
    return pl.pallas_call(
        functools.partial(_out_kernel2, npt),
        in_specs=[
            a_spec, w_spec,
            pl.BlockSpec((tm, tn), lambda j, i: (jnp.minimum(i, npt - 1), j)),
            pl.BlockSpec((tm, tn), lambda j, i: (jnp.maximum(i - npt, 0), j)),
        ],
        **common,
    )(merged, w_out, xp, xs)


def _router_kernel(x_ref, g_ref, wr_ref, br_ref, xn_ref, eid_ref, gate_ref):
    xb = _rms(x_ref[...], g_ref[...]).astype(BF16)
    lo = pltpu.bitcast(xb[:, :PACKED_WIDTH].astype(F32), jnp.uint32) >> 16
    hi = pltpu.bitcast(xb[:, PACKED_WIDTH:].astype(F32), jnp.uint32) & jnp.uint32(0xFFFF0000)
    xn_ref[...] = hi | lo
    logits = jnp.dot(xb, wr_ref[...], preferred_element_type=F32) + br_ref[...]
    tm = logits.shape[0]
    lane = lax.broadcasted_iota(jnp.int32, (tm, LANES), 1)
    neg = jnp.float32(-jnp.inf)
    far = jnp.int32(LANES)

    is_g = lane < N_EXPERT_GROUPS
    glog = jnp.where(is_g, logits, neg)
    gmax = jnp.max(glog, axis=1, keepdims=True)
    gsel = jnp.min(jnp.where(glog == gmax, lane, far), axis=1, keepdims=True)
    gsum = jnp.sum(jnp.where(is_g, jnp.exp(glog - gmax), 0.0), axis=1, keepdims=True)
    gp = 1.0 / gsum

    lo = N_EXPERT_GROUPS + gsel * EXPERTS_PER_GROUP
    in_grp = jnp.logical_and(lane >= lo, lane < lo + EXPERTS_PER_GROUP)
    el = jnp.where(in_grp, logits, neg)
    m1 = jnp.max(el, axis=1, keepdims=True)
    i1 = jnp.min(jnp.where(el == m1, lane, far), axis=1, keepdims=True)
    el2 = jnp.where(lane == i1, neg, el)
    m2 = jnp.max(el2, axis=1, keepdims=True)
    i2 = jnp.min(jnp.where(el2 == m2, lane, far), axis=1, keepdims=True)
    e2 = jnp.exp(m2 - m1)
    den = 1.0 + e2
    g1 = gp * (1.0 / den)
    g2 = gp * (e2 / den)

    eid_ref[...] = jnp.where(lane == 0, i1 - N_EXPERT_GROUPS, jnp.where(lane == 1, i2 - N_EXPERT_GROUPS, 0))
    gate_ref[...] = jnp.where(lane == 0, g1, jnp.where(lane == 1, g2, 0.0))


def _router(x1, g, wr, br, tm=256):
    t = x1.shape[0]
    return pl.pallas_call(
        _router_kernel,
        grid=(t // tm,),
        in_specs=[
            pl.BlockSpec((tm, D_MODEL), lambda i: (i, 0)),
            pl.BlockSpec((1, D_MODEL), lambda i: (0, 0)),
            pl.BlockSpec((D_MODEL, LANES), lambda i: (0, 0)),
            pl.BlockSpec((1, LANES), lambda i: (0, 0)),
        ],
        out_specs=[
            pl.BlockSpec((tm, PACKED_WIDTH), lambda i: (i, 0)),
            pl.BlockSpec((tm, LANES), lambda i: (i, 0)),
            pl.BlockSpec((tm, LANES), lambda i: (i, 0)),
        ],
        out_shape=[
            jax.ShapeDtypeStruct((t, PACKED_WIDTH), jnp.uint32),
            jax.ShapeDtypeStruct((t, LANES), jnp.int32),
            jax.ShapeDtypeStruct((t, LANES), F32),
        ],
        compiler_params=_cparams(("arbitrary",)),
        name="router",
    )(x1, g, wr, br)


def _gather_rows(idx_ref, idx_base, idx_stride, n_rows, src_hbm, dst, sem):
    def body(i, c):
        for q in range(DMA_QUEUES):
            r = i * DMA_QUEUES + q
            row = idx_ref[idx_base + r * idx_stride]
            pltpu.make_async_copy(src_hbm.at[pl.ds(row, 1)], dst.at[pl.ds(r, 1)], sem).start(priority=q)
        return c

    lax.fori_loop(0, n_rows // DMA_QUEUES, body, 0, unroll=4)


def _wait_rows(n_rows, src_hbm, dst, sem):
    pltpu.make_async_copy(src_hbm.at[pl.ds(0, n_rows)], dst, sem).wait()


def _dispatch_kernel(tok_ref, nb_ref, x_hbm, o_ref, gbuf, sems):
    b = pl.program_id(0)
    nb = nb_ref[0]

    def issue(blk):
        slot = blk % 2
        _gather_rows(tok_ref, blk * EXPERT_ROWS, 1, EXPERT_ROWS, x_hbm, gbuf.at[slot], sems.at[slot])

    @pl.when(jnp.logical_and(b == 0, nb > 0))
    def _():
        issue(b)

    @pl.when(b + 1 < nb)
    def _():
        issue(b + 1)

    @pl.when(b < nb)
    def _():
        slot = b % 2
        _wait_rows(EXPERT_ROWS, x_hbm, gbuf.at[slot], sems.at[slot])
        words = gbuf[slot]
        o_ref[:, :PACKED_WIDTH] = pltpu.bitcast(words << 16, F32).astype(BF16)
        o_ref[:, PACKED_WIDTH:] = pltpu.bitcast(words & jnp.uint32(0xFFFF0000), F32).astype(BF16)

    @pl.when(b >= nb)
    def _():
        o_ref[...] = jnp.zeros_like(o_ref)


def _dispatch(xn, tok_sorted, n_blk):
    cap = tok_sorted.shape[0]
    return pl.pallas_call(
        _dispatch_kernel,
        grid_spec=pltpu.PrefetchScalarGridSpec(
            num_scalar_prefetch=2,
            grid=(cap // EXPERT_ROWS,),
            in_specs=[pl.BlockSpec(memory_space=pl.ANY)],
            out_specs=pl.BlockSpec((EXPERT_ROWS, D_MODEL), lambda b, tok, nbk: (b, 0)),
            scratch_shapes=[pltpu.VMEM((2, EXPERT_ROWS, PACKED_WIDTH), jnp.uint32), pltpu.SemaphoreType.DMA((2,))],
        ),
        out_shape=jax.ShapeDtypeStruct((cap, D_MODEL), BF16),
        compiler_params=_cparams(("arbitrary",)),
        name="dispatch",
    )(tok_sorted, n_blk, xn)


def _stream_panels(first_ref, pidx_ref, meta_ref, pass_idx, n_pass, b, n_slots, panel_copies):
    n_distinct = meta_ref[1]
    n_panels = n_pass * n_distinct
    ahead = n_slots - 1

    def start(q):
        for c in panel_copies(q // n_distinct, q % n_distinct, q % n_slots):
            c.start()

    @pl.when(jnp.logical_and(pass_idx == 0, b == 0))
    def _():
        for q in range(ahead):
            @pl.when(q < n_panels)
            def _():
                start(q)

    p = pass_idx * n_distinct + pidx_ref[b]
    slot = p % n_slots

    @pl.when(first_ref[b] == 1)
    def _():
        for c in panel_copies(pass_idx, pidx_ref[b], slot):
            c.wait()

        @pl.when(p + ahead < n_panels)
        def _():
            start(p + ahead)

    return slot


UP_SLOTS = 3
DOWN_SLOTS = 2


def _expert_up_kernel(layer, tf, first_ref, pidx_ref, pe_ref, meta_ref, x_ref, w_hbm, o_ref, stage, sems):
    f = pl.program_id(0)
    b = pl.program_id(1)

    def panel_copies(fq, k, slot):
        e = pe_ref[k]
        col = pl.multiple_of(fq * tf, tf)
        return [pltpu.make_async_copy(w_hbm.at[layer, e, :, pl.ds(half * D_EXPERT + col, tf)],
                                      stage.at[slot, half], sems.at[slot]) for half in range(2)]

    slot = _stream_panels(first_ref, pidx_ref, meta_ref, f, pl.num_programs(0), b, UP_SLOTS, panel_copies)

    @pl.when(b < meta_ref[0])
    def _():
        x = x_ref[...]
        h1 = _mm(x, stage[slot, 0])
        h3 = _mm(x, stage[slot, 1])
        o_ref[...] = (h1 * jax.nn.sigmoid(h1) * h3).astype(BF16)

    @pl.when(b >= meta_ref[0])
    def _():
        o_ref[...] = jnp.zeros_like(o_ref)


def _expert_up(xs, w_ei, layer, sched, tf=512):
    cap = xs.shape[0]
    nb = cap // EXPERT_ROWS
    nf = D_EXPERT // tf
    return pl.pallas_call(
        functools.partial(_expert_up_kernel, layer, tf),
        grid_spec=pltpu.PrefetchScalarGridSpec(
            num_scalar_prefetch=4,
            grid=(nf, nb),
            in_specs=[
                pl.BlockSpec((EXPERT_ROWS, D_MODEL), lambda f, b, fi, pi, pe, meta: (jnp.minimum(b, meta[0] - 1), 0)),
                pl.BlockSpec(memory_space=pl.ANY),
            ],
            out_specs=pl.BlockSpec((EXPERT_ROWS, tf), lambda f, b, fi, pi, pe, meta: (b, f)),
            scratch_shapes=[pltpu.VMEM((UP_SLOTS, 2, D_MODEL, tf), F32), pltpu.SemaphoreType.DMA((UP_SLOTS,))],
        ),
        out_shape=jax.ShapeDtypeStruct((cap, D_EXPERT), BF16),
        compiler_params=_cparams(("arbitrary", "arbitrary")),
        name="expert_up",
    )(*sched, xs, w_ei)


def _expert_down_kernel(layer, first_ref, pidx_ref, pe_ref, meta_ref, h_ref, w_hbm, o_ref, stage, sems):
    b = pl.program_id(0)

    def panel_copies(fq, k, slot):
        del fq
        return [pltpu.make_async_copy(w_hbm.at[layer, pe_ref[k]], stage.at[slot], sems.at[slot])]

    slot = _stream_panels(first_ref, pidx_ref, meta_ref, 0, 1, b, DOWN_SLOTS, panel_copies)

    @pl.when(b < meta_ref[0])
    def _():
        o_ref[...] = _mm(h_ref[...], stage[slot])

    @pl.when(b >= meta_ref[0])
    def _():
        o_ref[...] = jnp.zeros_like(o_ref)


def _expert_down(hs, w_eo, layer, sched):
    cap = hs.shape[0]
    nb = cap // EXPERT_ROWS
    return pl.pallas_call(
        functools.partial(_expert_down_kernel, layer),
        grid_spec=pltpu.PrefetchScalarGridSpec(
            num_scalar_prefetch=4,
            grid=(nb,),
            in_specs=[
                pl.BlockSpec((EXPERT_ROWS, D_EXPERT), lambda b, fi, pi, pe, meta: (jnp.minimum(b, meta[0] - 1), 0)),
                pl.BlockSpec(memory_space=pl.ANY),
            ],
            out_specs=pl.BlockSpec((EXPERT_ROWS, D_MODEL), lambda b, fi, pi, pe, meta: (b, 0)),
            scratch_shapes=[pltpu.VMEM((DOWN_SLOTS, D_EXPERT, D_MODEL), F32), pltpu.SemaphoreType.DMA((DOWN_SLOTS,))],
        ),
        out_shape=jax.ShapeDtypeStruct((cap, D_MODEL), F32),
        compiler_params=_cparams(("arbitrary",)),
        name="expert_down",
    )(*sched, hs, w_eo)


def _combine_kernel(tm, split, dest_ref, x_ref, gate_ref, g_ref, ys_hbm, *rest):
    if split is None:
        x2_ref, xn_ref, ybuf, sem = rest
    else:
        yp_ref, ysm_ref, ybuf, sem = rest
    i = pl.program_id(0)

    def issue(tile):
        slot = tile % 2
        for k in range(TOP_K):
            _gather_rows(dest_ref, tile * tm * TOP_K + k, TOP_K, tm, ys_hbm, ybuf.at[slot, k], sem.at[slot])

    @pl.when(i == 0)
    def _():
        issue(i)

    @pl.when(i + 1 < pl.num_programs(0))
    def _():
        issue(i + 1)

    slot = i % 2
    for k in range(TOP_K):
        _wait_rows(tm, ys_hbm, ybuf.at[slot, k], sem.at[slot])
    gates = gate_ref[...]
    x2 = x_ref[...] + (ybuf[slot, 0] * gates[:, 0:1] + ybuf[slot, 1] * gates[:, 1:2])
    if split is None:
        x2_ref[...] = x2
        xn_ref[...] = _rms(x2, g_ref[...]).astype(xn_ref.dtype)
    else:
        y = _rms(x2, g_ref[...])

        @pl.when(i < split)
        def _():
            yp_ref[...] = y

        @pl.when(i >= split)
        def _():
            ysm_ref[...] = y


def _combine(x1, gates, dest_flat, ys, g, final_split=None, tm=128):
    t = x1.shape[0]
    in_specs = [
        pl.BlockSpec((tm, D_MODEL), lambda i, d: (i, 0)),
        pl.BlockSpec((tm, LANES), lambda i, d: (i, 0)),
        pl.BlockSpec((1, D_MODEL), lambda i, d: (0, 0)),
        pl.BlockSpec(memory_space=pl.ANY),
    ]
    if final_split is None:
        split = None
        out_specs = [pl.BlockSpec((tm, D_MODEL), lambda i, d: (i, 0)),
                     pl.BlockSpec((tm, D_MODEL), lambda i, d: (i, 0))]
        out_shape = [jax.ShapeDtypeStruct((t, D_MODEL), F32), jax.ShapeDtypeStruct((t, D_MODEL), BF16)]
    else:
        split = final_split // tm
        out_specs = [pl.BlockSpec((tm, D_MODEL), lambda i, d: (jnp.minimum(i, split - 1), 0)),
                     pl.BlockSpec((tm, D_MODEL), lambda i, d: (jnp.maximum(i - split, 0), 0))]
        out_shape = [jax.ShapeDtypeStruct((final_split, D_MODEL), F32),
                     jax.ShapeDtypeStruct((t - final_split, D_MODEL), F32)]
    return pl.pallas_call(
        functools.partial(_combine_kernel, tm, split),
        grid_spec=pltpu.PrefetchScalarGridSpec(
            num_scalar_prefetch=1,
            grid=(t // tm,),
            in_specs=in_specs,
            out_specs=out_specs,
            scratch_shapes=[pltpu.VMEM((2, TOP_K, tm, D_MODEL), F32), pltpu.SemaphoreType.DMA((2,))],
        ),
        out_shape=out_shape,
        compiler_params=_cparams(("arbitrary",)),
        name="combine",
    )(dest_flat, x1, gates, g, ys)


def _routing_tables(eid, n_blocks):
    e_flat = eid.reshape(-1)
    onehot = (e_flat[:, None] == jnp.arange(N_EXPERTS, dtype=jnp.int32)[None, :]).astype(jnp.int32)
    csum = jnp.cumsum(onehot, axis=0)
    rank = jnp.sum(csum * onehot, axis=1) - 1
    counts = csum[-1]
    blocks = (counts + EXPERT_ROWS - 1) // EXPERT_ROWS
    blk_end = jnp.cumsum(blocks)
    blk_start = blk_end - blocks
    dest = (blk_start[e_flat] * EXPERT_ROWS + rank).astype(jnp.int32)
    n_used = blk_end[-1]
    tok_sorted = jnp.zeros((n_blocks * EXPERT_ROWS,), jnp.int32).at[dest].set(
        jnp.arange(dest.shape[0], dtype=jnp.int32) // TOP_K, unique_indices=True)

    has = blocks > 0
    order = jnp.cumsum(has.astype(jnp.int32)) - 1
    panel_expert = jnp.argsort(jnp.logical_not(has), stable=True)
    blk_ids = jnp.arange(n_blocks, dtype=jnp.int32)
    blk_e = jnp.minimum(jnp.searchsorted(blk_end, blk_ids, side="right"), N_EXPERTS - 1)
    used = blk_ids < n_used
    first = jnp.logical_and(used, blk_ids == blk_start[blk_e])
    pidx = jnp.where(used, order[blk_e], 0)
    meta = jnp.stack([n_used, jnp.sum(has.astype(jnp.int32))])
    i32 = lambda a: a.astype(jnp.int32)
    return dest, tok_sorted, (i32(first), i32(pidx), i32(panel_expert), i32(meta))


def _gate_tables(w_spatial, b_spatial, sample_len):
    depth = w_spatial.shape[0]
    tril = jnp.tril(jnp.ones((CHUNK, CHUNK), dtype=bool))
    full = jnp.where(tril[None, None], w_spatial, 0)
    reps = CHUNK // sample_len
    small = jnp.where(tril[None, None, :sample_len, :sample_len], w_spatial[:, :, :sample_len, :sample_len], 0)
    eye = jnp.eye(reps, dtype=w_spatial.dtype)
    blockdiag = jnp.einsum("ab,lhts->lhatbs", eye, small).reshape(depth, SG_HEADS, CHUNK, CHUNK)
    mats = jnp.stack([full, blockdiag], axis=1).astype(BF16)
    bias_full = jnp.transpose(b_spatial, (0, 2, 1))
    bias_small = jnp.tile(bias_full[:, :sample_len, :], (1, reps, 1))
    bias = jnp.stack([bias_full, bias_small], axis=1)
    bias = jnp.repeat(bias, SG_HEAD_DIM, axis=-1)
    return mats, bias


def kernel(x_prompt, x_sample, state_pool, state_conv, norm_mix, w_in, w_pool_group, pool_scale, sg_norm_g, sg_norm_b, w_spatial, b_spatial, conv_w, w_branch, w_out, norm_ffn, w_router_group, b_router_group, w_router_expert, b_router_expert, w_expert_in, w_expert_out, norm_final):
    depth = w_in.shape[0]
    bp, lp, _ = x_prompt.shape
    bs, ls, _ = x_sample.shape
    tp, ts = bp * lp, bs * ls
    t = tp + ts
    assert lp % ROW_TILE == 0 and ROW_TILE % ls == 0 and bs % (ROW_TILE // ls) == 0 and ls >= CONV_STATE

    xp = x_prompt.reshape(tp, D_MODEL)
    xs = x_sample.reshape(ts, D_MODEL)

    gate_mats, gate_bias = _gate_tables(w_spatial, b_spatial, ls)
    wg_bf = w_pool_group.astype(BF16)
    vec = lambda a: a.reshape(depth, 1, -1)
    wr = jnp.concatenate([w_router_group, w_router_expert,
                          jnp.zeros((depth, D_MODEL, LANES - N_EXPERT_GROUPS - N_EXPERTS), F32)], axis=-1).astype(BF16)
    br = jnp.concatenate([b_router_group, b_router_expert,
                          jnp.zeros((depth, LANES - N_EXPERT_GROUPS - N_EXPERTS), F32)], axis=-1).reshape(depth, 1, LANES)
    pool_hist = jnp.pad(state_pool, ((0, 0), (0, 0), (POOL_HALO - POOL_STATE, 0), (0, 0))).reshape(
        depth, bs * POOL_HALO, MIX_WIDTH)
    conv_hist = jnp.pad(state_conv, ((0, 0), (0, 0), (CONV_HALO - CONV_STATE, 0), (0, 0))).reshape(
        depth, bs * CONV_HALO, MIX_WIDTH)

    n_assign = t * TOP_K
    n_blocks = n_assign // EXPERT_ROWS + N_EXPERTS

    pool_p, pool_s, conv_p, conv_s, v_out = [], [], [], [], []
    x_parts = (xp, xs)
    xn = _norm_in(xp, xs, norm_mix[0].reshape(1, D_MODEL))
    y_p = y_s = None
    for l in range(depth):
        proj = _in_proj(xn, w_in, l)
        oa, ob, oc, z_tail, a_tail, v_s, z_s = _branches(
            proj, l, bp, lp, bs, ls, pool_hist, conv_hist, wg_bf, vec(pool_scale), vec(sg_norm_g),
            vec(sg_norm_b), gate_mats, gate_bias, conv_w)
        merged = _merge(oa, ob, oc, w_branch, proj, l)
        x1 = _out_proj(merged, w_out, l, x_parts)

        xn2, eid, gates = _router(x1, norm_ffn[l].reshape(1, D_MODEL), wr[l], br[l])
        dest, tok_sorted, sched = _routing_tables(eid[:, :TOP_K], n_blocks)
        xs_sorted = _dispatch(xn2, tok_sorted, sched[3])
        hs = _expert_up(xs_sorted, w_expert_in, l, sched)
        ys = _expert_down(hs, w_expert_out, l, sched)
        if l + 1 < depth:
            x2, xn = _combine(x1, gates, dest, ys, norm_mix[l + 1].reshape(1, D_MODEL))
            x_parts = (x2,)
        else:
            y_p, y_s = _combine(x1, gates, dest, ys, norm_final.reshape(1, D_MODEL), final_split=tp)

        a_s = proj[tp:, :MIX_WIDTH].reshape(bs, ls, MIX_WIDTH)
        pool_p.append(a_tail[:, POOL_HALO - POOL_STATE:])
        pool_s.append(jnp.concatenate([state_pool[l], a_s], axis=1)[:, -POOL_STATE:])
        conv_p.append(z_tail[:, CONV_HALO - CONV_STATE:])
        conv_s.append(z_s.reshape(bs, ls, MIX_WIDTH)[:, ls - CONV_STATE:])
        v_out.append(v_s.reshape(bs, ls, MIX_WIDTH))

    return (y_p.reshape(bp, lp, D_MODEL), y_s.reshape(bs, ls, D_MODEL), jnp.stack(pool_p), jnp.stack(pool_s),
            jnp.stack(conv_p), jnp.stack(conv_s), jnp.stack(v_out))
```

```python
import functools

import jax
import jax.numpy as jnp
from jax import lax
from jax.experimental import pallas as pl
from jax.experimental.pallas import tpu as pltpu

F32 = jnp.float32
BF16 = jnp.bfloat16

D_MODEL = 4096
MIX_WIDTH = D_MODEL // 2
POOL_GROUPS = 4
POOL_WINDOWS = (2, 4, 8, 16)
POOL_GROUP_DIM = MIX_WIDTH // POOL_GROUPS
POOL_STATE = 15
SG_HEADS = 8
SG_HEAD_DIM = MIX_WIDTH // SG_HEADS
CHUNK = 128
CONV_STATE = 2
IN_COLS = MIX_WIDTH * 6 + 3 * D_MODEL
GATE_COL0 = MIX_WIDTH * 6
N_EXPERT_GROUPS = 4
EXPERTS_PER_GROUP = 8
N_EXPERTS = N_EXPERT_GROUPS * EXPERTS_PER_GROUP
TOP_K = 2
D_EXPERT = D_MODEL // 4
PAST_LEN = 16384
EPS = 1e-6

LANES = 128
SUBLANES = 8
POOL_HALO = 16
CONV_HALO = 8
ROW_TILE = 128
EXPERT_ROWS = 256
PACKED_WIDTH = D_MODEL // 2
VMEM_LIMIT = 56 * 1024 * 1024


def _cparams(sem):
    return pltpu.CompilerParams(dimension_semantics=sem, vmem_limit_bytes=VMEM_LIMIT)


def _gelu_tanh(x):
    c = 0.7978845608028654
    return 0.5 * x * (1.0 + jnp.tanh(c * (x + 0.044715 * (x * x * x))))


def _rms(x, g):
    return x * lax.rsqrt(jnp.mean(x * x, axis=-1, keepdims=True) + EPS) * g


def _norm_in_kernel(n_prompt_tiles, xp_ref, xs_ref, g_ref, o_ref):
    i = pl.program_id(0)

    @pl.when(i < n_prompt_tiles)
    def _():
        o_ref[...] = _rms(xp_ref[...], g_ref[...]).astype(BF16)

    @pl.when(i >= n_prompt_tiles)
    def _():
        o_ref[...] = _rms(xs_ref[...], g_ref[...]).astype(BF16)


def _norm_in(xp, xs, g, tm=256):
    tp, ts = xp.shape[0], xs.shape[0]
    npt, nst = tp // tm, ts // tm
    return pl.pallas_call(
        functools.partial(_norm_in_kernel, npt),
        grid=(npt + nst,),
        in_specs=[
            pl.BlockSpec((tm, D_MODEL), lambda i: (jnp.minimum(i, npt - 1), 0)),
            pl.BlockSpec((tm, D_MODEL), lambda i: (jnp.maximum(i - npt, 0), 0)),
            pl.BlockSpec((1, D_MODEL), lambda i: (0, 0)),
        ],
        out_specs=pl.BlockSpec((tm, D_MODEL), lambda i: (i, 0)),
        out_shape=jax.ShapeDtypeStruct((tp + ts, D_MODEL), BF16),
        compiler_params=_cparams(("arbitrary",)),
        name="norm_in",
    )(xp, xs, g)


def _mm(a, w):
    return lax.dot_general(a, w, (((1,), (0,)), ((), ())), preferred_element_type=F32)


def _panel_matmul_kernel(a_ref, w_ref, o_ref):
    o_ref[...] = _mm(a_ref[...], w_ref[...])


def _in_proj(xn, w_in, layer, tm=768, tn=1024):
    t = xn.shape[0]
    return pl.pallas_call(
        _panel_matmul_kernel,
        grid=(IN_COLS // tn, t // tm),
        in_specs=[
            pl.BlockSpec((tm, D_MODEL), lambda j, i: (i, 0)),
            pl.BlockSpec((None, D_MODEL, tn), lambda j, i: (layer, 0, j)),
        ],
        out_specs=pl.BlockSpec((tm, tn), lambda j, i: (i, j)),
        out_shape=jax.ShapeDtypeStruct((t, IN_COLS), F32),
        compiler_params=_cparams(("arbitrary", "arbitrary")),
        name="in_proj",
    )(xn, w_in)


def _window_rows(ext, n_seq, halo, rows):
    if n_seq == 1:
        return ext[halo:, :]
    c = ext.shape[-1]
    return ext.reshape(n_seq, halo + rows, c)[:, halo:, :].reshape(n_seq * rows, c)


def _stack_history(hist, cur, n_seq, halo, rows):
    if n_seq == 1:
        return jnp.concatenate([hist, cur], axis=0)
    c = cur.shape[-1]
    ext = jnp.concatenate([hist.reshape(n_seq, halo, c), cur.reshape(n_seq, rows, c)], axis=1)
    return ext.reshape(n_seq * (halo + rows), c)


def _branch_math(n_seq, rows, pos, a_ref, u_ref, v_ref, ci_ref, cb_ref, cc_ref, pool_hist, conv_hist,
                 wg_ref, ps_ref, lng_ref, lnb_ref, m_ref, bias_ref, cw_ref, oa_ref, ob_ref, oc_ref):
    for g, w in enumerate(POOL_WINDOWS):
        cols = slice(g * POOL_GROUP_DIM, (g + 1) * POOL_GROUP_DIM)
        a_g = a_ref[:, cols]
        s = _stack_history(pool_hist[:, cols], a_g, n_seq, POOL_HALO, rows)
        k = 1
        while k < w:
            s = s + pltpu.roll(s, k, 0)
            k *= 2
        s = _window_rows(s, n_seq, POOL_HALO, rows)
        cnt = jnp.minimum(w, pos + 1).astype(F32)
        pooled = (s / cnt - a_g).astype(BF16)
        out = jnp.dot(pooled, wg_ref[g], preferred_element_type=F32) * ps_ref[:, cols]
        oa_ref[:, cols] = out.astype(BF16)

    vg = _gelu_tanh(v_ref[...])
    mu = jnp.mean(vg, axis=-1, keepdims=True)
    xc = vg - mu
    v = xc * lax.rsqrt(jnp.mean(xc * xc, axis=-1, keepdims=True) + EPS) * lng_ref[...] + lnb_ref[...]
    vb = v.astype(BF16)
    for h in range(SG_HEADS):
        cols = slice(h * SG_HEAD_DIM, (h + 1) * SG_HEAD_DIM)
        sp = jnp.dot(m_ref[h], vb[:, cols], preferred_element_type=F32) + bias_ref[:, cols]
        ob_ref[:, cols] = (_gelu_tanh(u_ref[:, cols]) * sp).astype(BF16)

    z = cc_ref[...] * ci_ref[...]
    e = _stack_history(conv_hist, z, n_seq, CONV_HALO, rows)
    y = cw_ref[0:1, :] * pltpu.roll(e, 2, 0)
    y = y + cw_ref[1:2, :] * pltpu.roll(e, 1, 0)
    y = y + cw_ref[2:3, :] * e
    oc_ref[...] = (cb_ref[...] * _window_rows(y, n_seq, CONV_HALO, rows)).astype(BF16)
    return v, z


def _branch_kernel(tiles_per_seq, n_prompt_tiles, seq_per_tile, sample_len,
                   a_ref, u_ref, v_ref, ci_ref, cb_ref, cc_ref, ph_ref, ch_ref, wg_ref, ps_ref, lng_ref, lnb_ref,
                   m_ref, bias_ref, cw_ref, oa_ref, ob_ref, oc_ref, zl_ref, al_ref, vo_ref, zo_ref, pool_hist, conv_hist):
    i = pl.program_id(0)
    shared = (wg_ref, ps_ref, lng_ref, lnb_ref, m_ref, bias_ref, cw_ref, oa_ref, ob_ref, oc_ref)

    @pl.when(i < n_prompt_tiles)
    def _():
        t = i % tiles_per_seq

        @pl.when(t == 0)
        def _():
            pool_hist[...] = jnp.zeros_like(pool_hist)
            conv_hist[...] = jnp.zeros_like(conv_hist)

        pos = t * ROW_TILE + lax.broadcasted_iota(jnp.int32, (ROW_TILE, 1), 0)
        _, z = _branch_math(1, ROW_TILE, pos, a_ref, u_ref, v_ref, ci_ref, cb_ref, cc_ref, pool_hist[...],
                            conv_hist[...], *shared)
        pool_hist[...] = a_ref[ROW_TILE - POOL_HALO:, :]
        conv_hist[...] = z[ROW_TILE - CONV_HALO:, :]
        zl_ref[...] = z[ROW_TILE - CONV_HALO:, :]
        al_ref[...] = a_ref[ROW_TILE - POOL_HALO:, :]

    @pl.when(i >= n_prompt_tiles)
    def _():
        r = lax.broadcasted_iota(jnp.int32, (seq_per_tile * sample_len, 1), 0)
        pos = PAST_LEN + (r % sample_len)
        v, z = _branch_math(seq_per_tile, sample_len, pos, a_ref, u_ref, v_ref, ci_ref, cb_ref, cc_ref,
                            ph_ref[...], ch_ref[...], *shared)
        vo_ref[...] = v
        zo_ref[...] = z


def _branches(proj, layer, n_prompt_seq, prompt_len, n_sample_seq, sample_len, pool_state, conv_state,
              wg_bf, pool_scale, ln_g, ln_b, gate_mats, gate_bias, conv_w):
    t_total = proj.shape[0]
    tiles_per_seq = prompt_len // ROW_TILE
    npt = n_prompt_seq * tiles_per_seq
    seq_per_tile = ROW_TILE // sample_len
    nst = n_sample_seq // seq_per_tile
    sample_rows = n_sample_seq * sample_len
    act = jax.ShapeDtypeStruct((t_total, MIX_WIDTH), BF16)
    sample_f32 = jax.ShapeDtypeStruct((sample_rows, MIX_WIDTH), F32)

    def sample_tile(i):
        return jnp.maximum(i - npt, 0)

    def kind(i):
        return jnp.where(i >= npt, 1, 0)

    slab_specs = [pl.BlockSpec((ROW_TILE, MIX_WIDTH), functools.partial(lambda c, i: (i, c), c)) for c in range(6)]
    vec_spec = pl.BlockSpec((None, 1, MIX_WIDTH), lambda i: (layer, 0, 0))
    in_specs = slab_specs + [
        pl.BlockSpec((None, seq_per_tile * POOL_HALO, MIX_WIDTH), lambda i: (layer, sample_tile(i), 0)),
        pl.BlockSpec((None, seq_per_tile * CONV_HALO, MIX_WIDTH), lambda i: (layer, sample_tile(i), 0)),
        pl.BlockSpec((None, POOL_GROUPS, POOL_GROUP_DIM, POOL_GROUP_DIM), lambda i: (layer, 0, 0, 0)),
        vec_spec, vec_spec, vec_spec,
        pl.BlockSpec((None, None, SG_HEADS, CHUNK, CHUNK), lambda i: (layer, kind(i), 0, 0, 0)),
        pl.BlockSpec((None, None, CHUNK, MIX_WIDTH), lambda i: (layer, kind(i), 0, 0)),
        pl.BlockSpec((None, 3, MIX_WIDTH), lambda i: (layer, 0, 0)),
    ]
    row_spec = pl.BlockSpec((ROW_TILE, MIX_WIDTH), lambda i: (i, 0))
    out_specs = [
        row_spec, row_spec, row_spec,
        pl.BlockSpec((None, CONV_HALO, MIX_WIDTH), lambda i: (jnp.minimum(i // tiles_per_seq, n_prompt_seq - 1), 0, 0)),
        pl.BlockSpec((None, POOL_HALO, MIX_WIDTH), lambda i: (jnp.minimum(i // tiles_per_seq, n_prompt_seq - 1), 0, 0)),
        pl.BlockSpec((ROW_TILE, MIX_WIDTH), lambda i: (sample_tile(i), 0)),
        pl.BlockSpec((ROW_TILE, MIX_WIDTH), lambda i: (sample_tile(i), 0)),
    ]
    return pl.pallas_call(
        functools.partial(_branch_kernel, tiles_per_seq, npt, seq_per_tile, sample_len),
        grid=(npt + nst,),
        in_specs=in_specs,
        out_specs=out_specs,
        out_shape=[act, act, act, jax.ShapeDtypeStruct((n_prompt_seq, CONV_HALO, MIX_WIDTH), F32),
                   jax.ShapeDtypeStruct((n_prompt_seq, POOL_HALO, MIX_WIDTH), F32), sample_f32, sample_f32],
        scratch_shapes=[pltpu.VMEM((POOL_HALO, MIX_WIDTH), F32), pltpu.VMEM((CONV_HALO, MIX_WIDTH), F32)],
        compiler_params=_cparams(("arbitrary",)),
        name="branches",
    )(proj, proj, proj, proj, proj, proj, pool_state, conv_state, wg_bf, pool_scale, ln_g, ln_b,
      gate_mats, gate_bias, conv_w)


def _merge_kernel(oa_ref, ob_ref, oc_ref, wa_ref, wb_ref, wc_ref, ga_ref, gb_ref, gc_ref, o_ref):
    m = jax.nn.sigmoid(ga_ref[...]) * _mm(oa_ref[...], wa_ref[...])
    m = m + jax.nn.sigmoid(gb_ref[...]) * _mm(ob_ref[...], wb_ref[...])
    m = m + jax.nn.sigmoid(gc_ref[...]) * _mm(oc_ref[...], wc_ref[...])
    o_ref[...] = m.astype(BF16)


def _merge(oa, ob, oc, w_branch, proj, layer, tm=768, tn=512):
    t = oa.shape[0]
    act_spec = pl.BlockSpec((tm, MIX_WIDTH), lambda j, i: (i, 0))

    def wspec(b):
        return pl.BlockSpec((None, None, MIX_WIDTH, tn), lambda j, i: (layer, b, 0, j))

    def gspec(b):
        off = (GATE_COL0 + b * D_MODEL) // tn
        return pl.BlockSpec((tm, tn), lambda j, i: (i, off + j))

    return pl.pallas_call(
        _merge_kernel,
        grid=(D_MODEL // tn, t // tm),
        in_specs=[act_spec, act_spec, act_spec, wspec(0), wspec(1), wspec(2), gspec(0), gspec(1), gspec(2)],
        out_specs=pl.BlockSpec((tm, tn), lambda j, i: (i, j)),
        out_shape=jax.ShapeDtypeStruct((t, D_MODEL), BF16),
        compiler_params=_cparams(("arbitrary", "arbitrary")),
        name="merge",
    )(oa, ob, oc, w_branch, w_branch, w_branch, proj, proj, proj)


def _out_kernel2(n_prompt_tiles, a_ref, w_ref, xp_ref, xs_ref, o_ref):
    i = pl.program_id(1)
    h = _mm(a_ref[...], w_ref[...])

    @pl.when(i < n_prompt_tiles)
    def _():
        o_ref[...] = xp_ref[...] + h

    @pl.when(i >= n_prompt_tiles)
    def _():
        o_ref[...] = xs_ref[...] + h


def _out_kernel1(a_ref, w_ref, x_ref, o_ref):
    o_ref[...] = x_ref[...] + _mm(a_ref[...], w_ref[...])


def _out_proj(merged, w_out, layer, x_parts, tm=1024, tn=512):
    t = merged.shape[0]
    common = dict(
        grid=(D_MODEL // tn, t // tm),
        out_specs=pl.BlockSpec((tm, tn), lambda j, i: (i, j)),
        out_shape=jax.ShapeDtypeStruct((t, D_MODEL), F32),
        compiler_params=_cparams(("arbitrary", "arbitrary")),
        name="out_proj",
    )
    a_spec = pl.BlockSpec((tm, D_MODEL), lambda j, i: (i, 0))
    w_spec = pl.BlockSpec((None, D_MODEL, tn), lambda j, i: (layer, 0, j))
    if len(x_parts) == 1:
        return pl.pallas_call(
            _out_kernel1,
            in_specs=[a_spec, w_spec, pl.BlockSpec((tm, tn), lambda j, i: (i, j))],
            **common,
        )(merged, w_out, x_parts[0])
    xp, xs = x_parts
    npt = xp.shape[0] // tm
    return pl.pallas_call(
        functools.partial(_out_kernel2, npt),
        in_specs=[
            a_spec, w_spec,
            pl.BlockSpec((tm, tn), lambda j, i: (jnp.minimum(i, npt - 1), j)),
            pl.BlockSpec((tm, tn), lambda j, i: (jnp.maximum(i - npt, 0), j)),
        ],
        **common,
    )(merged, w_out, xp, xs)


def _router_kernel(x_ref, g_ref, wr_ref, br_ref, xn_ref, eid_ref, gate_ref):
    xb = _rms(x_ref[...], g_ref[...]).astype(BF16)
    lo = pltpu.bitcast(xb[:, :PACKED_WIDTH].astype(F32), jnp.uint32) >> 16
    hi = pltpu.bitcast(xb[:, PACKED_WIDTH:].astype(F32), jnp.uint32) & jnp.uint32(0xFFFF0000)
    xn_ref[...] = hi | lo
    logits = jnp.dot(xb, wr_ref[...], preferred_element_type=F32) + br_ref[...]
    tm = logits.shape[0]
    lane = lax.broadcasted_iota(jnp.int32, (tm, LANES), 1)
    neg = jnp.float32(-jnp.inf)
    far = jnp.int32(LANES)

    is_g = lane < N_EXPERT_GROUPS
    glog = jnp.where(is_g, logits, neg)
    gmax = jnp.max(glog, axis=1, keepdims=True)
    gsel = jnp.min(jnp.where(glog == gmax, lane, far), axis=1, keepdims=True)
    gsum = jnp.sum(jnp.where(is_g, jnp.exp(glog - gmax), 0.0), axis=1, keepdims=True)
    gp = 1.0 / gsum

    lo = N_EXPERT_GROUPS + gsel * EXPERTS_PER_GROUP
    in_grp = jnp.logical_and(lane >= lo, lane < lo + EXPERTS_PER_GROUP)
    el = jnp.where(in_grp, logits, neg)
    m1 = jnp.max(el, axis=1, keepdims=True)
    i1 = jnp.min(jnp.where(el == m1, lane, far), axis=1, keepdims=True)
    el2 = jnp.where(lane == i1, neg, el)
    m2 = jnp.max(el2, axis=1, keepdims=True)
    i2 = jnp.min(jnp.where(el2 == m2, lane, far), axis=1, keepdims=True)
    e2 = jnp.exp(m2 - m1)
    den = 1.0 + e2
    g1 = gp * (1.0 / den)
    g2 = gp * (e2 / den)

    eid_ref[...] = jnp.where(lane == 0, i1 - N_EXPERT_GROUPS, jnp.where(lane == 1, i2 - N_EXPERT_GROUPS, 0))
    gate_ref[...] = jnp.where(lane == 0, g1, jnp.where(lane == 1, g2, 0.0))


def _router(x1, g, wr, br, tm=256):
    t = x1.shape[0]
    return pl.pallas_call(
        _router_kernel,
        grid=(t // tm,),
        in_specs=[
            pl.BlockSpec((tm, D_MODEL), lambda i: (i, 0)),
            pl.BlockSpec((1, D_MODEL), lambda i: (0, 0)),
            pl.BlockSpec((D_MODEL, LANES), lambda i: (0, 0)),
            pl.BlockSpec((1, LANES), lambda i: (0, 0)),
        ],
        out_specs=[
            pl.BlockSpec((tm, PACKED_WIDTH), lambda i: (i, 0)),
            pl.BlockSpec((tm, LANES), lambda i: (i, 0)),
            pl.BlockSpec((tm, LANES), lambda i: (i, 0)),
        ],
        out_shape=[
            jax.ShapeDtypeStruct((t, PACKED_WIDTH), jnp.uint32),
            jax.ShapeDtypeStruct((t, LANES), jnp.int32),
            jax.ShapeDtypeStruct((t, LANES), F32),
        ],
        compiler_params=_cparams(("arbitrary",)),
        name="router",
    )(x1, g, wr, br)


def _dispatch_kernel(tm, dest_ref, x_ref, xs_in, xs_hbm, sem):
    del xs_in
    base = pl.program_id(0) * tm

    def body(r, c):
        for k in range(TOP_K):
            slot_row = dest_ref[(base + r) * TOP_K + k]
            pltpu.make_async_copy(x_ref.at[pl.ds(r, 1)], xs_hbm.at[pl.ds(slot_row, 1)], sem).start()
        return c

    lax.fori_loop(0, tm, body, 0, unroll=4)
    for _ in range(TOP_K):
        pltpu.make_async_copy(x_ref, xs_hbm.at[pl.ds(0, tm)], sem).wait()


def _dispatch(xn_packed, dest_flat, cap, tm=256):
    t = xn_packed.shape[0]
    return pl.pallas_call(
        functools.partial(_dispatch_kernel, tm),
        grid_spec=pltpu.PrefetchScalarGridSpec(
            num_scalar_prefetch=1,
            grid=(t // tm,),
            in_specs=[pl.BlockSpec((tm, PACKED_WIDTH), lambda i, d: (i, 0)), pl.BlockSpec(memory_space=pl.ANY)],
            out_specs=pl.BlockSpec(memory_space=pl.ANY),
            scratch_shapes=[pltpu.SemaphoreType.DMA],
        ),
        out_shape=jax.ShapeDtypeStruct((cap, PACKED_WIDTH), jnp.uint32),
        input_output_aliases={2: 0},
        compiler_params=_cparams(("arbitrary",)),
        name="dispatch",
    )(dest_flat, xn_packed, jnp.zeros((cap, PACKED_WIDTH), jnp.uint32))


def _stream_panels(first_ref, pidx_ref, meta_ref, pass_idx, n_pass, b, n_slots, panel_copies):
    n_distinct = meta_ref[1]
    n_panels = n_pass * n_distinct
    ahead = n_slots - 1

    def start(q):
        for c in panel_copies(q // n_distinct, q % n_distinct, q % n_slots):
            c.start()

    @pl.when(jnp.logical_and(pass_idx == 0, b == 0))
    def _():
        for q in range(ahead):
            @pl.when(q < n_panels)
            def _():
                start(q)

    p = pass_idx * n_distinct + pidx_ref[b]
    slot = p % n_slots

    @pl.when(first_ref[b] == 1)
    def _():
        for c in panel_copies(pass_idx, pidx_ref[b], slot):
            c.wait()

        @pl.when(p + ahead < n_panels)
        def _():
            start(p + ahead)

    return slot


UP_SLOTS = 3
DOWN_SLOTS = 2


def _expert_up_kernel(layer, tf, first_ref, pidx_ref, pe_ref, meta_ref, x_ref, w_hbm, o_ref, stage, sems):
    f = pl.program_id(0)
    b = pl.program_id(1)

    def panel_copies(fq, k, slot):
        e = pe_ref[k]
        col = pl.multiple_of(fq * tf, tf)
        return [pltpu.make_async_copy(w_hbm.at[layer, e, :, pl.ds(half * D_EXPERT + col, tf)],
                                      stage.at[slot, half], sems.at[slot]) for half in range(2)]

    slot = _stream_panels(first_ref, pidx_ref, meta_ref, f, pl.num_programs(0), b, UP_SLOTS, panel_copies)

    @pl.when(b < meta_ref[0])
    def _():
        words = x_ref[...]
        lo = pltpu.bitcast(words << 16, F32).astype(BF16)
        hi = pltpu.bitcast(words & jnp.uint32(0xFFFF0000), F32).astype(BF16)
        h1 = _mm(lo, stage[slot, 0, :PACKED_WIDTH, :]) + _mm(hi, stage[slot, 0, PACKED_WIDTH:, :])
        h3 = _mm(lo, stage[slot, 1, :PACKED_WIDTH, :]) + _mm(hi, stage[slot, 1, PACKED_WIDTH:, :])
        o_ref[...] = (h1 * jax.nn.sigmoid(h1) * h3).astype(BF16)

    @pl.when(b >= meta_ref[0])
    def _():
        o_ref[...] = jnp.zeros_like(o_ref)


def _expert_up(xs, w_ei, layer, sched, tf=512):
    cap = xs.shape[0]
    nb = cap // EXPERT_ROWS
    nf = D_EXPERT // tf
    return pl.pallas_call(
        functools.partial(_expert_up_kernel, layer, tf),
        grid_spec=pltpu.PrefetchScalarGridSpec(
            num_scalar_prefetch=4,
            grid=(nf, nb),
            in_specs=[
                pl.BlockSpec((EXPERT_ROWS, PACKED_WIDTH), lambda f, b, fi, pi, pe, meta: (jnp.minimum(b, meta[0] - 1), 0)),
                pl.BlockSpec(memory_space=pl.ANY),
            ],
            out_specs=pl.BlockSpec((EXPERT_ROWS, tf), lambda f, b, fi, pi, pe, meta: (b, f)),
            scratch_shapes=[pltpu.VMEM((UP_SLOTS, 2, D_MODEL, tf), F32), pltpu.SemaphoreType.DMA((UP_SLOTS,))],
        ),
        out_shape=jax.ShapeDtypeStruct((cap, D_EXPERT), BF16),
        compiler_params=_cparams(("arbitrary", "arbitrary")),
        name="expert_up",
    )(*sched, xs, w_ei)


def _expert_down_kernel(layer, first_ref, pidx_ref, pe_ref, meta_ref, nvalid_ref, asg_ref, h_ref, w_hbm, y_hbm,
                        stage, obuf, wsems, osems):
    b = pl.program_id(0)
    n_used = meta_ref[0]

    def panel_copies(fq, k, slot):
        del fq
        return [pltpu.make_async_copy(w_hbm.at[layer, pe_ref[k]], stage.at[slot], wsems.at[slot])]

    slot = _stream_panels(first_ref, pidx_ref, meta_ref, 0, 1, b, DOWN_SLOTS, panel_copies)

    def row_copy(blk, r, row):
        return pltpu.make_async_copy(obuf.at[blk % 2, pl.ds(r, 1)], y_hbm.at[pl.ds(row, 1)], osems.at[blk % 2])

    def drain(blk):
        def body(r, c):
            row_copy(blk, r, 0).wait()
            return c

        lax.fori_loop(0, nvalid_ref[blk], body, 0)

    @pl.when(jnp.logical_and(b >= 2, b - 2 < n_used))
    def _():
        drain(b - 2)

    @pl.when(b < n_used)
    def _():
        obuf[b % 2] = _mm(h_ref[...], stage[slot])

        def body(r, c):
            row_copy(b, r, asg_ref[b * EXPERT_ROWS + r]).start()
            return c

        lax.fori_loop(0, nvalid_ref[b], body, 0)

    @pl.when(b == pl.num_programs(0) - 1)
    def _():
        @pl.when(jnp.logical_and(b >= 1, b - 1 < n_used))
        def _():
            drain(b - 1)

        @pl.when(b < n_used)
        def _():
            drain(b)


def _expert_down(hs, w_eo, layer, sched, nvalid, slot_asg, n_assign):
    cap = hs.shape[0]
    nb = cap // EXPERT_ROWS
    return pl.pallas_call(
        functools.partial(_expert_down_kernel, layer),
        grid_spec=pltpu.PrefetchScalarGridSpec(
            num_scalar_prefetch=6,
            grid=(nb,),
            in_specs=[
                pl.BlockSpec((EXPERT_ROWS, D_EXPERT), lambda b, fi, pi, pe, meta, nv, asg: (jnp.minimum(b, meta[0] - 1), 0)),
                pl.BlockSpec(memory_space=pl.ANY),
            ],
            out_specs=pl.BlockSpec(memory_space=pl.ANY),
            scratch_shapes=[pltpu.VMEM((DOWN_SLOTS, D_EXPERT, D_MODEL), F32), pltpu.VMEM((2, EXPERT_ROWS, D_MODEL), F32),
                            pltpu.SemaphoreType.DMA((DOWN_SLOTS,)), pltpu.SemaphoreType.DMA((2,))],
        ),
        out_shape=jax.ShapeDtypeStruct((n_assign, D_MODEL), F32),
        compiler_params=_cparams(("arbitrary",)),
        name="expert_down",
    )(*sched, nvalid, slot_asg, hs, w_eo)


def _combine_kernel(split, x_ref, gate_ref, g_ref, y_ref, *outs):
    i = pl.program_id(0)
    gates = gate_ref[...]
    x2 = x_ref[...] + (y_ref[:, :D_MODEL] * gates[:, 0:1] + y_ref[:, D_MODEL:] * gates[:, 1:2])
    if split is None:
        x2_ref, xn_ref = outs
        x2_ref[...] = x2
        xn_ref[...] = _rms(x2, g_ref[...]).astype(xn_ref.dtype)
    else:
        yp_ref, ysm_ref = outs
        y = _rms(x2, g_ref[...])

        @pl.when(i < split)
        def _():
            yp_ref[...] = y

        @pl.when(i >= split)
        def _():
            ysm_ref[...] = y


def _combine(x1, gates, y_pairs, g, final_split=None, tm=256):
    t = x1.shape[0]
    in_specs = [
        pl.BlockSpec((tm, D_MODEL), lambda i: (i, 0)),
        pl.BlockSpec((tm, LANES), lambda i: (i, 0)),
        pl.BlockSpec((1, D_MODEL), lambda i: (0, 0)),
        pl.BlockSpec((tm, TOP_K * D_MODEL), lambda i: (i, 0)),
    ]
    if final_split is None:
        split = None
        out_specs = [pl.BlockSpec((tm, D_MODEL), lambda i: (i, 0)),
                     pl.BlockSpec((tm, D_MODEL), lambda i: (i, 0))]
        out_shape = [jax.ShapeDtypeStruct((t, D_MODEL), F32), jax.ShapeDtypeStruct((t, D_MODEL), BF16)]
    else:
        split = final_split // tm
        out_specs = [pl.BlockSpec((tm, D_MODEL), lambda i: (jnp.minimum(i, split - 1), 0)),
                     pl.BlockSpec((tm, D_MODEL), lambda i: (jnp.maximum(i - split, 0), 0))]
        out_shape = [jax.ShapeDtypeStruct((final_split, D_MODEL), F32),
                     jax.ShapeDtypeStruct((t - final_split, D_MODEL), F32)]
    return pl.pallas_call(
        functools.partial(_combine_kernel, split),
        grid=(t // tm,),
        in_specs=in_specs,
        out_specs=out_specs,
        out_shape=out_shape,
        compiler_params=_cparams(("arbitrary",)),
        name="combine",
    )(x1, gates, g, y_pairs)


def _routing_tables(eid, n_blocks):
    e_flat = eid.reshape(-1)
    onehot = (e_flat[:, None] == jnp.arange(N_EXPERTS, dtype=jnp.int32)[None, :]).astype(jnp.int32)
    csum = jnp.cumsum(onehot, axis=0)
    rank = jnp.sum(csum * onehot, axis=1) - 1
    counts = csum[-1]
    blocks = (counts + EXPERT_ROWS - 1) // EXPERT_ROWS
    blk_end = jnp.cumsum(blocks)
    blk_start = blk_end - blocks
    dest = (blk_start[e_flat] * EXPERT_ROWS + rank).astype(jnp.int32)
    n_used = blk_end[-1]
    slot_asg = jnp.zeros((n_blocks * EXPERT_ROWS,), jnp.int32).at[dest].set(
        jnp.arange(dest.shape[0], dtype=jnp.int32), unique_indices=True)

    has = blocks > 0
    order = jnp.cumsum(has.astype(jnp.int32)) - 1
    panel_expert = jnp.argsort(jnp.logical_not(has), stable=True)
    blk_ids = jnp.arange(n_blocks, dtype=jnp.int32)
    blk_e = jnp.minimum(jnp.searchsorted(blk_end, blk_ids, side="right"), N_EXPERTS - 1)
    used = blk_ids < n_used
    first = jnp.logical_and(used, blk_ids == blk_start[blk_e])
    pidx = jnp.where(used, order[blk_e], 0)
    nvalid = jnp.where(used, jnp.clip(counts[blk_e] - (blk_ids - blk_start[blk_e]) * EXPERT_ROWS, 0, EXPERT_ROWS), 0)
    meta = jnp.stack([n_used, jnp.sum(has.astype(jnp.int32))])
    i32 = lambda a: a.astype(jnp.int32)
    return dest, slot_asg, i32(nvalid), (i32(first), i32(pidx), i32(panel_expert), i32(meta))


def _gate_tables(w_spatial, b_spatial, sample_len):
    depth = w_spatial.shape[0]
    tril = jnp.tril(jnp.ones((CHUNK, CHUNK), dtype=bool))
    full = jnp.where(tril[None, None], w_spatial, 0)
    reps = CHUNK // sample_len
    small = jnp.where(tril[None, None, :sample_len, :sample_len], w_spatial[:, :, :sample_len, :sample_len], 0)
    eye = jnp.eye(reps, dtype=w_spatial.dtype)
    blockdiag = jnp.einsum("ab,lhts->lhatbs", eye, small).reshape(depth, SG_HEADS, CHUNK, CHUNK)
    mats = jnp.stack([full, blockdiag], axis=1).astype(BF16)
    bias_full = jnp.transpose(b_spatial, (0, 2, 1))
    bias_small = jnp.tile(bias_full[:, :sample_len, :], (1, reps, 1))
    bias = jnp.stack([bias_full, bias_small], axis=1)
    bias = jnp.repeat(bias, SG_HEAD_DIM, axis=-1)
    return mats, bias


def kernel(x_prompt, x_sample, state_pool, state_conv, norm_mix, w_in, w_pool_group, pool_scale, sg_norm_g, sg_norm_b, w_spatial, b_spatial, conv_w, w_branch, w_out, norm_ffn, w_router_group, b_router_group, w_router_expert, b_router_expert, w_expert_in, w_expert_out, norm_final):
    depth = w_in.shape[0]
    bp, lp, _ = x_prompt.shape
    bs, ls, _ = x_sample.shape
    tp, ts = bp * lp, bs * ls
    t = tp + ts
    assert lp % ROW_TILE == 0 and ROW_TILE % ls == 0 and bs % (ROW_TILE // ls) == 0 and ls >= CONV_STATE

    xp = x_prompt.reshape(tp, D_MODEL)
    xs = x_sample.reshape(ts, D_MODEL)

    gate_mats, gate_bias = _gate_tables(w_spatial, b_spatial, ls)
    wg_bf = w_pool_group.astype(BF16)
    vec = lambda a: a.reshape(depth, 1, -1)
    wr = jnp.concatenate([w_router_group, w_router_expert,
                          jnp.zeros((depth, D_MODEL, LANES - N_EXPERT_GROUPS - N_EXPERTS), F32)], axis=-1).astype(BF16)
    br = jnp.concatenate([b_router_group, b_router_expert,
                          jnp.zeros((depth, LANES - N_EXPERT_GROUPS - N_EXPERTS), F32)], axis=-1).reshape(depth, 1, LANES)
    pool_hist = jnp.pad(state_pool, ((0, 0), (0, 0), (POOL_HALO - POOL_STATE, 0), (0, 0))).reshape(
        depth, bs * POOL_HALO, MIX_WIDTH)
    conv_hist = jnp.pad(state_conv, ((0, 0), (0, 0), (CONV_HALO - CONV_STATE, 0), (0, 0))).reshape(
        depth, bs * CONV_HALO, MIX_WIDTH)

    n_assign = t * TOP_K
    n_blocks = n_assign // EXPERT_ROWS + N_EXPERTS

    pool_p, pool_s, conv_p, conv_s, v_out = [], [], [], [], []
    x_parts = (xp, xs)
    xn = _norm_in(xp, xs, norm_mix[0].reshape(1, D_MODEL))
    y_p = y_s = None
    for l in range(depth):
        proj = _in_proj(xn, w_in, l)
        oa, ob, oc, z_tail, a_tail, v_s, z_s = _branches(
            proj, l, bp, lp, bs, ls, pool_hist, conv_hist, wg_bf, vec(pool_scale), vec(sg_norm_g),
            vec(sg_norm_b), gate_mats, gate_bias, conv_w)
        merged = _merge(oa, ob, oc, w_branch, proj, l)
        x1 = _out_proj(merged, w_out, l, x_parts)

        xn2, eid, gates = _router(x1, norm_ffn[l].reshape(1, D_MODEL), wr[l], br[l])
        dest, slot_asg, nvalid, sched = _routing_tables(eid[:, :TOP_K], n_blocks)
        xs_sorted = _dispatch(xn2, dest, n_blocks * EXPERT_ROWS)
        hs = _expert_up(xs_sorted, w_expert_in, l, sched)
        ys = _expert_down(hs, w_expert_out, l, sched, nvalid, slot_asg, n_assign).reshape(t, TOP_K * D_MODEL)
        if l + 1 < depth:
            x2, xn = _combine(x1, gates, ys, norm_mix[l + 1].reshape(1, D_MODEL))
            x_parts = (x2,)
        else:
            y_p, y_s = _combine(x1, gates, ys, norm_final.reshape(1, D_MODEL), final_split=tp)

        a_s = proj[tp:, :MIX_WIDTH].reshape(bs, ls, MIX_WIDTH)
        pool_p.append(a_tail[:, POOL_HALO - POOL_STATE:])
        pool_s.append(jnp.concatenate([state_pool[l], a_s], axis=1)[:, -POOL_STATE:])
        conv_p.append(z_tail[:, CONV_HALO - CONV_STATE:])
        conv_s.append(z_s.reshape(bs, ls, MIX_WIDTH)[:, ls - CONV_STATE:])
        v_out.append(v_s.reshape(bs, ls, MIX_WIDTH))

    return (y_p.reshape(bp, lp, D_MODEL), y_s.reshape(bs, ls, D_MODEL), jnp.stack(pool_p), jnp.stack(pool_s),
            jnp.stack(conv_p), jnp.stack(conv_s), jnp.stack(v_out))
```

```python
import functools

import jax
import jax.numpy as jnp
from jax import lax
from jax.experimental import pallas as pl
from jax.experimental.pallas import tpu as pltpu

F32 = jnp.float32
BF16 = jnp.bfloat16

D_MODEL = 4096
MIX_WIDTH = D_MODEL // 2
POOL_GROUPS = 4
POOL_WINDOWS = (2, 4, 8, 16)
POOL_GROUP_DIM = MIX_WIDTH // POOL_GROUPS
POOL_STATE = 15
SG_HEADS = 8
SG_HEAD_DIM = MIX_WIDTH // SG_HEADS
CHUNK = 128
CONV_STATE = 2
IN_COLS = MIX_WIDTH * 6 + 3 * D_MODEL
GATE_COL0 = MIX_WIDTH * 6
N_EXPERT_GROUPS = 4
EXPERTS_PER_GROUP = 8
N_EXPERTS = N_EXPERT_GROUPS * EXPERTS_PER_GROUP
TOP_K = 2
D_EXPERT = D_MODEL // 4
PAST_LEN = 16384
EPS = 1e-6

LANES = 128
SUBLANES = 8
POOL_HALO = 16
CONV_HALO = 8
ROW_TILE = 128
EXPERT_ROWS = 256
PACKED_WIDTH = D_MODEL // 2
VMEM_LIMIT = 56 * 1024 * 1024


def _cparams(sem):
    return pltpu.CompilerParams(dimension_semantics=sem, vmem_limit_bytes=VMEM_LIMIT)


def _gelu_tanh(x):
    c = 0.7978845608028654
    return 0.5 * x * (1.0 + jnp.tanh(c * (x + 0.044715 * (x * x * x))))


def _rms(x, g):
    return x * lax.rsqrt(jnp.mean(x * x, axis=-1, keepdims=True) + EPS) * g


def _norm_in_kernel(n_prompt_tiles, xp_ref, xs_ref, g_ref, o_ref):
    i = pl.program_id(0)

    @pl.when(i < n_prompt_tiles)
    def _():
        o_ref[...] = _rms(xp_ref[...], g_ref[...]).astype(BF16)

    @pl.when(i >= n_prompt_tiles)
    def _():
        o_ref[...] = _rms(xs_ref[...], g_ref[...]).astype(BF16)


def _norm_in(xp, xs, g, tm=256):
    tp, ts = xp.shape[0], xs.shape[0]
    npt, nst = tp // tm, ts // tm
    return pl.pallas_call(
        functools.partial(_norm_in_kernel, npt),
        grid=(npt + nst,),
        in_specs=[
            pl.BlockSpec((tm, D_MODEL), lambda i: (jnp.minimum(i, npt - 1), 0)),
            pl.BlockSpec((tm, D_MODEL), lambda i: (jnp.maximum(i - npt, 0), 0)),
            pl.BlockSpec((1, D_MODEL), lambda i: (0, 0)),
        ],
        out_specs=pl.BlockSpec((tm, D_MODEL), lambda i: (i, 0)),
        out_shape=jax.ShapeDtypeStruct((tp + ts, D_MODEL), BF16),
        compiler_params=_cparams(("arbitrary",)),
        name="norm_in",
    )(xp, xs, g)


def _mm(a, w):
    return lax.dot_general(a, w, (((1,), (0,)), ((), ())), preferred_element_type=F32)


def _panel_matmul_kernel(a_ref, w_ref, o_ref):
    o_ref[...] = _mm(a_ref[...], w_ref[...])


def _in_proj(xn, w_in, layer, tm=768, tn=1024):
    t = xn.shape[0]
    return pl.pallas_call(
        _panel_matmul_kernel,
        grid=(IN_COLS // tn, t // tm),
        in_specs=[
            pl.BlockSpec((tm, D_MODEL), lambda j, i: (i, 0)),
            pl.BlockSpec((None, D_MODEL, tn), lambda j, i: (layer, 0, j)),
        ],
        out_specs=pl.BlockSpec((tm, tn), lambda j, i: (i, j)),
        out_shape=jax.ShapeDtypeStruct((t, IN_COLS), F32),
        compiler_params=_cparams(("arbitrary", "arbitrary")),
        name="in_proj",
    )(xn, w_in)


def _window_rows(ext, n_seq, halo, rows):
    if n_seq == 1:
        return ext[halo:, :]
    c = ext.shape[-1]
    return ext.reshape(n_seq, halo + rows, c)[:, halo:, :].reshape(n_seq * rows, c)


def _stack_history(hist, cur, n_seq, halo, rows):
    if n_seq == 1:
        return jnp.concatenate([hist, cur], axis=0)
    c = cur.shape[-1]
    ext = jnp.concatenate([hist.reshape(n_seq, halo, c), cur.reshape(n_seq, rows, c)], axis=1)
    return ext.reshape(n_seq * (halo + rows), c)


def _branch_math(n_seq, rows, pos, a_ref, u_ref, v_ref, ci_ref, cb_ref, cc_ref, pool_hist, conv_hist,
                 wg_ref, ps_ref, lng_ref, lnb_ref, m_ref, bias_ref, cw_ref, oa_ref, ob_ref, oc_ref):
    for g, w in enumerate(POOL_WINDOWS):
        cols = slice(g * POOL_GROUP_DIM, (g + 1) * POOL_GROUP_DIM)
        a_g = a_ref[:, cols]
        s = _stack_history(pool_hist[:, cols], a_g, n_seq, POOL_HALO, rows)
        k = 1
        while k < w:
            s = s + pltpu.roll(s, k, 0)
            k *= 2
        s = _window_rows(s, n_seq, POOL_HALO, rows)
        cnt = jnp.minimum(w, pos + 1).astype(F32)
        pooled = (s / cnt - a_g).astype(BF16)
        out = jnp.dot(pooled, wg_ref[g], preferred_element_type=F32) * ps_ref[:, cols]
        oa_ref[:, cols] = out.astype(BF16)

    vg = _gelu_tanh(v_ref[...])
    mu = jnp.mean(vg, axis=-1, keepdims=True)
    xc = vg - mu
    v = xc * lax.rsqrt(jnp.mean(xc * xc, axis=-1, keepdims=True) + EPS) * lng_ref[...] + lnb_ref[...]
    vb = v.astype(BF16)
    for h in range(SG_HEADS):
        cols = slice(h * SG_HEAD_DIM, (h + 1) * SG_HEAD_DIM)
        sp = jnp.dot(m_ref[h], vb[:, cols], preferred_element_type=F32) + bias_ref[:, cols]
        ob_ref[:, cols] = (_gelu_tanh(u_ref[:, cols]) * sp).astype(BF16)

    z = cc_ref[...] * ci_ref[...]
    e = _stack_history(conv_hist, z, n_seq, CONV_HALO, rows)
    y = cw_ref[0:1, :] * pltpu.roll(e, 2, 0)
    y = y + cw_ref[1:2, :] * pltpu.roll(e, 1, 0)
    y = y + cw_ref[2:3, :] * e
    oc_ref[...] = (cb_ref[...] * _window_rows(y, n_seq, CONV_HALO, rows)).astype(BF16)
    return v, z


def _branch_kernel(tiles_per_seq, n_prompt_tiles, seq_per_tile, sample_len,
                   a_ref, u_ref, v_ref, ci_ref, cb_ref, cc_ref, ph_ref, ch_ref, wg_ref, ps_ref, lng_ref, lnb_ref,
                   m_ref, bias_ref, cw_ref, oa_ref, ob_ref, oc_ref, zl_ref, al_ref, vo_ref, zo_ref, pool_hist, conv_hist):
    i = pl.program_id(0)
    shared = (wg_ref, ps_ref, lng_ref, lnb_ref, m_ref, bias_ref, cw_ref, oa_ref, ob_ref, oc_ref)

    @pl.when(i < n_prompt_tiles)
    def _():
        t = i % tiles_per_seq

        @pl.when(t == 0)
        def _():
            pool_hist[...] = jnp.zeros_like(pool_hist)
            conv_hist[...] = jnp.zeros_like(conv_hist)

        pos = t * ROW_TILE + lax.broadcasted_iota(jnp.int32, (ROW_TILE, 1), 0)
        _, z = _branch_math(1, ROW_TILE, pos, a_ref, u_ref, v_ref, ci_ref, cb_ref, cc_ref, pool_hist[...],
                            conv_hist[...], *shared)
        pool_hist[...] = a_ref[ROW_TILE - POOL_HALO:, :]
        conv_hist[...] = z[ROW_TILE - CONV_HALO:, :]
        zl_ref[...] = z[ROW_TILE - CONV_HALO:, :]
        al_ref[...] = a_ref[ROW_TILE - POOL_HALO:, :]

    @pl.when(i >= n_prompt_tiles)
    def _():
        r = lax.broadcasted_iota(jnp.int32, (seq_per_tile * sample_len, 1), 0)
        pos = PAST_LEN + (r % sample_len)
        v, z = _branch_math(seq_per_tile, sample_len, pos, a_ref, u_ref, v_ref, ci_ref, cb_ref, cc_ref,
                            ph_ref[...], ch_ref[...], *shared)
        vo_ref[...] = v
        zo_ref[...] = z


def _branches(proj, layer, n_prompt_seq, prompt_len, n_sample_seq, sample_len, pool_state, conv_state,
              wg_bf, pool_scale, ln_g, ln_b, gate_mats, gate_bias, conv_w):
    t_total = proj.shape[0]
    tiles_per_seq = prompt_len // ROW_TILE
    npt = n_prompt_seq * tiles_per_seq
    seq_per_tile = ROW_TILE // sample_len
    nst = n_sample_seq // seq_per_tile
    sample_rows = n_sample_seq * sample_len
    act = jax.ShapeDtypeStruct((t_total, MIX_WIDTH), BF16)
    sample_f32 = jax.ShapeDtypeStruct((sample_rows, MIX_WIDTH), F32)

    def sample_tile(i):
        return jnp.maximum(i - npt, 0)

    def kind(i):
        return jnp.where(i >= npt, 1, 0)

    slab_specs = [pl.BlockSpec((ROW_TILE, MIX_WIDTH), functools.partial(lambda c, i: (i, c), c)) for c in range(6)]
    vec_spec = pl.BlockSpec((None, 1, MIX_WIDTH), lambda i: (layer, 0, 0))
    in_specs = slab_specs + [
        pl.BlockSpec((None, seq_per_tile * POOL_HALO, MIX_WIDTH), lambda i: (layer, sample_tile(i), 0)),
        pl.BlockSpec((None, seq_per_tile * CONV_HALO, MIX_WIDTH), lambda i: (layer, sample_tile(i), 0)),
        pl.BlockSpec((None, POOL_GROUPS, POOL_GROUP_DIM, POOL_GROUP_DIM), lambda i: (layer, 0, 0, 0)),
        vec_spec, vec_spec, vec_spec,
        pl.BlockSpec((None, None, SG_HEADS, CHUNK, CHUNK), lambda i: (layer, kind(i), 0, 0, 0)),
        pl.BlockSpec((None, None, CHUNK, MIX_WIDTH), lambda i: (layer, kind(i), 0, 0)),
        pl.BlockSpec((None, 3, MIX_WIDTH), lambda i: (layer, 0, 0)),
    ]
    row_spec = pl.BlockSpec((ROW_TILE, MIX_WIDTH), lambda i: (i, 0))
    out_specs = [
        row_spec, row_spec, row_spec,
        pl.BlockSpec((None, CONV_HALO, MIX_WIDTH), lambda i: (jnp.minimum(i // tiles_per_seq, n_prompt_seq - 1), 0, 0)),
        pl.BlockSpec((None, POOL_HALO, MIX_WIDTH), lambda i: (jnp.minimum(i // tiles_per_seq, n_prompt_seq - 1), 0, 0)),
        pl.BlockSpec((ROW_TILE, MIX_WIDTH), lambda i: (sample_tile(i), 0)),
        pl.BlockSpec((ROW_TILE, MIX_WIDTH), lambda i: (sample_tile(i), 0)),
    ]
    return pl.pallas_call(
        functools.partial(_branch_kernel, tiles_per_seq, npt, seq_per_tile, sample_len),
        grid=(npt + nst,),
        in_specs=in_specs,
        out_specs=out_specs,
        out_shape=[act, act, act, jax.ShapeDtypeStruct((n_prompt_seq, CONV_HALO, MIX_WIDTH), F32),
                   jax.ShapeDtypeStruct((n_prompt_seq, POOL_HALO, MIX_WIDTH), F32), sample_f32, sample_f32],
        scratch_shapes=[pltpu.VMEM((POOL_HALO, MIX_WIDTH), F32), pltpu.VMEM((CONV_HALO, MIX_WIDTH), F32)],
        compiler_params=_cparams(("arbitrary",)),
        name="branches",
    )(proj, proj, proj, proj, proj, proj, pool_state, conv_state, wg_bf, pool_scale, ln_g, ln_b,
      gate_mats, gate_bias, conv_w)


def _merge_kernel(oa_ref, ob_ref, oc_ref, wa_ref, wb_ref, wc_ref, ga_ref, gb_ref, gc_ref, o_ref):
    m = jax.nn.sigmoid(ga_ref[...]) * _mm(oa_ref[...], wa_ref[...])
    m = m + jax.nn.sigmoid(gb_ref[...]) * _mm(ob_ref[...], wb_ref[...])
    m = m + jax.nn.sigmoid(gc_ref[...]) * _mm(oc_ref[...], wc_ref[...])
    o_ref[...] = m.astype(BF16)


def _merge(oa, ob, oc, w_branch, proj, layer, tm=768, tn=512):
    t = oa.shape[0]
    act_spec = pl.BlockSpec((tm, MIX_WIDTH), lambda j, i: (i, 0))

    def wspec(b):
        return pl.BlockSpec((None, None, MIX_WIDTH, tn), lambda j, i: (layer, b, 0, j))

    def gspec(b):
        off = (GATE_COL0 + b * D_MODEL) // tn
        return pl.BlockSpec((tm, tn), lambda j, i: (i, off + j))

    return pl.pallas_call(
        _merge_kernel,
        grid=(D_MODEL // tn, t // tm),
        in_specs=[act_spec, act_spec, act_spec, wspec(0), wspec(1), wspec(2), gspec(0), gspec(1), gspec(2)],
        out_specs=pl.BlockSpec((tm, tn), lambda j, i: (i, j)),
        out_shape=jax.ShapeDtypeStruct((t, D_MODEL), BF16),
        compiler_params=_cparams(("arbitrary", "arbitrary")),
        name="merge",
    )(oa, ob, oc, w_branch, w_branch, w_branch, proj, proj, proj)


def _out_kernel2(n_prompt_tiles, a_ref, w_ref, xp_ref, xs_ref, o_ref):
    i = pl.program_id(1)
    h = _mm(a_ref[...], w_ref[...])

    @pl.when(i < n_prompt_tiles)
    def _():
        o_ref[...] = xp_ref[...] + h

    @pl.when(i >= n_prompt_tiles)
    def _():
        o_ref[...] = xs_ref[...] + h


def _out_kernel1(a_ref, w_ref, x_ref, o_ref):
    o_ref[...] = x_ref[...] + _mm(a_ref[...], w_ref[...])


def _out_proj(merged, w_out, layer, x_parts, tm=1024, tn=512):
    t = merged.shape[0]
    common = dict(
        grid=(D_MODEL // tn, t // tm),
        out_specs=pl.BlockSpec((tm, tn), lambda j, i: (i, j)),
        out_shape=jax.ShapeDtypeStruct((t, D_MODEL), F32),
        compiler_params=_cparams(("arbitrary", "arbitrary")),
        name="out_proj",
    )
    a_spec = pl.BlockSpec((tm, D_MODEL), lambda j, i: (i, 0))
    w_spec = pl.BlockSpec((None, D_MODEL, tn), lambda j, i: (layer, 0, j))
    if len(x_parts) == 1:
        return pl.pallas_call(
            _out_kernel1,
            in_specs=[a_spec, w_spec, pl.BlockSpec((tm, tn), lambda j, i: (i, j))],
            **common,
        )(merged, w_out, x_parts[0])
    xp, xs = x_parts
    npt = xp.shape[0] // tm
    return pl.pallas_call(
        functools.partial(_out_kernel2, npt),
        in_specs=[
            a_spec, w_spec,
            pl.BlockSpec((tm, tn), lambda j, i: (jnp.minimum(i, npt - 1), j)),
            pl.BlockSpec((tm, tn), lambda j, i: (jnp.maximum(i - npt, 0), j)),
        ],
        **common,
    )(merged, w_out, xp, xs)


def _router_kernel(x_ref, g_ref, wr_ref, br_ref, xn_ref, eid_ref, gate_ref):
    xb = _rms(x_ref[...], g_ref[...]).astype(BF16)
    lo = pltpu.bitcast(xb[:, :PACKED_WIDTH].astype(F32), jnp.uint32) >> 16
    hi = pltpu.bitcast(xb[:, PACKED_WIDTH:].astype(F32), jnp.uint32) & jnp.uint32(0xFFFF0000)
    xn_ref[...] = hi | lo
    logits = jnp.dot(xb, wr_ref[...], preferred_element_type=F32) + br_ref[...]
    tm = logits.shape[0]
    lane = lax.broadcasted_iota(jnp.int32, (tm, LANES), 1)
    neg = jnp.float32(-jnp.inf)
    far = jnp.int32(LANES)

    is_g = lane < N_EXPERT_GROUPS
    glog = jnp.where(is_g, logits, neg)
    gmax = jnp.max(glog, axis=1, keepdims=True)
    gsel = jnp.min(jnp.where(glog == gmax, lane, far), axis=1, keepdims=True)
    gsum = jnp.sum(jnp.where(is_g, jnp.exp(glog - gmax), 0.0), axis=1, keepdims=True)
    gp = 1.0 / gsum

    lo = N_EXPERT_GROUPS + gsel * EXPERTS_PER_GROUP
    in_grp = jnp.logical_and(lane >= lo, lane < lo + EXPERTS_PER_GROUP)
    el = jnp.where(in_grp, logits, neg)
    m1 = jnp.max(el, axis=1, keepdims=True)
    i1 = jnp.min(jnp.where(el == m1, lane, far), axis=1, keepdims=True)
    el2 = jnp.where(lane == i1, neg, el)
    m2 = jnp.max(el2, axis=1, keepdims=True)
    i2 = jnp.min(jnp.where(el2 == m2, lane, far), axis=1, keepdims=True)
    e2 = jnp.exp(m2 - m1)
    den = 1.0 + e2
    g1 = gp * (1.0 / den)
    g2 = gp * (e2 / den)

    eid_ref[...] = jnp.where(lane == 0, i1 - N_EXPERT_GROUPS, jnp.where(lane == 1, i2 - N_EXPERT_GROUPS, 0))
    gate_ref[...] = jnp.where(lane == 0, g1, jnp.where(lane == 1, g2, 0.0))


def _router(x1, g, wr, br, tm=256):
    t = x1.shape[0]
    return pl.pallas_call(
        _router_kernel,
        grid=(t // tm,),
        in_specs=[
            pl.BlockSpec((tm, D_MODEL), lambda i: (i, 0)),
            pl.BlockSpec((1, D_MODEL), lambda i: (0, 0)),
            pl.BlockSpec((D_MODEL, LANES), lambda i: (0, 0)),
            pl.BlockSpec((1, LANES), lambda i: (0, 0)),
        ],
        out_specs=[
            pl.BlockSpec((tm, PACKED_WIDTH), lambda i: (i, 0)),
            pl.BlockSpec((tm, LANES), lambda i: (i, 0)),
            pl.BlockSpec((tm, LANES), lambda i: (i, 0)),
        ],
        out_shape=[
            jax.ShapeDtypeStruct((t, PACKED_WIDTH), jnp.uint32),
            jax.ShapeDtypeStruct((t, LANES), jnp.int32),
            jax.ShapeDtypeStruct((t, LANES), F32),
        ],
        compiler_params=_cparams(("arbitrary",)),
        name="router",
    )(x1, g, wr, br)


def _dispatch_kernel(tm, dest_ref, x_ref, xs_in, xs_hbm, sem):
    del xs_in
    base = pl.program_id(0) * tm

    def body(r, c):
        for k in range(TOP_K):
            slot_row = dest_ref[(base + r) * TOP_K + k]
            pltpu.make_async_copy(x_ref.at[pl.ds(r, 1)], xs_hbm.at[pl.ds(slot_row, 1)], sem).start()
        return c

    lax.fori_loop(0, tm, body, 0, unroll=4)
    for _ in range(TOP_K):
        pltpu.make_async_copy(x_ref, xs_hbm.at[pl.ds(0, tm)], sem).wait()


def _dispatch(xn_packed, dest_flat, cap, tm=256):
    t = xn_packed.shape[0]
    return pl.pallas_call(
        functools.partial(_dispatch_kernel, tm),
        grid_spec=pltpu.PrefetchScalarGridSpec(
            num_scalar_prefetch=1,
            grid=(t // tm,),
            in_specs=[pl.BlockSpec((tm, PACKED_WIDTH), lambda i, d: (i, 0)), pl.BlockSpec(memory_space=pl.ANY)],
            out_specs=pl.BlockSpec(memory_space=pl.ANY),
            scratch_shapes=[pltpu.SemaphoreType.DMA],
        ),
        out_shape=jax.ShapeDtypeStruct((cap, PACKED_WIDTH), jnp.uint32),
        input_output_aliases={2: 0},
        compiler_params=_cparams(("arbitrary",)),
        name="dispatch",
    )(dest_flat, xn_packed, jnp.zeros((cap, PACKED_WIDTH), jnp.uint32))


def _stream_panels(first_ref, pidx_ref, meta_ref, pass_idx, n_pass, b, n_slots, panel_copies):
    n_distinct = meta_ref[1]
    n_panels = n_pass * n_distinct
    ahead = n_slots - 1

    def start(q):
        for c in panel_copies(q // n_distinct, q % n_distinct, q % n_slots):
            c.start()

    @pl.when(jnp.logical_and(pass_idx == 0, b == 0))
    def _():
        for q in range(ahead):
            @pl.when(q < n_panels)
            def _():
                start(q)

    p = pass_idx * n_distinct + pidx_ref[b]
    slot = p % n_slots

    @pl.when(first_ref[b] == 1)
    def _():
        for c in panel_copies(pass_idx, pidx_ref[b], slot):
            c.wait()

        @pl.when(p + ahead < n_panels)
        def _():
            start(p + ahead)

    return slot


UP_SLOTS = 3
DOWN_SLOTS = 2


def _expert_up_kernel(layer, tf, first_ref, pidx_ref, pe_ref, meta_ref, x_ref, w_hbm, o_ref, stage, sems):
    f = pl.program_id(0)
    b = pl.program_id(1)

    def panel_copies(fq, k, slot):
        e = pe_ref[k]
        col = pl.multiple_of(fq * tf, tf)
        return [pltpu.make_async_copy(w_hbm.at[layer, e, :, pl.ds(half * D_EXPERT + col, tf)],
                                      stage.at[slot, half], sems.at[slot]) for half in range(2)]

    slot = _stream_panels(first_ref, pidx_ref, meta_ref, f, pl.num_programs(0), b, UP_SLOTS, panel_copies)

    @pl.when(b < meta_ref[0])
    def _():
        words = x_ref[...]
        lo = pltpu.bitcast(words << 16, F32).astype(BF16)
        hi = pltpu.bitcast(words & jnp.uint32(0xFFFF0000), F32).astype(BF16)
        h1 = _mm(lo, stage[slot, 0, :PACKED_WIDTH, :]) + _mm(hi, stage[slot, 0, PACKED_WIDTH:, :])
        h3 = _mm(lo, stage[slot, 1, :PACKED_WIDTH, :]) + _mm(hi, stage[slot, 1, PACKED_WIDTH:, :])
        o_ref[...] = (h1 * jax.nn.sigmoid(h1) * h3).astype(BF16)

    @pl.when(b >= meta_ref[0])
    def _():
        o_ref[...] = jnp.zeros_like(o_ref)


def _expert_up(xs, w_ei, layer, sched, tf=512):
    cap = xs.shape[0]
    nb = cap // EXPERT_ROWS
    nf = D_EXPERT // tf
    return pl.pallas_call(
        functools.partial(_expert_up_kernel, layer, tf),
        grid_spec=pltpu.PrefetchScalarGridSpec(
            num_scalar_prefetch=4,
            grid=(nf, nb),
            in_specs=[
                pl.BlockSpec((EXPERT_ROWS, PACKED_WIDTH), lambda f, b, fi, pi, pe, meta: (jnp.minimum(b, meta[0] - 1), 0)),
                pl.BlockSpec(memory_space=pl.ANY),
            ],
            out_specs=pl.BlockSpec((EXPERT_ROWS, tf), lambda f, b, fi, pi, pe, meta: (b, f)),
            scratch_shapes=[pltpu.VMEM((UP_SLOTS, 2, D_MODEL, tf), F32), pltpu.SemaphoreType.DMA((UP_SLOTS,))],
        ),
        out_shape=jax.ShapeDtypeStruct((cap, D_EXPERT), BF16),
        compiler_params=_cparams(("arbitrary", "arbitrary")),
        name="expert_up",
    )(*sched, xs, w_ei)


def _expert_down_kernel(layer, first_ref, pidx_ref, pe_ref, meta_ref, h_ref, w_hbm, o_ref, stage, sems):
    b = pl.program_id(0)

    def panel_copies(fq, k, slot):
        del fq
        return [pltpu.make_async_copy(w_hbm.at[layer, pe_ref[k]], stage.at[slot], sems.at[slot])]

    slot = _stream_panels(first_ref, pidx_ref, meta_ref, 0, 1, b, DOWN_SLOTS, panel_copies)

    @pl.when(b < meta_ref[0])
    def _():
        o_ref[...] = _mm(h_ref[...], stage[slot])

    @pl.when(b >= meta_ref[0])
    def _():
        o_ref[...] = jnp.zeros_like(o_ref)


def _expert_down(hs, w_eo, layer, sched):
    cap = hs.shape[0]
    nb = cap // EXPERT_ROWS
    return pl.pallas_call(
        functools.partial(_expert_down_kernel, layer),
        grid_spec=pltpu.PrefetchScalarGridSpec(
            num_scalar_prefetch=4,
            grid=(nb,),
            in_specs=[
                pl.BlockSpec((EXPERT_ROWS, D_EXPERT), lambda b, fi, pi, pe, meta: (jnp.minimum(b, meta[0] - 1), 0)),
                pl.BlockSpec(memory_space=pl.ANY),
            ],
            out_specs=pl.BlockSpec((EXPERT_ROWS, D_MODEL), lambda b, fi, pi, pe, meta: (b, 0)),
            scratch_shapes=[pltpu.VMEM((DOWN_SLOTS, D_EXPERT, D_MODEL), F32), pltpu.SemaphoreType.DMA((DOWN_SLOTS,))],
        ),
        out_shape=jax.ShapeDtypeStruct((cap, D_MODEL), F32),
        compiler_params=_cparams(("arbitrary",)),
        name="expert_down",
    )(*sched, hs, w_eo)


def _gather_rows(idx_ref, idx_base, idx_stride, n_rows, src_hbm, dst, sem):
    def body(r, c):
        row = idx_ref[idx_base + r * idx_stride]
        pltpu.make_async_copy(src_hbm.at[pl.ds(row, 1)], dst.at[pl.ds(r, 1)], sem).start()
        return c

    lax.fori_loop(0, n_rows, body, 0, unroll=8)


def _wait_rows(n_rows, src_hbm, dst, sem):
    pltpu.make_async_copy(src_hbm.at[pl.ds(0, n_rows)], dst, sem).wait()


def _combine_kernel(tm, split, dest_ref, x_ref, gate_ref, g_ref, ys_hbm, *rest):
    if split is None:
        x2_ref, xn_ref, ybuf, sem = rest
    else:
        yp_ref, ysm_ref, ybuf, sem = rest
    i = pl.program_id(0)

    def issue(tile):
        slot = tile % 2
        for k in range(TOP_K):
            _gather_rows(dest_ref, tile * tm * TOP_K + k, TOP_K, tm, ys_hbm, ybuf.at[slot, k], sem.at[slot])

    @pl.when(i == 0)
    def _():
        issue(i)

    @pl.when(i + 1 < pl.num_programs(0))
    def _():
        issue(i + 1)

    slot = i % 2
    for k in range(TOP_K):
        _wait_rows(tm, ys_hbm, ybuf.at[slot, k], sem.at[slot])
    gates = gate_ref[...]
    x2 = x_ref[...] + (ybuf[slot, 0] * gates[:, 0:1] + ybuf[slot, 1] * gates[:, 1:2])
    if split is None:
        x2_ref[...] = x2
        xn_ref[...] = _rms(x2, g_ref[...]).astype(xn_ref.dtype)
    else:
        y = _rms(x2, g_ref[...])

        @pl.when(i < split)
        def _():
            yp_ref[...] = y

        @pl.when(i >= split)
        def _():
            ysm_ref[...] = y


def _combine(x1, gates, dest_flat, ys, g, final_split=None, tm=128):
    t = x1.shape[0]
    in_specs = [
        pl.BlockSpec((tm, D_MODEL), lambda i, d: (i, 0)),
        pl.BlockSpec((tm, LANES), lambda i, d: (i, 0)),
        pl.BlockSpec((1, D_MODEL), lambda i, d: (0, 0)),
        pl.BlockSpec(memory_space=pl.ANY),
    ]
    if final_split is None:
        split = None
        out_specs = [pl.BlockSpec((tm, D_MODEL), lambda i, d: (i, 0)),
                     pl.BlockSpec((tm, D_MODEL), lambda i, d: (i, 0))]
        out_shape = [jax.ShapeDtypeStruct((t, D_MODEL), F32), jax.ShapeDtypeStruct((t, D_MODEL), BF16)]
    else:
        split = final_split // tm
        out_specs = [pl.BlockSpec((tm, D_MODEL), lambda i, d: (jnp.minimum(i, split - 1), 0)),
                     pl.BlockSpec((tm, D_MODEL), lambda i, d: (jnp.maximum(i - split, 0), 0))]
        out_shape = [jax.ShapeDtypeStruct((final_split, D_MODEL), F32),
                     jax.ShapeDtypeStruct((t - final_split, D_MODEL), F32)]
    return pl.pallas_call(
        functools.partial(_combine_kernel, tm, split),
        grid_spec=pltpu.PrefetchScalarGridSpec(
            num_scalar_prefetch=1,
            grid=(t // tm,),
            in_specs=in_specs,
            out_specs=out_specs,
            scratch_shapes=[pltpu.VMEM((2, TOP_K, tm, D_MODEL), F32), pltpu.SemaphoreType.DMA((2,))],
        ),
        out_shape=out_shape,
        compiler_params=_cparams(("arbitrary",)),
        name="combine",
    )(dest_flat, x1, gates, g, ys)


def _routing_tables(eid, n_blocks):
    e_flat = eid.reshape(-1)
    onehot = (e_flat[:, None] == jnp.arange(N_EXPERTS, dtype=jnp.int32)[None, :]).astype(jnp.int32)
    csum = jnp.cumsum(onehot, axis=0)
    rank = jnp.sum(csum * onehot, axis=1) - 1
    counts = csum[-1]
    blocks = (counts + EXPERT_ROWS - 1) // EXPERT_ROWS
    blk_end = jnp.cumsum(blocks)
    blk_start = blk_end - blocks
    dest = (blk_start[e_flat] * EXPERT_ROWS + rank).astype(jnp.int32)
    n_used = blk_end[-1]

    has = blocks > 0
    order = jnp.cumsum(has.astype(jnp.int32)) - 1
    panel_expert = jnp.argsort(jnp.logical_not(has), stable=True)
    blk_ids = jnp.arange(n_blocks, dtype=jnp.int32)
    blk_e = jnp.minimum(jnp.searchsorted(blk_end, blk_ids, side="right"), N_EXPERTS - 1)
    used = blk_ids < n_used
    first = jnp.logical_and(used, blk_ids == blk_start[blk_e])
    pidx = jnp.where(used, order[blk_e], 0)
    meta = jnp.stack([n_used, jnp.sum(has.astype(jnp.int32))])
    i32 = lambda a: a.astype(jnp.int32)
    return dest, (i32(first), i32(pidx), i32(panel_expert), i32(meta))


def _gate_tables(w_spatial, b_spatial, sample_len):
    depth = w_spatial.shape[0]
    tril = jnp.tril(jnp.ones((CHUNK, CHUNK), dtype=bool))
    full = jnp.where(tril[None, None], w_spatial, 0)
    reps = CHUNK // sample_len
    small = jnp.where(tril[None, None, :sample_len, :sample_len], w_spatial[:, :, :sample_len, :sample_len], 0)
    eye = jnp.eye(reps, dtype=w_spatial.dtype)
    blockdiag = jnp.einsum("ab,lhts->lhatbs", eye, small).reshape(depth, SG_HEADS, CHUNK, CHUNK)
    mats = jnp.stack([full, blockdiag], axis=1).astype(BF16)
    bias_full = jnp.transpose(b_spatial, (0, 2, 1))
    bias_small = jnp.tile(bias_full[:, :sample_len, :], (1, reps, 1))
    bias = jnp.stack([bias_full, bias_small], axis=1)
    bias = jnp.repeat(bias, SG_HEAD_DIM, axis=-1)
    return mats, bias


def kernel(x_prompt, x_sample, state_pool, state_conv, norm_mix, w_in, w_pool_group, pool_scale, sg_norm_g, sg_norm_b, w_spatial, b_spatial, conv_w, w_branch, w_out, norm_ffn, w_router_group, b_router_group, w_router_expert, b_router_expert, w_expert_in, w_expert_out, norm_final):
    depth = w_in.shape[0]
    bp, lp, _ = x_prompt.shape
    bs, ls, _ = x_sample.shape
    tp, ts = bp * lp, bs * ls
    t = tp + ts
    assert lp % ROW_TILE == 0 and ROW_TILE % ls == 0 and bs % (ROW_TILE // ls) == 0 and ls >= CONV_STATE

    xp = x_prompt.reshape(tp, D_MODEL)
    xs = x_sample.reshape(ts, D_MODEL)

    gate_mats, gate_bias = _gate_tables(w_spatial, b_spatial, ls)
    wg_bf = w_pool_group.astype(BF16)
    vec = lambda a: a.reshape(depth, 1, -1)
    wr = jnp.concatenate([w_router_group, w_router_expert,
                          jnp.zeros((depth, D_MODEL, LANES - N_EXPERT_GROUPS - N_EXPERTS), F32)], axis=-1).astype(BF16)
    br = jnp.concatenate([b_router_group, b_router_expert,
                          jnp.zeros((depth, LANES - N_EXPERT_GROUPS - N_EXPERTS), F32)], axis=-1).reshape(depth, 1, LANES)
    pool_hist = jnp.pad(state_pool, ((0, 0), (0, 0), (POOL_HALO - POOL_STATE, 0), (0, 0))).reshape(
        depth, bs * POOL_HALO, MIX_WIDTH)
    conv_hist = jnp.pad(state_conv, ((0, 0), (0, 0), (CONV_HALO - CONV_STATE, 0), (0, 0))).reshape(
        depth, bs * CONV_HALO, MIX_WIDTH)

    n_assign = t * TOP_K
    n_blocks = n_assign // EXPERT_ROWS + N_EXPERTS

    pool_p, pool_s, conv_p, conv_s, v_out = [], [], [], [], []
    x_parts = (xp, xs)
    xn = _norm_in(xp, xs, norm_mix[0].reshape(1, D_MODEL))
    y_p = y_s = None
    for l in range(depth):
        proj = _in_proj(xn, w_in, l)
        oa, ob, oc, z_tail, a_tail, v_s, z_s = _branches(
            proj, l, bp, lp, bs, ls, pool_hist, conv_hist, wg_bf, vec(pool_scale), vec(sg_norm_g),
            vec(sg_norm_b), gate_mats, gate_bias, conv_w)
        merged = _merge(oa, ob, oc, w_branch, proj, l)
        x1 = _out_proj(merged, w_out, l, x_parts)

        xn2, eid, gates = _router(x1, norm_ffn[l].reshape(1, D_MODEL), wr[l], br[l])
        dest, sched = _routing_tables(eid[:, :TOP_K], n_blocks)
        xs_sorted = _dispatch(xn2, dest, n_blocks * EXPERT_ROWS)
        hs = _expert_up(xs_sorted, w_expert_in, l, sched)
        ys = _expert_down(hs, w_expert_out, l, sched)
        if l + 1 < depth:
            x2, xn = _combine(x1, gates, dest, ys, norm_mix[l + 1].reshape(1, D_MODEL))
            x_parts = (x2,)
        else:
            y_p, y_s = _combine(x1, gates, dest, ys, norm_final.reshape(1, D_MODEL), final_split=tp)

        a_s = proj[tp:, :MIX_WIDTH].reshape(bs, ls, MIX_WIDTH)
        pool_p.append(a_tail[:, POOL_HALO - POOL_STATE:])
        pool_s.append(jnp.concatenate([state_pool[l], a_s], axis=1)[:, -POOL_STATE:])
        conv_p.append(z_tail[:, CONV_HALO - CONV_STATE:])
        conv_s.append(z_s.reshape(bs, ls, MIX_WIDTH)[:, ls - CONV_STATE:])
        v_out.append(v_s.reshape(bs, ls, MIX_WIDTH))

    return (y_p.reshape(bp, lp, D_MODEL), y_s.reshape(bs, ls, D_MODEL), jnp.stack(pool_p), jnp.stack(pool_s),
            jnp.stack(conv_p), jnp.stack(conv_s), jnp.stack(v_out))
```

```python
import functools

import jax
import jax.numpy as jnp
from jax import lax
from jax.experimental import pallas as pl
from jax.experimental.pallas import tpu as pltpu

F32 = jnp.float32
BF16 = jnp.bfloat16

D_MODEL = 4096
MIX_WIDTH = D_MODEL // 2
POOL_GROUPS = 4
POOL_WINDOWS = (2, 4, 8, 16)
POOL_GROUP_DIM = MIX_WIDTH // POOL_GROUPS
POOL_STATE = 15
SG_HEADS = 8
SG_HEAD_DIM = MIX_WIDTH // SG_HEADS
CHUNK = 128
CONV_STATE = 2
IN_COLS = MIX_WIDTH * 6 + 3 * D_MODEL
GATE_COL0 = MIX_WIDTH * 6
N_EXPERT_GROUPS = 4
EXPERTS_PER_GROUP = 8
N_EXPERTS = N_EXPERT_GROUPS * EXPERTS_PER_GROUP
TOP_K = 2
D_EXPERT = D_MODEL // 4
PAST_LEN = 16384
EPS = 1e-6

LANES = 128
SUBLANES = 8
POOL_HALO = 16
CONV_HALO = 8
ROW_TILE = 128
EXPERT_ROWS = 256
PACKED_WIDTH = D_MODEL // 2
VMEM_LIMIT = 56 * 1024 * 1024


def _cparams(sem):
    return pltpu.CompilerParams(dimension_semantics=sem, vmem_limit_bytes=VMEM_LIMIT)


def _gelu_tanh(x):
    c = 0.7978845608028654
    return 0.5 * x * (1.0 + jnp.tanh(c * (x + 0.044715 * (x * x * x))))


def _rms(x, g):
    return x * lax.rsqrt(jnp.mean(x * x, axis=-1, keepdims=True) + EPS) * g


def _norm_in_kernel(n_prompt_tiles, xp_ref, xs_ref, g_ref, o_ref):
    i = pl.program_id(0)

    @pl.when(i < n_prompt_tiles)
    def _():
        o_ref[...] = _rms(xp_ref[...], g_ref[...]).astype(BF16)

    @pl.when(i >= n_prompt_tiles)
    def _():
        o_ref[...] = _rms(xs_ref[...], g_ref[...]).astype(BF16)


def _norm_in(xp, xs, g, tm=256):
    tp, ts = xp.shape[0], xs.shape[0]
    npt, nst = tp // tm, ts // tm
    return pl.pallas_call(
        functools.partial(_norm_in_kernel, npt),
        grid=(npt + nst,),
        in_specs=[
            pl.BlockSpec((tm, D_MODEL), lambda i: (jnp.minimum(i, npt - 1), 0)),
            pl.BlockSpec((tm, D_MODEL), lambda i: (jnp.maximum(i - npt, 0), 0)),
            pl.BlockSpec((1, D_MODEL), lambda i: (0, 0)),
        ],
        out_specs=pl.BlockSpec((tm, D_MODEL), lambda i: (i, 0)),
        out_shape=jax.ShapeDtypeStruct((tp + ts, D_MODEL), BF16),
        compiler_params=_cparams(("arbitrary",)),
        name="norm_in",
    )(xp, xs, g)


def _mm(a, w):
    return lax.dot_general(a, w, (((1,), (0,)), ((), ())), preferred_element_type=F32)


def _panel_matmul_kernel(a_ref, w_ref, o_ref):
    o_ref[...] = _mm(a_ref[...], w_ref[...])


def _in_proj(xn, w_in, layer, tm=768, tn=1024):
    t = xn.shape[0]
    return pl.pallas_call(
        _panel_matmul_kernel,
        grid=(IN_COLS // tn, t // tm),
        in_specs=[
            pl.BlockSpec((tm, D_MODEL), lambda j, i: (i, 0)),
            pl.BlockSpec((None, D_MODEL, tn), lambda j, i: (layer, 0, j)),
        ],
        out_specs=pl.BlockSpec((tm, tn), lambda j, i: (i, j)),
        out_shape=jax.ShapeDtypeStruct((t, IN_COLS), F32),
        compiler_params=_cparams(("arbitrary", "arbitrary")),
        name="in_proj",
    )(xn, w_in)


def _window_rows(ext, n_seq, halo, rows):
    if n_seq == 1:
        return ext[halo:, :]
    c = ext.shape[-1]
    return ext.reshape(n_seq, halo + rows, c)[:, halo:, :].reshape(n_seq * rows, c)


def _stack_history(hist, cur, n_seq, halo, rows):
    if n_seq == 1:
        return jnp.concatenate([hist, cur], axis=0)
    c = cur.shape[-1]
    ext = jnp.concatenate([hist.reshape(n_seq, halo, c), cur.reshape(n_seq, rows, c)], axis=1)
    return ext.reshape(n_seq * (halo + rows), c)


def _branch_math(n_seq, rows, pos, a_ref, u_ref, v_ref, ci_ref, cb_ref, cc_ref, pool_hist, conv_hist,
                 wg_ref, ps_ref, lng_ref, lnb_ref, m_ref, bias_ref, cw_ref, oa_ref, ob_ref, oc_ref):
    for g, w in enumerate(POOL_WINDOWS):
        cols = slice(g * POOL_GROUP_DIM, (g + 1) * POOL_GROUP_DIM)
        a_g = a_ref[:, cols]
        s = _stack_history(pool_hist[:, cols], a_g, n_seq, POOL_HALO, rows)
        k = 1
        while k < w:
            s = s + pltpu.roll(s, k, 0)
            k *= 2
        s = _window_rows(s, n_seq, POOL_HALO, rows)
        cnt = jnp.minimum(w, pos + 1).astype(F32)
        pooled = (s / cnt - a_g).astype(BF16)
        out = jnp.dot(pooled, wg_ref[g], preferred_element_type=F32) * ps_ref[:, cols]
        oa_ref[:, cols] = out.astype(BF16)

    vg = _gelu_tanh(v_ref[...])
    mu = jnp.mean(vg, axis=-1, keepdims=True)
    xc = vg - mu
    v = xc * lax.rsqrt(jnp.mean(xc * xc, axis=-1, keepdims=True) + EPS) * lng_ref[...] + lnb_ref[...]
    vb = v.astype(BF16)
    for h in range(SG_HEADS):
        cols = slice(h * SG_HEAD_DIM, (h + 1) * SG_HEAD_DIM)
        sp = jnp.dot(m_ref[h], vb[:, cols], preferred_element_type=F32) + bias_ref[:, cols]
        ob_ref[:, cols] = (_gelu_tanh(u_ref[:, cols]) * sp).astype(BF16)

    z = cc_ref[...] * ci_ref[...]
    e = _stack_history(conv_hist, z, n_seq, CONV_HALO, rows)
    y = cw_ref[0:1, :] * pltpu.roll(e, 2, 0)
    y = y + cw_ref[1:2, :] * pltpu.roll(e, 1, 0)
    y = y + cw_ref[2:3, :] * e
    oc_ref[...] = (cb_ref[...] * _window_rows(y, n_seq, CONV_HALO, rows)).astype(BF16)
    return v, z


def _branch_kernel(tiles_per_seq, n_prompt_tiles, seq_per_tile, sample_len,
                   a_ref, u_ref, v_ref, ci_ref, cb_ref, cc_ref, ph_ref, ch_ref, wg_ref, ps_ref, lng_ref, lnb_ref,
                   m_ref, bias_ref, cw_ref, oa_ref, ob_ref, oc_ref, zl_ref, al_ref, vo_ref, zo_ref, pool_hist, conv_hist):
    i = pl.program_id(0)
    shared = (wg_ref, ps_ref, lng_ref, lnb_ref, m_ref, bias_ref, cw_ref, oa_ref, ob_ref, oc_ref)

    @pl.when(i < n_prompt_tiles)
    def _():
        t = i % tiles_per_seq

        @pl.when(t == 0)
        def _():
            pool_hist[...] = jnp.zeros_like(pool_hist)
            conv_hist[...] = jnp.zeros_like(conv_hist)

        pos = t * ROW_TILE + lax.broadcasted_iota(jnp.int32, (ROW_TILE, 1), 0)
        _, z = _branch_math(1, ROW_TILE, pos, a_ref, u_ref, v_ref, ci_ref, cb_ref, cc_ref, pool_hist[...],
                            conv_hist[...], *shared)
        pool_hist[...] = a_ref[ROW_TILE - POOL_HALO:, :]
        conv_hist[...] = z[ROW_TILE - CONV_HALO:, :]
        zl_ref[...] = z[ROW_TILE - CONV_HALO:, :]
        al_ref[...] = a_ref[ROW_TILE - POOL_HALO:, :]

    @pl.when(i >= n_prompt_tiles)
    def _():
        r = lax.broadcasted_iota(jnp.int32, (seq_per_tile * sample_len, 1), 0)
        pos = PAST_LEN + (r % sample_len)
        v, z = _branch_math(seq_per_tile, sample_len, pos, a_ref, u_ref, v_ref, ci_ref, cb_ref, cc_ref,
                            ph_ref[...], ch_ref[...], *shared)
        vo_ref[...] = v
        zo_ref[...] = z


def _branches(proj, layer, n_prompt_seq, prompt_len, n_sample_seq, sample_len, pool_state, conv_state,
              wg_bf, pool_scale, ln_g, ln_b, gate_mats, gate_bias, conv_w):
    t_total = proj.shape[0]
    tiles_per_seq = prompt_len // ROW_TILE
    npt = n_prompt_seq * tiles_per_seq
    seq_per_tile = ROW_TILE // sample_len
    nst = n_sample_seq // seq_per_tile
    sample_rows = n_sample_seq * sample_len
    act = jax.ShapeDtypeStruct((t_total, MIX_WIDTH), BF16)
    sample_f32 = jax.ShapeDtypeStruct((sample_rows, MIX_WIDTH), F32)

    def sample_tile(i):
        return jnp.maximum(i - npt, 0)

    def kind(i):
        return jnp.where(i >= npt, 1, 0)

    slab_specs = [pl.BlockSpec((ROW_TILE, MIX_WIDTH), functools.partial(lambda c, i: (i, c), c)) for c in range(6)]
    vec_spec = pl.BlockSpec((None, 1, MIX_WIDTH), lambda i: (layer, 0, 0))
    in_specs = slab_specs + [
        pl.BlockSpec((None, seq_per_tile * POOL_HALO, MIX_WIDTH), lambda i: (layer, sample_tile(i), 0)),
        pl.BlockSpec((None, seq_per_tile * CONV_HALO, MIX_WIDTH), lambda i: (layer, sample_tile(i), 0)),
        pl.BlockSpec((None, POOL_GROUPS, POOL_GROUP_DIM, POOL_GROUP_DIM), lambda i: (layer, 0, 0, 0)),
        vec_spec, vec_spec, vec_spec,
        pl.BlockSpec((None, None, SG_HEADS, CHUNK, CHUNK), lambda i: (layer, kind(i), 0, 0, 0)),
        pl.BlockSpec((None, None, CHUNK, MIX_WIDTH), lambda i: (layer, kind(i), 0, 0)),
        pl.BlockSpec((None, 3, MIX_WIDTH), lambda i: (layer, 0, 0)),
    ]
    row_spec = pl.BlockSpec((ROW_TILE, MIX_WIDTH), lambda i: (i, 0))
    out_specs = [
        row_spec, row_spec, row_spec,
        pl.BlockSpec((None, CONV_HALO, MIX_WIDTH), lambda i: (jnp.minimum(i // tiles_per_seq, n_prompt_seq - 1), 0, 0)),
        pl.BlockSpec((None, POOL_HALO, MIX_WIDTH), lambda i: (jnp.minimum(i // tiles_per_seq, n_prompt_seq - 1), 0, 0)),
        pl.BlockSpec((ROW_TILE, MIX_WIDTH), lambda i: (sample_tile(i), 0)),
        pl.BlockSpec((ROW_TILE, MIX_WIDTH), lambda i: (sample_tile(i), 0)),
    ]
    return pl.pallas_call(
        functools.partial(_branch_kernel, tiles_per_seq, npt, seq_per_tile, sample_len),
        grid=(npt + nst,),
        in_specs=in_specs,
        out_specs=out_specs,
        out_shape=[act, act, act, jax.ShapeDtypeStruct((n_prompt_seq, CONV_HALO, MIX_WIDTH), F32),
                   jax.ShapeDtypeStruct((n_prompt_seq, POOL_HALO, MIX_WIDTH), F32), sample_f32, sample_f32],
        scratch_shapes=[pltpu.VMEM((POOL_HALO, MIX_WIDTH), F32), pltpu.VMEM((CONV_HALO, MIX_WIDTH), F32)],
        compiler_params=_cparams(("arbitrary",)),
        name="branches",
    )(proj, proj, proj, proj, proj, proj, pool_state, conv_state, wg_bf, pool_scale, ln_g, ln_b,
      gate_mats, gate_bias, conv_w)


def _merge_kernel(oa_ref, ob_ref, oc_ref, wa_ref, wb_ref, wc_ref, ga_ref, gb_ref, gc_ref, o_ref):
    m = jax.nn.sigmoid(ga_ref[...]) * _mm(oa_ref[...], wa_ref[...])
    m = m + jax.nn.sigmoid(gb_ref[...]) * _mm(ob_ref[...], wb_ref[...])
    m = m + jax.nn.sigmoid(gc_ref[...]) * _mm(oc_ref[...], wc_ref[...])
    o_ref[...] = m.astype(BF16)


def _merge(oa, ob, oc, w_branch, proj, layer, tm=768, tn=512):
    t = oa.shape[0]
    act_spec = pl.BlockSpec((tm, MIX_WIDTH), lambda j, i: (i, 0))

    def wspec(b):
        return pl.BlockSpec((None, None, MIX_WIDTH, tn), lambda j, i: (layer, b, 0, j))

    def gspec(b):
        off = (GATE_COL0 + b * D_MODEL) // tn
        return pl.BlockSpec((tm, tn), lambda j, i: (i, off + j))

    return pl.pallas_call(
        _merge_kernel,
        grid=(D_MODEL // tn, t // tm),
        in_specs=[act_spec, act_spec, act_spec, wspec(0), wspec(1), wspec(2), gspec(0), gspec(1), gspec(2)],
        out_specs=pl.BlockSpec((tm, tn), lambda j, i: (i, j)),
        out_shape=jax.ShapeDtypeStruct((t, D_MODEL), BF16),
        compiler_params=_cparams(("arbitrary", "arbitrary")),
        name="merge",
    )(oa, ob, oc, w_branch, w_branch, w_branch, proj, proj, proj)


def _out_kernel2(n_prompt_tiles, a_ref, w_ref, xp_ref, xs_ref, o_ref):
    i = pl.program_id(1)
    h = _mm(a_ref[...], w_ref[...])

    @pl.when(i < n_prompt_tiles)
    def _():
        o_ref[...] = xp_ref[...] + h

    @pl.when(i >= n_prompt_tiles)
    def _():
        o_ref[...] = xs_ref[...] + h


def _out_kernel1(a_ref, w_ref, x_ref, o_ref):
    o_ref[...] = x_ref[...] + _mm(a_ref[...], w_ref[...])


def _out_proj(merged, w_out, layer, x_parts, tm=1024, tn=512):
    t = merged.shape[0]
    common = dict(
        grid=(D_MODEL // tn, t // tm),
        out_specs=pl.BlockSpec((tm, tn), lambda j, i: (i, j)),
        out_shape=jax.ShapeDtypeStruct((t, D_MODEL), F32),
        compiler_params=_cparams(("arbitrary", "arbitrary")),
        name="out_proj",
    )
    a_spec = pl.BlockSpec((tm, D_MODEL), lambda j, i: (i, 0))
    w_spec = pl.BlockSpec((None, D_MODEL, tn), lambda j, i: (layer, 0, j))
    if len(x_parts) == 1:
        return pl.pallas_call(
            _out_kernel1,
            in_specs=[a_spec, w_spec, pl.BlockSpec((tm, tn), lambda j, i: (i, j))],
            **common,
        )(merged, w_out, x_parts[0])
    xp, xs = x_parts
    npt = xp.shape[0] // tm
    return pl.pallas_call(
        functools.partial(_out_kernel2, npt),
        in_specs=[
            a_spec, w_spec,
            pl.BlockSpec((tm, tn), lambda j, i: (jnp.minimum(i, npt - 1), j)),
            pl.BlockSpec((tm, tn), lambda j, i: (jnp.maximum(i - npt, 0), j)),
        ],
        **common,
    )(merged, w_out, xp, xs)


def _router_kernel(x_ref, g_ref, wr_ref, br_ref, xn_ref, eid_ref, gate_ref):
    xb = _rms(x_ref[...], g_ref[...]).astype(BF16)
    lo = pltpu.bitcast(xb[:, :PACKED_WIDTH].astype(F32), jnp.uint32) >> 16
    hi = pltpu.bitcast(xb[:, PACKED_WIDTH:].astype(F32), jnp.uint32) & jnp.uint32(0xFFFF0000)
    xn_ref[...] = hi | lo
    logits = jnp.dot(xb, wr_ref[...], preferred_element_type=F32) + br_ref[...]
    tm = logits.shape[0]
    lane = lax.broadcasted_iota(jnp.int32, (tm, LANES), 1)
    neg = jnp.float32(-jnp.inf)
    far = jnp.int32(LANES)

    is_g = lane < N_EXPERT_GROUPS
    glog = jnp.where(is_g, logits, neg)
    gmax = jnp.max(glog, axis=1, keepdims=True)
    gsel = jnp.min(jnp.where(glog == gmax, lane, far), axis=1, keepdims=True)
    gsum = jnp.sum(jnp.where(is_g, jnp.exp(glog - gmax), 0.0), axis=1, keepdims=True)
    gp = 1.0 / gsum

    lo = N_EXPERT_GROUPS + gsel * EXPERTS_PER_GROUP
    in_grp = jnp.logical_and(lane >= lo, lane < lo + EXPERTS_PER_GROUP)
    el = jnp.where(in_grp, logits, neg)
    m1 = jnp.max(el, axis=1, keepdims=True)
    i1 = jnp.min(jnp.where(el == m1, lane, far), axis=1, keepdims=True)
    el2 = jnp.where(lane == i1, neg, el)
    m2 = jnp.max(el2, axis=1, keepdims=True)
    i2 = jnp.min(jnp.where(el2 == m2, lane, far), axis=1, keepdims=True)
    e2 = jnp.exp(m2 - m1)
    den = 1.0 + e2
    g1 = gp * (1.0 / den)
    g2 = gp * (e2 / den)

    eid_ref[...] = jnp.where(lane == 0, i1 - N_EXPERT_GROUPS, jnp.where(lane == 1, i2 - N_EXPERT_GROUPS, 0))
    gate_ref[...] = jnp.where(lane == 0, g1, jnp.where(lane == 1, g2, 0.0))


def _router(x1, g, wr, br, tm=256):
    t = x1.shape[0]
    return pl.pallas_call(
        _router_kernel,
        grid=(t // tm,),
        in_specs=[
            pl.BlockSpec((tm, D_MODEL), lambda i: (i, 0)),
            pl.BlockSpec((1, D_MODEL), lambda i: (0, 0)),
            pl.BlockSpec((D_MODEL, LANES), lambda i: (0, 0)),
            pl.BlockSpec((1, LANES), lambda i: (0, 0)),
        ],
        out_specs=[
            pl.BlockSpec((tm, PACKED_WIDTH), lambda i: (i, 0)),
            pl.BlockSpec((tm, LANES), lambda i: (i, 0)),
            pl.BlockSpec((tm, LANES), lambda i: (i, 0)),
        ],
        out_shape=[
            jax.ShapeDtypeStruct((t, PACKED_WIDTH), jnp.uint32),
            jax.ShapeDtypeStruct((t, LANES), jnp.int32),
            jax.ShapeDtypeStruct((t, LANES), F32),
        ],
        compiler_params=_cparams(("arbitrary",)),
        name="router",
    )(x1, g, wr, br)


def _dispatch_kernel(tm, dest_ref, x_ref, xs_in, xs_hbm, sem):
    del xs_in
    base = pl.program_id(0) * tm

    def body(r, c):
        for k in range(TOP_K):
            slot_row = dest_ref[(base + r) * TOP_K + k]
            pltpu.make_async_copy(x_ref.at[pl.ds(r, 1)], xs_hbm.at[pl.ds(slot_row, 1)], sem).start()
        return c

    lax.fori_loop(0, tm, body, 0, unroll=4)
    for _ in range(TOP_K):
        pltpu.make_async_copy(x_ref, xs_hbm.at[pl.ds(0, tm)], sem).wait()


def _dispatch(xn_packed, dest_flat, cap, tm=512):
    t = xn_packed.shape[0]
    return pl.pallas_call(
        functools.partial(_dispatch_kernel, tm),
        grid_spec=pltpu.PrefetchScalarGridSpec(
            num_scalar_prefetch=1,
            grid=(t // tm,),
            in_specs=[pl.BlockSpec((tm, PACKED_WIDTH), lambda i, d: (i, 0)), pl.BlockSpec(memory_space=pl.ANY)],
            out_specs=pl.BlockSpec(memory_space=pl.ANY),
            scratch_shapes=[pltpu.SemaphoreType.DMA],
        ),
        out_shape=jax.ShapeDtypeStruct((cap, PACKED_WIDTH), jnp.uint32),
        input_output_aliases={2: 0},
        compiler_params=_cparams(("arbitrary",)),
        name="dispatch",
    )(dest_flat, xn_packed, jnp.zeros((cap, PACKED_WIDTH), jnp.uint32))


def _stream_panels(first_ref, pidx_ref, meta_ref, pass_idx, n_pass, b, n_slots, panel_copies):
    n_distinct = meta_ref[1]
    n_panels = n_pass * n_distinct
    ahead = n_slots - 1

    def start(q):
        for c in panel_copies(q // n_distinct, q % n_distinct, q % n_slots):
            c.start()

    @pl.when(jnp.logical_and(pass_idx == 0, b == 0))
    def _():
        for q in range(ahead):
            @pl.when(q < n_panels)
            def _():
                start(q)

    p = pass_idx * n_distinct + pidx_ref[b]
    slot = p % n_slots

    @pl.when(first_ref[b] == 1)
    def _():
        for c in panel_copies(pass_idx, pidx_ref[b], slot):
            c.wait()

        @pl.when(p + ahead < n_panels)
        def _():
            start(p + ahead)

    return slot


UP_SLOTS = 3
DOWN_SLOTS = 2
PANEL_PIECES = 2


def _expert_up_kernel(layer, tf, first_ref, pidx_ref, pe_ref, meta_ref, x_ref, w_hbm, o_ref, stage, sems):
    f = pl.program_id(0)
    b = pl.program_id(1)

    def panel_copies(fq, k, slot):
        e = pe_ref[k]
        col = pl.multiple_of(fq * tf, tf)
        rows = D_MODEL // PANEL_PIECES
        return [pltpu.make_async_copy(w_hbm.at[layer, e, pl.ds(piece * rows, rows), pl.ds(half * D_EXPERT + col, tf)],
                                      stage.at[slot, half, pl.ds(piece * rows, rows)], sems.at[slot])
                for half in range(2) for piece in range(PANEL_PIECES)]

    slot = _stream_panels(first_ref, pidx_ref, meta_ref, f, pl.num_programs(0), b, UP_SLOTS, panel_copies)

    @pl.when(b < meta_ref[0])
    def _():
        words = x_ref[...]
        lo = pltpu.bitcast(words << 16, F32).astype(BF16)
        hi = pltpu.bitcast(words & jnp.uint32(0xFFFF0000), F32).astype(BF16)
        h1 = _mm(lo, stage[slot, 0, :PACKED_WIDTH, :]) + _mm(hi, stage[slot, 0, PACKED_WIDTH:, :])
        h3 = _mm(lo, stage[slot, 1, :PACKED_WIDTH, :]) + _mm(hi, stage[slot, 1, PACKED_WIDTH:, :])
        o_ref[...] = (h1 * jax.nn.sigmoid(h1) * h3).astype(BF16)

    @pl.when(b >= meta_ref[0])
    def _():
        o_ref[...] = jnp.zeros_like(o_ref)


def _expert_up(xs, w_ei, layer, sched, tf=512):
    cap = xs.shape[0]
    nb = cap // EXPERT_ROWS
    nf = D_EXPERT // tf
    return pl.pallas_call(
        functools.partial(_expert_up_kernel, layer, tf),
        grid_spec=pltpu.PrefetchScalarGridSpec(
            num_scalar_prefetch=4,
            grid=(nf, nb),
            in_specs=[
                pl.BlockSpec((EXPERT_ROWS, PACKED_WIDTH), lambda f, b, fi, pi, pe, meta: (jnp.minimum(b, meta[0] - 1), 0)),
                pl.BlockSpec(memory_space=pl.ANY),
            ],
            out_specs=pl.BlockSpec((EXPERT_ROWS, tf), lambda f, b, fi, pi, pe, meta: (b, f)),
            scratch_shapes=[pltpu.VMEM((UP_SLOTS, 2, D_MODEL, tf), F32), pltpu.SemaphoreType.DMA((UP_SLOTS,))],
        ),
        out_shape=jax.ShapeDtypeStruct((cap, D_EXPERT), BF16),
        compiler_params=_cparams(("arbitrary", "arbitrary")),
        name="expert_up",
    )(*sched, xs, w_ei)


def _expert_down_kernel(layer, first_ref, pidx_ref, pe_ref, meta_ref, h_ref, w_hbm, o_ref, stage, sems):
    b = pl.program_id(0)

    def panel_copies(fq, k, slot):
        del fq
        rows = D_EXPERT // (2 * PANEL_PIECES)
        return [pltpu.make_async_copy(w_hbm.at[layer, pe_ref[k], pl.ds(piece * rows, rows)],
                                      stage.at[slot, pl.ds(piece * rows, rows)], sems.at[slot])
                for piece in range(2 * PANEL_PIECES)]

    slot = _stream_panels(first_ref, pidx_ref, meta_ref, 0, 1, b, DOWN_SLOTS, panel_copies)

    @pl.when(b < meta_ref[0])
    def _():
        o_ref[...] = _mm(h_ref[...], stage[slot])

    @pl.when(b >= meta_ref[0])
    def _():
        o_ref[...] = jnp.zeros_like(o_ref)


def _expert_down(hs, w_eo, layer, sched):
    cap = hs.shape[0]
    nb = cap // EXPERT_ROWS
    return pl.pallas_call(
        functools.partial(_expert_down_kernel, layer),
        grid_spec=pltpu.PrefetchScalarGridSpec(
            num_scalar_prefetch=4,
            grid=(nb,),
            in_specs=[
                pl.BlockSpec((EXPERT_ROWS, D_EXPERT), lambda b, fi, pi, pe, meta: (jnp.minimum(b, meta[0] - 1), 0)),
                pl.BlockSpec(memory_space=pl.ANY),
            ],
            out_specs=pl.BlockSpec((EXPERT_ROWS, D_MODEL), lambda b, fi, pi, pe, meta: (b, 0)),
            scratch_shapes=[pltpu.VMEM((DOWN_SLOTS, D_EXPERT, D_MODEL), F32), pltpu.SemaphoreType.DMA((DOWN_SLOTS,))],
        ),
        out_shape=jax.ShapeDtypeStruct((cap, D_MODEL), F32),
        compiler_params=_cparams(("arbitrary",)),
        name="expert_down",
    )(*sched, hs, w_eo)


def _gather_rows(idx_ref, idx_base, idx_stride, n_rows, src_hbm, dst, sem):
    def body(r, c):
        row = idx_ref[idx_base + r * idx_stride]
        pltpu.make_async_copy(src_hbm.at[pl.ds(row, 1)], dst.at[pl.ds(r, 1)], sem).start()
        return c

    lax.fori_loop(0, n_rows, body, 0, unroll=8)


def _wait_rows(n_rows, src_hbm, dst, sem):
    pltpu.make_async_copy(src_hbm.at[pl.ds(0, n_rows)], dst, sem).wait()


def _combine_kernel(tm, split, dest_ref, x_ref, gate_ref, g_ref, ys_hbm, *rest):
    if split is None:
        x2_ref, xn_ref, ybuf, sem = rest
    else:
        yp_ref, ysm_ref, ybuf, sem = rest
    i = pl.program_id(0)

    def issue(tile):
        slot = tile % 2
        for k in range(TOP_K):
            _gather_rows(dest_ref, tile * tm * TOP_K + k, TOP_K, tm, ys_hbm, ybuf.at[slot, k], sem.at[slot])

    @pl.when(i == 0)
    def _():
        issue(i)

    @pl.when(i + 1 < pl.num_programs(0))
    def _():
        issue(i + 1)

    slot = i % 2
    for k in range(TOP_K):
        _wait_rows(tm, ys_hbm, ybuf.at[slot, k], sem.at[slot])
    gates = gate_ref[...]
    x2 = x_ref[...] + (ybuf[slot, 0] * gates[:, 0:1] + ybuf[slot, 1] * gates[:, 1:2])
    if split is None:
        x2_ref[...] = x2
        xn_ref[...] = _rms(x2, g_ref[...]).astype(xn_ref.dtype)
    else:
        y = _rms(x2, g_ref[...])

        @pl.when(i < split)
        def _():
            yp_ref[...] = y

        @pl.when(i >= split)
        def _():
            ysm_ref[...] = y


def _combine(x1, gates, dest_flat, ys, g, final_split=None, tm=256):
    t = x1.shape[0]
    in_specs = [
        pl.BlockSpec((tm, D_MODEL), lambda i, d: (i, 0)),
        pl.BlockSpec((tm, LANES), lambda i, d: (i, 0)),
        pl.BlockSpec((1, D_MODEL), lambda i, d: (0, 0)),
        pl.BlockSpec(memory_space=pl.ANY),
    ]
    if final_split is None:
        split = None
        out_specs = [pl.BlockSpec((tm, D_MODEL), lambda i, d: (i, 0)),
                     pl.BlockSpec((tm, D_MODEL), lambda i, d: (i, 0))]
        out_shape = [jax.ShapeDtypeStruct((t, D_MODEL), F32), jax.ShapeDtypeStruct((t, D_MODEL), BF16)]
    else:
        split = final_split // tm
        out_specs = [pl.BlockSpec((tm, D_MODEL), lambda i, d: (jnp.minimum(i, split - 1), 0)),
                     pl.BlockSpec((tm, D_MODEL), lambda i, d: (jnp.maximum(i - split, 0), 0))]
        out_shape = [jax.ShapeDtypeStruct((final_split, D_MODEL), F32),
                     jax.ShapeDtypeStruct((t - final_split, D_MODEL), F32)]
    return pl.pallas_call(
        functools.partial(_combine_kernel, tm, split),
        grid_spec=pltpu.PrefetchScalarGridSpec(
            num_scalar_prefetch=1,
            grid=(t // tm,),
            in_specs=in_specs,
            out_specs=out_specs,
            scratch_shapes=[pltpu.VMEM((2, TOP_K, tm, D_MODEL), F32), pltpu.SemaphoreType.DMA((2,))],
        ),
        out_shape=out_shape,
        compiler_params=_cparams(("arbitrary",)),
        name="combine",
    )(dest_flat, x1, gates, g, ys)


def _routing_tables(eid, n_blocks):
    e_flat = eid.reshape(-1)
    onehot = (e_flat[:, None] == jnp.arange(N_EXPERTS, dtype=jnp.int32)[None, :]).astype(jnp.int32)
    csum = jnp.cumsum(onehot, axis=0)
    rank = jnp.sum(csum * onehot, axis=1) - 1
    counts = csum[-1]
    blocks = (counts + EXPERT_ROWS - 1) // EXPERT_ROWS
    blk_end = jnp.cumsum(blocks)
    blk_start = blk_end - blocks
    dest = (blk_start[e_flat] * EXPERT_ROWS + rank).astype(jnp.int32)
    n_used = blk_end[-1]

    has = blocks > 0
    order = jnp.cumsum(has.astype(jnp.int32)) - 1
    panel_expert = jnp.argsort(jnp.logical_not(has), stable=True)
    blk_ids = jnp.arange(n_blocks, dtype=jnp.int32)
    blk_e = jnp.minimum(jnp.searchsorted(blk_end, blk_ids, side="right"), N_EXPERTS - 1)
    used = blk_ids < n_used
    first = jnp.logical_and(used, blk_ids == blk_start[blk_e])
    pidx = jnp.where(used, order[blk_e], 0)
    meta = jnp.stack([n_used, jnp.sum(has.astype(jnp.int32))])
    i32 = lambda a: a.astype(jnp.int32)
    return dest, (i32(first), i32(pidx), i32(panel_expert), i32(meta))


def _gate_tables(w_spatial, b_spatial, sample_len):
    depth = w_spatial.shape[0]
    tril = jnp.tril(jnp.ones((CHUNK, CHUNK), dtype=bool))
    full = jnp.where(tril[None, None], w_spatial, 0)
    reps = CHUNK // sample_len
    small = jnp.where(tril[None, None, :sample_len, :sample_len], w_spatial[:, :, :sample_len, :sample_len], 0)
    eye = jnp.eye(reps, dtype=w_spatial.dtype)
    blockdiag = jnp.einsum("ab,lhts->lhatbs", eye, small).reshape(depth, SG_HEADS, CHUNK, CHUNK)
    mats = jnp.stack([full, blockdiag], axis=1).astype(BF16)
    bias_full = jnp.transpose(b_spatial, (0, 2, 1))
    bias_small = jnp.tile(bias_full[:, :sample_len, :], (1, reps, 1))
    bias = jnp.stack([bias_full, bias_small], axis=1)
    bias = jnp.repeat(bias, SG_HEAD_DIM, axis=-1)
    return mats, bias


def kernel(x_prompt, x_sample, state_pool, state_conv, norm_mix, w_in, w_pool_group, pool_scale, sg_norm_g, sg_norm_b, w_spatial, b_spatial, conv_w, w_branch, w_out, norm_ffn, w_router_group, b_router_group, w_router_expert, b_router_expert, w_expert_in, w_expert_out, norm_final):
    depth = w_in.shape[0]
    bp, lp, _ = x_prompt.shape
    bs, ls, _ = x_sample.shape
    tp, ts = bp * lp, bs * ls
    t = tp + ts
    assert lp % ROW_TILE == 0 and ROW_TILE % ls == 0 and bs % (ROW_TILE // ls) == 0 and ls >= CONV_STATE

    xp = x_prompt.reshape(tp, D_MODEL)
    xs = x_sample.reshape(ts, D_MODEL)

    gate_mats, gate_bias = _gate_tables(w_spatial, b_spatial, ls)
    wg_bf = w_pool_group.astype(BF16)
    vec = lambda a: a.reshape(depth, 1, -1)
    wr = jnp.concatenate([w_router_group, w_router_expert,
                          jnp.zeros((depth, D_MODEL, LANES - N_EXPERT_GROUPS - N_EXPERTS), F32)], axis=-1).astype(BF16)
    br = jnp.concatenate([b_router_group, b_router_expert,
                          jnp.zeros((depth, LANES - N_EXPERT_GROUPS - N_EXPERTS), F32)], axis=-1).reshape(depth, 1, LANES)
    pool_hist = jnp.pad(state_pool, ((0, 0), (0, 0), (POOL_HALO - POOL_STATE, 0), (0, 0))).reshape(
        depth, bs * POOL_HALO, MIX_WIDTH)
    conv_hist = jnp.pad(state_conv, ((0, 0), (0, 0), (CONV_HALO - CONV_STATE, 0), (0, 0))).reshape(
        depth, bs * CONV_HALO, MIX_WIDTH)

    n_assign = t * TOP_K
    n_blocks = n_assign // EXPERT_ROWS + N_EXPERTS

    pool_p, pool_s, conv_p, conv_s, v_out = [], [], [], [], []
    x_parts = (xp, xs)
    xn = _norm_in(xp, xs, norm_mix[0].reshape(1, D_MODEL))
    y_p = y_s = None
    for l in range(depth):
        proj = _in_proj(xn, w_in, l)
        oa, ob, oc, z_tail, a_tail, v_s, z_s = _branches(
            proj, l, bp, lp, bs, ls, pool_hist, conv_hist, wg_bf, vec(pool_scale), vec(sg_norm_g),
            vec(sg_norm_b), gate_mats, gate_bias, conv_w)
        merged = _merge(oa, ob, oc, w_branch, proj, l)
        x1 = _out_proj(merged, w_out, l, x_parts)

        xn2, eid, gates = _router(x1, norm_ffn[l].reshape(1, D_MODEL), wr[l], br[l])
        dest, sched = _routing_tables(eid[:, :TOP_K], n_blocks)
        xs_sorted = _dispatch(xn2, dest, n_blocks * EXPERT_ROWS)
        hs = _expert_up(xs_sorted, w_expert_in, l, sched)
        ys = _expert_down(hs, w_expert_out, l, sched)
        if l + 1 < depth:
            x2, xn = _combine(x1, gates, dest, ys, norm_mix[l + 1].reshape(1, D_MODEL))
            x_parts = (x2,)
        else:
            y_p, y_s = _combine(x1, gates, dest, ys, norm_final.reshape(1, D_MODEL), final_split=tp)

        a_s = proj[tp:, :MIX_WIDTH].reshape(bs, ls, MIX_WIDTH)
        pool_p.append(a_tail[:, POOL_HALO - POOL_STATE:])
        pool_s.append(jnp.concatenate([state_pool[l], a_s], axis=1)[:, -POOL_STATE:])
        conv_p.append(z_tail[:, CONV_HALO - CONV_STATE:])
        conv_s.append(z_s.reshape(bs, ls, MIX_WIDTH)[:, ls - CONV_STATE:])
        v_out.append(v_s.reshape(bs, ls, MIX_WIDTH))

    return (y_p.reshape(bp, lp, D_MODEL), y_s.reshape(bs, ls, D_MODEL), jnp.stack(pool_p), jnp.stack(pool_s),
            jnp.stack(conv_p), jnp.stack(conv_s), jnp.stack(v_out))
```

```python
import functools

import jax
import jax.numpy as jnp
from jax import lax
from jax.experimental import pallas as pl
from jax.experimental.pallas import tpu as pltpu

F32 = jnp.float32
BF16 = jnp.bfloat16

D_MODEL = 4096
MIX_WIDTH = D_MODEL // 2
POOL_GROUPS = 4
POOL_WINDOWS = (2, 4, 8, 16)
POOL_GROUP_DIM = MIX_WIDTH // POOL_GROUPS
POOL_STATE = 15
SG_HEADS = 8
SG_HEAD_DIM = MIX_WIDTH // SG_HEADS
CHUNK = 128
CONV_STATE = 2
IN_COLS = MIX_WIDTH * 6 + 3 * D_MODEL
GATE_COL0 = MIX_WIDTH * 6
N_EXPERT_GROUPS = 4
EXPERTS_PER_GROUP = 8
N_EXPERTS = N_EXPERT_GROUPS * EXPERTS_PER_GROUP
TOP_K = 2
D_EXPERT = D_MODEL // 4
PAST_LEN = 16384
EPS = 1e-6

LANES = 128
SUBLANES = 8
POOL_HALO = 16
CONV_HALO = 8
ROW_TILE = 128
EXPERT_ROWS = 256
PACKED_WIDTH = D_MODEL // 2
VMEM_LIMIT = 56 * 1024 * 1024


def _cparams(sem):
    return pltpu.CompilerParams(dimension_semantics=sem, vmem_limit_bytes=VMEM_LIMIT)


def _gelu_tanh(x):
    c = 0.7978845608028654
    return 0.5 * x * (1.0 + jnp.tanh(c * (x + 0.044715 * (x * x * x))))


def _rms(x, g):
    return x * lax.rsqrt(jnp.mean(x * x, axis=-1, keepdims=True) + EPS) * g


def _norm_in_kernel(n_prompt_tiles, xp_ref, xs_ref, g_ref, o_ref):
    i = pl.program_id(0)

    @pl.when(i < n_prompt_tiles)
    def _():
        o_ref[...] = _rms(xp_ref[...], g_ref[...]).astype(BF16)

    @pl.when(i >= n_prompt_tiles)
    def _():
        o_ref[...] = _rms(xs_ref[...], g_ref[...]).astype(BF16)


def _norm_in(xp, xs, g, tm=256):
    tp, ts = xp.shape[0], xs.shape[0]
    npt, nst = tp // tm, ts // tm
    return pl.pallas_call(
        functools.partial(_norm_in_kernel, npt),
        grid=(npt + nst,),
        in_specs=[
            pl.BlockSpec((tm, D_MODEL), lambda i: (jnp.minimum(i, npt - 1), 0)),
            pl.BlockSpec((tm, D_MODEL), lambda i: (jnp.maximum(i - npt, 0), 0)),
            pl.BlockSpec((1, D_MODEL), lambda i: (0, 0)),
        ],
        out_specs=pl.BlockSpec((tm, D_MODEL), lambda i: (i, 0)),
        out_shape=jax.ShapeDtypeStruct((tp + ts, D_MODEL), BF16),
        compiler_params=_cparams(("arbitrary",)),
        name="norm_in",
    )(xp, xs, g)


def _mm(a, w):
    return lax.dot_general(a, w, (((1,), (0,)), ((), ())), preferred_element_type=F32)


def _panel_matmul_kernel(a_ref, w_ref, o_ref):
    o_ref[...] = _mm(a_ref[...], w_ref[...])


def _in_proj(xn, w_in, layer, tm=768, tn=1024):
    t = xn.shape[0]
    return pl.pallas_call(
        _panel_matmul_kernel,
        grid=(IN_COLS // tn, t // tm),
        in_specs=[
            pl.BlockSpec((tm, D_MODEL), lambda j, i: (i, 0)),
            pl.BlockSpec((None, D_MODEL, tn), lambda j, i: (layer, 0, j)),
        ],
        out_specs=pl.BlockSpec((tm, tn), lambda j, i: (i, j)),
        out_shape=jax.ShapeDtypeStruct((t, IN_COLS), F32),
        compiler_params=_cparams(("arbitrary", "arbitrary")),
        name="in_proj",
    )(xn, w_in)


def _window_rows(ext, n_seq, halo, rows):
    if n_seq == 1:
        return ext[halo:, :]
    c = ext.shape[-1]
    return ext.reshape(n_seq, halo + rows, c)[:, halo:, :].reshape(n_seq * rows, c)


def _stack_history(hist, cur, n_seq, halo, rows):
    if n_seq == 1:
        return jnp.concatenate([hist, cur], axis=0)
    c = cur.shape[-1]
    ext = jnp.concatenate([hist.reshape(n_seq, halo, c), cur.reshape(n_seq, rows, c)], axis=1)
    return ext.reshape(n_seq * (halo + rows), c)


def _branch_math(n_seq, rows, pos, a_ref, u_ref, v_ref, ci_ref, cb_ref, cc_ref, pool_hist, conv_hist,
                 wg_ref, ps_ref, lng_ref, lnb_ref, m_ref, bias_ref, cw_ref, oa_ref, ob_ref, oc_ref):
    for g, w in enumerate(POOL_WINDOWS):
        cols = slice(g * POOL_GROUP_DIM, (g + 1) * POOL_GROUP_DIM)
        a_g = a_ref[:, cols]
        s = _stack_history(pool_hist[:, cols], a_g, n_seq, POOL_HALO, rows)
        k = 1
        while k < w:
            s = s + pltpu.roll(s, k, 0)
            k *= 2
        s = _window_rows(s, n_seq, POOL_HALO, rows)
        cnt = jnp.minimum(w, pos + 1).astype(F32)
        pooled = (s / cnt - a_g).astype(BF16)
        out = jnp.dot(pooled, wg_ref[g], preferred_element_type=F32) * ps_ref[:, cols]
        oa_ref[:, cols] = out.astype(BF16)

    vg = _gelu_tanh(v_ref[...])
    mu = jnp.mean(vg, axis=-1, keepdims=True)
    xc = vg - mu
    v = xc * lax.rsqrt(jnp.mean(xc * xc, axis=-1, keepdims=True) + EPS) * lng_ref[...] + lnb_ref[...]
    vb = v.astype(BF16)
    for h in range(SG_HEADS):
        cols = slice(h * SG_HEAD_DIM, (h + 1) * SG_HEAD_DIM)
        sp = jnp.dot(m_ref[h], vb[:, cols], preferred_element_type=F32) + bias_ref[:, cols]
        ob_ref[:, cols] = (_gelu_tanh(u_ref[:, cols]) * sp).astype(BF16)

    z = cc_ref[...] * ci_ref[...]
    e = _stack_history(conv_hist, z, n_seq, CONV_HALO, rows)
    y = cw_ref[0:1, :] * pltpu.roll(e, 2, 0)
    y = y + cw_ref[1:2, :] * pltpu.roll(e, 1, 0)
    y = y + cw_ref[2:3, :] * e
    oc_ref[...] = (cb_ref[...] * _window_rows(y, n_seq, CONV_HALO, rows)).astype(BF16)
    return v, z


def _branch_kernel(tiles_per_seq, n_prompt_tiles, seq_per_tile, sample_len,
                   a_ref, u_ref, v_ref, ci_ref, cb_ref, cc_ref, ph_ref, ch_ref, wg_ref, ps_ref, lng_ref, lnb_ref,
                   m_ref, bias_ref, cw_ref, oa_ref, ob_ref, oc_ref, zl_ref, al_ref, vo_ref, zo_ref, pool_hist, conv_hist):
    i = pl.program_id(0)
    shared = (wg_ref, ps_ref, lng_ref, lnb_ref, m_ref, bias_ref, cw_ref, oa_ref, ob_ref, oc_ref)

    @pl.when(i < n_prompt_tiles)
    def _():
        t = i % tiles_per_seq

        @pl.when(t == 0)
        def _():
            pool_hist[...] = jnp.zeros_like(pool_hist)
            conv_hist[...] = jnp.zeros_like(conv_hist)

        pos = t * ROW_TILE + lax.broadcasted_iota(jnp.int32, (ROW_TILE, 1), 0)
        _, z = _branch_math(1, ROW_TILE, pos, a_ref, u_ref, v_ref, ci_ref, cb_ref, cc_ref, pool_hist[...],
                            conv_hist[...], *shared)
        pool_hist[...] = a_ref[ROW_TILE - POOL_HALO:, :]
        conv_hist[...] = z[ROW_TILE - CONV_HALO:, :]
        zl_ref[...] = z[ROW_TILE - CONV_HALO:, :]
        al_ref[...] = a_ref[ROW_TILE - POOL_HALO:, :]

    @pl.when(i >= n_prompt_tiles)
    def _():
        r = lax.broadcasted_iota(jnp.int32, (seq_per_tile * sample_len, 1), 0)
        pos = PAST_LEN + (r % sample_len)
        v, z = _branch_math(seq_per_tile, sample_len, pos, a_ref, u_ref, v_ref, ci_ref, cb_ref, cc_ref,
                            ph_ref[...], ch_ref[...], *shared)
        vo_ref[...] = v
        zo_ref[...] = z


def _branches(proj, layer, n_prompt_seq, prompt_len, n_sample_seq, sample_len, pool_state, conv_state,
              wg_bf, pool_scale, ln_g, ln_b, gate_mats, gate_bias, conv_w):
    t_total = proj.shape[0]
    tiles_per_seq = prompt_len // ROW_TILE
    npt = n_prompt_seq * tiles_per_seq
    seq_per_tile = ROW_TILE // sample_len
    nst = n_sample_seq // seq_per_tile
    sample_rows = n_sample_seq * sample_len
    act = jax.ShapeDtypeStruct((t_total, MIX_WIDTH), BF16)
    sample_f32 = jax.ShapeDtypeStruct((sample_rows, MIX_WIDTH), F32)

    def sample_tile(i):
        return jnp.maximum(i - npt, 0)

    def kind(i):
        return jnp.where(i >= npt, 1, 0)

    slab_specs = [pl.BlockSpec((ROW_TILE, MIX_WIDTH), functools.partial(lambda c, i: (i, c), c)) for c in range(6)]
    vec_spec = pl.BlockSpec((None, 1, MIX_WIDTH), lambda i: (layer, 0, 0))
    in_specs = slab_specs + [
        pl.BlockSpec((None, seq_per_tile * POOL_HALO, MIX_WIDTH), lambda i: (layer, sample_tile(i), 0)),
        pl.BlockSpec((None, seq_per_tile * CONV_HALO, MIX_WIDTH), lambda i: (layer, sample_tile(i), 0)),
        pl.BlockSpec((None, POOL_GROUPS, POOL_GROUP_DIM, POOL_GROUP_DIM), lambda i: (layer, 0, 0, 0)),
        vec_spec, vec_spec, vec_spec,
        pl.BlockSpec((None, None, SG_HEADS, CHUNK, CHUNK), lambda i: (layer, kind(i), 0, 0, 0)),
        pl.BlockSpec((None, None, CHUNK, MIX_WIDTH), lambda i: (layer, kind(i), 0, 0)),
        pl.BlockSpec((None, 3, MIX_WIDTH), lambda i: (layer, 0, 0)),
    ]
    row_spec = pl.BlockSpec((ROW_TILE, MIX_WIDTH), lambda i: (i, 0))
    out_specs = [
        row_spec, row_spec, row_spec,
        pl.BlockSpec((None, CONV_HALO, MIX_WIDTH), lambda i: (jnp.minimum(i // tiles_per_seq, n_prompt_seq - 1), 0, 0)),
        pl.BlockSpec((None, POOL_HALO, MIX_WIDTH), lambda i: (jnp.minimum(i // tiles_per_seq, n_prompt_seq - 1), 0, 0)),
        pl.BlockSpec((ROW_TILE, MIX_WIDTH), lambda i: (sample_tile(i), 0)),
        pl.BlockSpec((ROW_TILE, MIX_WIDTH), lambda i: (sample_tile(i), 0)),
    ]
    return pl.pallas_call(
        functools.partial(_branch_kernel, tiles_per_seq, npt, seq_per_tile, sample_len),
        grid=(npt + nst,),
        in_specs=in_specs,
        out_specs=out_specs,
        out_shape=[act, act, act, jax.ShapeDtypeStruct((n_prompt_seq, CONV_HALO, MIX_WIDTH), F32),
                   jax.ShapeDtypeStruct((n_prompt_seq, POOL_HALO, MIX_WIDTH), F32), sample_f32, sample_f32],
        scratch_shapes=[pltpu.VMEM((POOL_HALO, MIX_WIDTH), F32), pltpu.VMEM((CONV_HALO, MIX_WIDTH), F32)],
        compiler_params=_cparams(("arbitrary",)),
        name="branches",
    )(proj, proj, proj, proj, proj, proj, pool_state, conv_state, wg_bf, pool_scale, ln_g, ln_b,
      gate_mats, gate_bias, conv_w)


def _merge_kernel(oa_ref, ob_ref, oc_ref, wa_ref, wb_ref, wc_ref, ga_ref, gb_ref, gc_ref, o_ref):
    m = jax.nn.sigmoid(ga_ref[...]) * _mm(oa_ref[...], wa_ref[...])
    m = m + jax.nn.sigmoid(gb_ref[...]) * _mm(ob_ref[...], wb_ref[...])
    m = m + jax.nn.sigmoid(gc_ref[...]) * _mm(oc_ref[...], wc_ref[...])
    o_ref[...] = m.astype(BF16)


def _merge(oa, ob, oc, w_branch, proj, layer, tm=768, tn=512):
    t = oa.shape[0]
    act_spec = pl.BlockSpec((tm, MIX_WIDTH), lambda j, i: (i, 0))

    def wspec(b):
        return pl.BlockSpec((None, None, MIX_WIDTH, tn), lambda j, i: (layer, b, 0, j))

    def gspec(b):
        off = (GATE_COL0 + b * D_MODEL) // tn
        return pl.BlockSpec((tm, tn), lambda j, i: (i, off + j))

    return pl.pallas_call(
        _merge_kernel,
        grid=(D_MODEL // tn, t // tm),
        in_specs=[act_spec, act_spec, act_spec, wspec(0), wspec(1), wspec(2), gspec(0), gspec(1), gspec(2)],
        out_specs=pl.BlockSpec((tm, tn), lambda j, i: (i, j)),
        out_shape=jax.ShapeDtypeStruct((t, D_MODEL), BF16),
        compiler_params=_cparams(("arbitrary", "arbitrary")),
        name="merge",
    )(oa, ob, oc, w_branch, w_branch, w_branch, proj, proj, proj)


def _out_kernel2(n_prompt_tiles, a_ref, w_ref, xp_ref, xs_ref, o_ref):
    i = pl.program_id(1)
    h = _mm(a_ref[...], w_ref[...])

    @pl.when(i < n_prompt_tiles)
    def _():
        o_ref[...] = xp_ref[...] + h

    @pl.when(i >= n_prompt_tiles)
    def _():
        o_ref[...] = xs_ref[...] + h


def _out_kernel1(a_ref, w_ref, x_ref, o_ref):
    o_ref[...] = x_ref[...] + _mm(a_ref[...], w_ref[...])


def _out_proj(merged, w_out, layer, x_parts, tm=1024, tn=512):
    t = merged.shape[0]
    common = dict(
        grid=(D_MODEL // tn, t // tm),
        out_specs=pl.BlockSpec((tm, tn), lambda j, i: (i, j)),
        out_shape=jax.ShapeDtypeStruct((t, D_MODEL), F32),
        compiler_params=_cparams(("arbitrary", "arbitrary")),
        name="out_proj",
    )
    a_spec = pl.BlockSpec((tm, D_MODEL), lambda j, i: (i, 0))
    w_spec = pl.BlockSpec((None, D_MODEL, tn), lambda j, i: (layer, 0, j))
    if len(x_parts) == 1:
        return pl.pallas_call(
            _out_kernel1,
            in_specs=[a_spec, w_spec, pl.BlockSpec((tm, tn), lambda j, i: (i, j))],
            **common,
        )(merged, w_out, x_parts[0])
    xp, xs = x_parts
    npt = xp.shape[0] // tm
    return pl.pallas_call(
        functools.partial(_out_kernel2, npt),
        in_specs=[
            a_spec, w_spec,
            pl.BlockSpec((tm, tn), lambda j, i: (jnp.minimum(i, npt - 1), j)),
            pl.BlockSpec((tm, tn), lambda j, i: (jnp.maximum(i - npt, 0), j)),
        ],
        **common,
    )(merged, w_out, xp, xs)


def _router_kernel(x_ref, g_ref, wr_ref, br_ref, xn_ref, eid_ref, gate_ref):
    xb = _rms(x_ref[...], g_ref[...]).astype(BF16)
    lo = pltpu.bitcast(xb[:, :PACKED_WIDTH].astype(F32), jnp.uint32) >> 16
    hi = pltpu.bitcast(xb[:, PACKED_WIDTH:].astype(F32), jnp.uint32) & jnp.uint32(0xFFFF0000)
    xn_ref[...] = hi | lo
    logits = jnp.dot(xb, wr_ref[...], preferred_element_type=F32) + br_ref[...]
    tm = logits.shape[0]
    lane = lax.broadcasted_iota(jnp.int32, (tm, LANES), 1)
    neg = jnp.float32(-jnp.inf)
    far = jnp.int32(LANES)

    is_g = lane < N_EXPERT_GROUPS
    glog = jnp.where(is_g, logits, neg)
    gmax = jnp.max(glog, axis=1, keepdims=True)
    gsel = jnp.min(jnp.where(glog == gmax, lane, far), axis=1, keepdims=True)
    gsum = jnp.sum(jnp.where(is_g, jnp.exp(glog - gmax), 0.0), axis=1, keepdims=True)
    gp = 1.0 / gsum

    lo = N_EXPERT_GROUPS + gsel * EXPERTS_PER_GROUP
    in_grp = jnp.logical_and(lane >= lo, lane < lo + EXPERTS_PER_GROUP)
    el = jnp.where(in_grp, logits, neg)
    m1 = jnp.max(el, axis=1, keepdims=True)
    i1 = jnp.min(jnp.where(el == m1, lane, far), axis=1, keepdims=True)
    el2 = jnp.where(lane == i1, neg, el)
    m2 = jnp.max(el2, axis=1, keepdims=True)
    i2 = jnp.min(jnp.where(el2 == m2, lane, far), axis=1, keepdims=True)
    e2 = jnp.exp(m2 - m1)
    den = 1.0 + e2
    g1 = gp * (1.0 / den)
    g2 = gp * (e2 / den)

    eid_ref[...] = jnp.where(lane == 0, i1 - N_EXPERT_GROUPS, jnp.where(lane == 1, i2 - N_EXPERT_GROUPS, 0))
    gate_ref[...] = jnp.where(lane == 0, g1, jnp.where(lane == 1, g2, 0.0))


def _router(x1, g, wr, br, tm=256):
    t = x1.shape[0]
    return pl.pallas_call(
        _router_kernel,
        grid=(t // tm,),
        in_specs=[
            pl.BlockSpec((tm, D_MODEL), lambda i: (i, 0)),
            pl.BlockSpec((1, D_MODEL), lambda i: (0, 0)),
            pl.BlockSpec((D_MODEL, LANES), lambda i: (0, 0)),
            pl.BlockSpec((1, LANES), lambda i: (0, 0)),
        ],
        out_specs=[
            pl.BlockSpec((tm, PACKED_WIDTH), lambda i: (i, 0)),
            pl.BlockSpec((tm, LANES), lambda i: (i, 0)),
            pl.BlockSpec((tm, LANES), lambda i: (i, 0)),
        ],
        out_shape=[
            jax.ShapeDtypeStruct((t, PACKED_WIDTH), jnp.uint32),
            jax.ShapeDtypeStruct((t, LANES), jnp.int32),
            jax.ShapeDtypeStruct((t, LANES), F32),
        ],
        compiler_params=_cparams(("arbitrary",)),
        name="router",
    )(x1, g, wr, br)


def _dispatch_kernel(tm, dest_ref, x_ref, xs_in, xs_hbm, sem):
    del xs_in
    base = pl.program_id(0) * tm

    def body(r, c):
        for k in range(TOP_K):
            slot_row = dest_ref[(base + r) * TOP_K + k]
            pltpu.make_async_copy(x_ref.at[pl.ds(r, 1)], xs_hbm.at[pl.ds(slot_row, 1)], sem).start()
        return c

    lax.fori_loop(0, tm, body, 0, unroll=4)
    for _ in range(TOP_K):
        pltpu.make_async_copy(x_ref, xs_hbm.at[pl.ds(0, tm)], sem).wait()


def _dispatch(xn_packed, dest_flat, xs_init, tm=512):
    t = xn_packed.shape[0]
    cap = xs_init.shape[0]
    return pl.pallas_call(
        functools.partial(_dispatch_kernel, tm),
        grid_spec=pltpu.PrefetchScalarGridSpec(
            num_scalar_prefetch=1,
            grid=(t // tm,),
            in_specs=[pl.BlockSpec((tm, PACKED_WIDTH), lambda i, d: (i, 0)), pl.BlockSpec(memory_space=pl.ANY)],
            out_specs=pl.BlockSpec(memory_space=pl.ANY),
            scratch_shapes=[pltpu.SemaphoreType.DMA],
        ),
        out_shape=jax.ShapeDtypeStruct((cap, PACKED_WIDTH), jnp.uint32),
        input_output_aliases={2: 0},
        compiler_params=_cparams(("arbitrary",)),
        name="dispatch",
    )(dest_flat, xn_packed, xs_init)


def _stream_panels(first_ref, pidx_ref, meta_ref, pass_idx, n_pass, b, n_slots, panel_copies):
    n_distinct = meta_ref[1]
    n_panels = n_pass * n_distinct
    ahead = n_slots - 1

    def start(q):
        for c in panel_copies(q // n_distinct, q % n_distinct, q % n_slots):
            c.start()

    @pl.when(jnp.logical_and(pass_idx == 0, b == 0))
    def _():
        for q in range(ahead):
            @pl.when(q < n_panels)
            def _():
                start(q)

    p = pass_idx * n_distinct + pidx_ref[b]
    slot = p % n_slots

    @pl.when(first_ref[b] == 1)
    def _():
        for c in panel_copies(pass_idx, pidx_ref[b], slot):
            c.wait()

        @pl.when(p + ahead < n_panels)
        def _():
            start(p + ahead)

    return slot


UP_SLOTS = 3
DOWN_SLOTS = 2


def _expert_up_kernel(layer, tf, first_ref, pidx_ref, pe_ref, meta_ref, x_ref, w_hbm, o_ref, stage, sems):
    f = pl.program_id(0)
    b = pl.program_id(1)

    def panel_copies(fq, k, slot):
        e = pe_ref[k]
        col = pl.multiple_of(fq * tf, tf)
        return [pltpu.make_async_copy(w_hbm.at[layer, e, :, pl.ds(half * D_EXPERT + col, tf)],
                                      stage.at[slot, half], sems.at[slot]) for half in range(2)]

    slot = _stream_panels(first_ref, pidx_ref, meta_ref, f, pl.num_programs(0), b, UP_SLOTS, panel_copies)

    @pl.when(b < meta_ref[0])
    def _():
        words = x_ref[...]
        lo = pltpu.bitcast(words << 16, F32).astype(BF16)
        hi = pltpu.bitcast(words & jnp.uint32(0xFFFF0000), F32).astype(BF16)
        h1 = _mm(lo, stage[slot, 0, :PACKED_WIDTH, :]) + _mm(hi, stage[slot, 0, PACKED_WIDTH:, :])
        h3 = _mm(lo, stage[slot, 1, :PACKED_WIDTH, :]) + _mm(hi, stage[slot, 1, PACKED_WIDTH:, :])
        o_ref[...] = (h1 * jax.nn.sigmoid(h1) * h3).astype(BF16)

    @pl.when(b >= meta_ref[0])
    def _():
        o_ref[...] = jnp.zeros_like(o_ref)


def _expert_up(xs, w_ei, layer, sched, tf=512):
    cap = xs.shape[0]
    nb = cap // EXPERT_ROWS
    nf = D_EXPERT // tf
    return pl.pallas_call(
        functools.partial(_expert_up_kernel, layer, tf),
        grid_spec=pltpu.PrefetchScalarGridSpec(
            num_scalar_prefetch=4,
            grid=(nf, nb),
            in_specs=[
                pl.BlockSpec((EXPERT_ROWS, PACKED_WIDTH), lambda f, b, fi, pi, pe, meta: (jnp.minimum(b, meta[0] - 1), 0)),
                pl.BlockSpec(memory_space=pl.ANY),
            ],
            out_specs=pl.BlockSpec((EXPERT_ROWS, tf), lambda f, b, fi, pi, pe, meta: (b, f)),
            scratch_shapes=[pltpu.VMEM((UP_SLOTS, 2, D_MODEL, tf), F32), pltpu.SemaphoreType.DMA((UP_SLOTS,))],
        ),
        out_shape=jax.ShapeDtypeStruct((cap, D_EXPERT), BF16),
        compiler_params=_cparams(("arbitrary", "arbitrary")),
        name="expert_up",
    )(*sched, xs, w_ei)


def _expert_down_kernel(layer, first_ref, pidx_ref, pe_ref, meta_ref, h_ref, w_hbm, o_ref, stage, sems):
    b = pl.program_id(0)

    def panel_copies(fq, k, slot):
        del fq
        return [pltpu.make_async_copy(w_hbm.at[layer, pe_ref[k]], stage.at[slot], sems.at[slot])]

    slot = _stream_panels(first_ref, pidx_ref, meta_ref, 0, 1, b, DOWN_SLOTS, panel_copies)

    @pl.when(b < meta_ref[0])
    def _():
        o_ref[...] = _mm(h_ref[...], stage[slot])

    @pl.when(b >= meta_ref[0])
    def _():
        o_ref[...] = jnp.zeros_like(o_ref)


def _expert_down(hs, w_eo, layer, sched):
    cap = hs.shape[0]
    nb = cap // EXPERT_ROWS
    return pl.pallas_call(
        functools.partial(_expert_down_kernel, layer),
        grid_spec=pltpu.PrefetchScalarGridSpec(
            num_scalar_prefetch=4,
            grid=(nb,),
            in_specs=[
                pl.BlockSpec((EXPERT_ROWS, D_EXPERT), lambda b, fi, pi, pe, meta: (jnp.minimum(b, meta[0] - 1), 0)),
                pl.BlockSpec(memory_space=pl.ANY),
            ],
            out_specs=pl.BlockSpec((EXPERT_ROWS, D_MODEL), lambda b, fi, pi, pe, meta: (b, 0)),
            scratch_shapes=[pltpu.VMEM((DOWN_SLOTS, D_EXPERT, D_MODEL), F32), pltpu.SemaphoreType.DMA((DOWN_SLOTS,))],
        ),
        out_shape=jax.ShapeDtypeStruct((cap, D_MODEL), F32),
        compiler_params=_cparams(("arbitrary",)),
        name="expert_down",
    )(*sched, hs, w_eo)


def _gather_rows(idx_ref, idx_base, idx_stride, n_rows, src_hbm, dst, sem):
    def body(r, c):
        row = idx_ref[idx_base + r * idx_stride]
        pltpu.make_async_copy(src_hbm.at[pl.ds(row, 1)], dst.at[pl.ds(r, 1)], sem).start()
        return c

    lax.fori_loop(0, n_rows, body, 0, unroll=8)


def _wait_rows(n_rows, src_hbm, dst, sem):
    pltpu.make_async_copy(src_hbm.at[pl.ds(0, n_rows)], dst, sem).wait()


def _combine_kernel(tm, split, dest_ref, x_ref, gate_ref, g_ref, ys_hbm, *rest):
    if split is None:
        x2_ref, xn_ref, ybuf, sem = rest
    else:
        yp_ref, ysm_ref, ybuf, sem = rest
    i = pl.program_id(0)

    def issue(tile):
        slot = tile % 2
        for k in range(TOP_K):
            _gather_rows(dest_ref, tile * tm * TOP_K + k, TOP_K, tm, ys_hbm, ybuf.at[slot, k], sem.at[slot])

    @pl.when(i == 0)
    def _():
        issue(i)

    @pl.when(i + 1 < pl.num_programs(0))
    def _():
        issue(i + 1)

    slot = i % 2
    for k in range(TOP_K):
        _wait_rows(tm, ys_hbm, ybuf.at[slot, k], sem.at[slot])
    gates = gate_ref[...]
    x2 = x_ref[...] + (ybuf[slot, 0] * gates[:, 0:1] + ybuf[slot, 1] * gates[:, 1:2])
    if split is None:
        x2_ref[...] = x2
        xn_ref[...] = _rms(x2, g_ref[...]).astype(xn_ref.dtype)
    else:
        y = _rms(x2, g_ref[...])

        @pl.when(i < split)
        def _():
            yp_ref[...] = y

        @pl.when(i >= split)
        def _():
            ysm_ref[...] = y


def _combine(x1, gates, dest_flat, ys, g, final_split=None, tm=256):
    t = x1.shape[0]
    in_specs = [
        pl.BlockSpec((tm, D_MODEL), lambda i, d: (i, 0)),
        pl.BlockSpec((tm, LANES), lambda i, d: (i, 0)),
        pl.BlockSpec((1, D_MODEL), lambda i, d: (0, 0)),
        pl.BlockSpec(memory_space=pl.ANY),
    ]
    if final_split is None:
        split = None
        out_specs = [pl.BlockSpec((tm, D_MODEL), lambda i, d: (i, 0)),
                     pl.BlockSpec((tm, D_MODEL), lambda i, d: (i, 0))]
        out_shape = [jax.ShapeDtypeStruct((t, D_MODEL), F32), jax.ShapeDtypeStruct((t, D_MODEL), BF16)]
    else:
        split = final_split // tm
        out_specs = [pl.BlockSpec((tm, D_MODEL), lambda i, d: (jnp.minimum(i, split - 1), 0)),
                     pl.BlockSpec((tm, D_MODEL), lambda i, d: (jnp.maximum(i - split, 0), 0))]
        out_shape = [jax.ShapeDtypeStruct((final_split, D_MODEL), F32),
                     jax.ShapeDtypeStruct((t - final_split, D_MODEL), F32)]
    return pl.pallas_call(
        functools.partial(_combine_kernel, tm, split),
        grid_spec=pltpu.PrefetchScalarGridSpec(
            num_scalar_prefetch=1,
            grid=(t // tm,),
            in_specs=in_specs,
            out_specs=out_specs,
            scratch_shapes=[pltpu.VMEM((2, TOP_K, tm, D_MODEL), F32), pltpu.SemaphoreType.DMA((2,))],
        ),
        out_shape=out_shape,
        compiler_params=_cparams(("arbitrary",)),
        name="combine",
    )(dest_flat, x1, gates, g, ys)


def _routing_tables(eid, n_blocks):
    e_flat = eid.reshape(-1)
    onehot = (e_flat[:, None] == jnp.arange(N_EXPERTS, dtype=jnp.int32)[None, :]).astype(jnp.int32)
    csum = jnp.cumsum(onehot, axis=0)
    rank = jnp.sum(csum * onehot, axis=1) - 1
    counts = csum[-1]
    blocks = (counts + EXPERT_ROWS - 1) // EXPERT_ROWS
    blk_end = jnp.cumsum(blocks)
    blk_start = blk_end - blocks
    dest = (blk_start[e_flat] * EXPERT_ROWS + rank).astype(jnp.int32)
    n_used = blk_end[-1]

    has = blocks > 0
    order = jnp.cumsum(has.astype(jnp.int32)) - 1
    panel_expert = jnp.argsort(jnp.logical_not(has), stable=True)
    blk_ids = jnp.arange(n_blocks, dtype=jnp.int32)
    blk_e = jnp.minimum(jnp.searchsorted(blk_end, blk_ids, side="right"), N_EXPERTS - 1)
    used = blk_ids < n_used
    first = jnp.logical_and(used, blk_ids == blk_start[blk_e])
    pidx = jnp.where(used, order[blk_e], 0)
    meta = jnp.stack([n_used, jnp.sum(has.astype(jnp.int32))])
    i32 = lambda a: a.astype(jnp.int32)
    return dest, (i32(first), i32(pidx), i32(panel_expert), i32(meta))


def _gate_tables(w_spatial, b_spatial, sample_len):
    depth = w_spatial.shape[0]
    tril = jnp.tril(jnp.ones((CHUNK, CHUNK), dtype=bool))
    full = jnp.where(tril[None, None], w_spatial, 0)
    reps = CHUNK // sample_len
    small = jnp.where(tril[None, None, :sample_len, :sample_len], w_spatial[:, :, :sample_len, :sample_len], 0)
    eye = jnp.eye(reps, dtype=w_spatial.dtype)
    blockdiag = jnp.einsum("ab,lhts->lhatbs", eye, small).reshape(depth, SG_HEADS, CHUNK, CHUNK)
    mats = jnp.stack([full, blockdiag], axis=1).astype(BF16)
    bias_full = jnp.transpose(b_spatial, (0, 2, 1))
    bias_small = jnp.tile(bias_full[:, :sample_len, :], (1, reps, 1))
    bias = jnp.stack([bias_full, bias_small], axis=1)
    bias = jnp.repeat(bias, SG_HEAD_DIM, axis=-1)
    return mats, bias


def kernel(x_prompt, x_sample, state_pool, state_conv, norm_mix, w_in, w_pool_group, pool_scale, sg_norm_g, sg_norm_b, w_spatial, b_spatial, conv_w, w_branch, w_out, norm_ffn, w_router_group, b_router_group, w_router_expert, b_router_expert, w_expert_in, w_expert_out, norm_final):
    depth = w_in.shape[0]
    bp, lp, _ = x_prompt.shape
    bs, ls, _ = x_sample.shape
    tp, ts = bp * lp, bs * ls
    t = tp + ts
    assert lp % ROW_TILE == 0 and ROW_TILE % ls == 0 and bs % (ROW_TILE // ls) == 0 and ls >= CONV_STATE

    xp = x_prompt.reshape(tp, D_MODEL)
    xs = x_sample.reshape(ts, D_MODEL)

    gate_mats, gate_bias = _gate_tables(w_spatial, b_spatial, ls)
    wg_bf = w_pool_group.astype(BF16)
    vec = lambda a: a.reshape(depth, 1, -1)
    wr = jnp.concatenate([w_router_group, w_router_expert,
                          jnp.zeros((depth, D_MODEL, LANES - N_EXPERT_GROUPS - N_EXPERTS), F32)], axis=-1).astype(BF16)
    br = jnp.concatenate([b_router_group, b_router_expert,
                          jnp.zeros((depth, LANES - N_EXPERT_GROUPS - N_EXPERTS), F32)], axis=-1).reshape(depth, 1, LANES)
    pool_hist = jnp.pad(state_pool, ((0, 0), (0, 0), (POOL_HALO - POOL_STATE, 0), (0, 0))).reshape(
        depth, bs * POOL_HALO, MIX_WIDTH)
    conv_hist = jnp.pad(state_conv, ((0, 0), (0, 0), (CONV_HALO - CONV_STATE, 0), (0, 0))).reshape(
        depth, bs * CONV_HALO, MIX_WIDTH)

    n_assign = t * TOP_K
    n_blocks = n_assign // EXPERT_ROWS + N_EXPERTS

    pool_p, pool_s, conv_p, conv_s, v_out = [], [], [], [], []
    xs_sorted = jnp.zeros((n_blocks * EXPERT_ROWS, PACKED_WIDTH), jnp.uint32)
    x_parts = (xp, xs)
    xn = _norm_in(xp, xs, norm_mix[0].reshape(1, D_MODEL))
    y_p = y_s = None
    for l in range(depth):
        proj = _in_proj(xn, w_in, l)
        oa, ob, oc, z_tail, a_tail, v_s, z_s = _branches(
            proj, l, bp, lp, bs, ls, pool_hist, conv_hist, wg_bf, vec(pool_scale), vec(sg_norm_g),
            vec(sg_norm_b), gate_mats, gate_bias, conv_w)
        merged = _merge(oa, ob, oc, w_branch, proj, l)
        x1 = _out_proj(merged, w_out, l, x_parts)

        xn2, eid, gates = _router(x1, norm_ffn[l].reshape(1, D_MODEL), wr[l], br[l])
        dest, sched = _routing_tables(eid[:, :TOP_K], n_blocks)
        xs_sorted = _dispatch(xn2, dest, xs_sorted)
        hs = _expert_up(xs_sorted, w_expert_in, l, sched)
        ys = _expert_down(hs, w_expert_out, l, sched)
        if l + 1 < depth:
            x2, xn = _combine(x1, gates, dest, ys, norm_mix[l + 1].reshape(1, D_MODEL))
            x_parts = (x2,)
        else:
            y_p, y_s = _combine(x1, gates, dest, ys, norm_final.reshape(1, D_MODEL), final_split=tp)

        a_s = proj[tp:, :MIX_WIDTH].reshape(bs, ls, MIX_WIDTH)
        pool_p.append(a_tail[:, POOL_HALO - POOL_STATE:])
        pool_s.append(jnp.concatenate([state_pool[l], a_s], axis=1)[:, -POOL_STATE:])
        conv_p.append(z_tail[:, CONV_HALO - CONV_STATE:])
        conv_s.append(z_s.reshape(bs, ls, MIX_WIDTH)[:, ls - CONV_STATE:])
        v_out.append(v_s.reshape(bs, ls, MIX_WIDTH))

    return (y_p.reshape(bp, lp, D_MODEL), y_s.reshape(bs, ls, D_MODEL), jnp.stack(pool_p), jnp.stack(pool_s),
            jnp.stack(conv_p), jnp.stack(conv_s), jnp.stack(v_out))
```

```python
import functools

import jax
import jax.numpy as jnp
from jax import lax
from jax.experimental import pallas as pl
from jax.experimental.pallas import tpu as pltpu

F32 = jnp.float32
BF16 = jnp.bfloat16

D_MODEL = 4096
MIX_WIDTH = D_MODEL // 2
POOL_GROUPS = 4
POOL_WINDOWS = (2, 4, 8, 16)
POOL_GROUP_DIM = MIX_WIDTH // POOL_GROUPS
POOL_STATE = 15
SG_HEADS = 8
SG_HEAD_DIM = MIX_WIDTH // SG_HEADS
CHUNK = 128
CONV_STATE = 2
IN_COLS = MIX_WIDTH * 6 + 3 * D_MODEL
GATE_COL0 = MIX_WIDTH * 6
N_EXPERT_GROUPS = 4
EXPERTS_PER_GROUP = 8
N_EXPERTS = N_EXPERT_GROUPS * EXPERTS_PER_GROUP
TOP_K = 2
D_EXPERT = D_MODEL // 4
PAST_LEN = 16384
EPS = 1e-6

LANES = 128
SUBLANES = 8
POOL_HALO = 16
CONV_HALO = 8
ROW_TILE = 128
EXPERT_ROWS = 256
PACKED_WIDTH = D_MODEL // 2
VMEM_LIMIT = 56 * 1024 * 1024


def _cparams(sem):
    return pltpu.CompilerParams(dimension_semantics=sem, vmem_limit_bytes=VMEM_LIMIT)


def _gelu_tanh(x):
    c = 0.7978845608028654
    return 0.5 * x * (1.0 + jnp.tanh(c * (x + 0.044715 * (x * x * x))))


def _rms(x, g):
    return x * lax.rsqrt(jnp.mean(x * x, axis=-1, keepdims=True) + EPS) * g


def _norm_in_kernel(n_prompt_tiles, xp_ref, xs_ref, g_ref, o_ref):
    i = pl.program_id(0)

    @pl.when(i < n_prompt_tiles)
    def _():
        o_ref[...] = _rms(xp_ref[...], g_ref[...]).astype(BF16)

    @pl.when(i >= n_prompt_tiles)
    def _():
        o_ref[...] = _rms(xs_ref[...], g_ref[...]).astype(BF16)


def _norm_in(xp, xs, g, tm=256):
    tp, ts = xp.shape[0], xs.shape[0]
    npt, nst = tp // tm, ts // tm
    return pl.pallas_call(
        functools.partial(_norm_in_kernel, npt),
        grid=(npt + nst,),
        in_specs=[
            pl.BlockSpec((tm, D_MODEL), lambda i: (jnp.minimum(i, npt - 1), 0)),
            pl.BlockSpec((tm, D_MODEL), lambda i: (jnp.maximum(i - npt, 0), 0)),
            pl.BlockSpec((1, D_MODEL), lambda i: (0, 0)),
        ],
        out_specs=pl.BlockSpec((tm, D_MODEL), lambda i: (i, 0)),
        out_shape=jax.ShapeDtypeStruct((tp + ts, D_MODEL), BF16),
        compiler_params=_cparams(("arbitrary",)),
        name="norm_in",
    )(xp, xs, g)


def _mm(a, w):
    return lax.dot_general(a, w, (((1,), (0,)), ((), ())), preferred_element_type=F32)


def _panel_matmul_kernel(a_ref, w_ref, o_ref):
    o_ref[...] = _mm(a_ref[...], w_ref[...])


def _in_proj(xn, w_in, layer, tm=768, tn=1024):
    t = xn.shape[0]
    return pl.pallas_call(
        _panel_matmul_kernel,
        grid=(IN_COLS // tn, t // tm),
        in_specs=[
            pl.BlockSpec((tm, D_MODEL), lambda j, i: (i, 0)),
            pl.BlockSpec((None, D_MODEL, tn), lambda j, i: (layer, 0, j)),
        ],
        out_specs=pl.BlockSpec((tm, tn), lambda j, i: (i, j)),
        out_shape=jax.ShapeDtypeStruct((t, IN_COLS), F32),
        compiler_params=_cparams(("arbitrary", "arbitrary")),
        name="in_proj",
    )(xn, w_in)


def _window_rows(ext, n_seq, halo, rows):
    if n_seq == 1:
        return ext[halo:, :]
    c = ext.shape[-1]
    return ext.reshape(n_seq, halo + rows, c)[:, halo:, :].reshape(n_seq * rows, c)


def _stack_history(hist, cur, n_seq, halo, rows):
    if n_seq == 1:
        return jnp.concatenate([hist, cur], axis=0)
    c = cur.shape[-1]
    ext = jnp.concatenate([hist.reshape(n_seq, halo, c), cur.reshape(n_seq, rows, c)], axis=1)
    return ext.reshape(n_seq * (halo + rows), c)


def _branch_math(n_seq, rows, pos, a_ref, u_ref, v_ref, ci_ref, cb_ref, cc_ref, pool_hist, conv_hist,
                 wg_ref, ps_ref, lng_ref, lnb_ref, m_ref, bias_ref, cw_ref, oa_ref, ob_ref, oc_ref):
    for g, w in enumerate(POOL_WINDOWS):
        cols = slice(g * POOL_GROUP_DIM, (g + 1) * POOL_GROUP_DIM)
        a_g = a_ref[:, cols]
        s = _stack_history(pool_hist[:, cols], a_g, n_seq, POOL_HALO, rows)
        k = 1
        while k < w:
            s = s + pltpu.roll(s, k, 0)
            k *= 2
        s = _window_rows(s, n_seq, POOL_HALO, rows)
        cnt = jnp.minimum(w, pos + 1).astype(F32)
        pooled = (s / cnt - a_g).astype(BF16)
        out = jnp.dot(pooled, wg_ref[g], preferred_element_type=F32) * ps_ref[:, cols]
        oa_ref[:, cols] = out.astype(BF16)

    vg = _gelu_tanh(v_ref[...])
    mu = jnp.mean(vg, axis=-1, keepdims=True)
    xc = vg - mu
    v = xc * lax.rsqrt(jnp.mean(xc * xc, axis=-1, keepdims=True) + EPS) * lng_ref[...] + lnb_ref[...]
    vb = v.astype(BF16)
    for h in range(SG_HEADS):
        cols = slice(h * SG_HEAD_DIM, (h + 1) * SG_HEAD_DIM)
        sp = jnp.dot(m_ref[h], vb[:, cols], preferred_element_type=F32) + bias_ref[:, cols]
        ob_ref[:, cols] = (_gelu_tanh(u_ref[:, cols]) * sp).astype(BF16)

    z = cc_ref[...] * ci_ref[...]
    e = _stack_history(conv_hist, z, n_seq, CONV_HALO, rows)
    y = cw_ref[0:1, :] * pltpu.roll(e, 2, 0)
    y = y + cw_ref[1:2, :] * pltpu.roll(e, 1, 0)
    y = y + cw_ref[2:3, :] * e
    oc_ref[...] = (cb_ref[...] * _window_rows(y, n_seq, CONV_HALO, rows)).astype(BF16)
    return v, z


SLAB_RING = 3


def _branch_kernel(tiles_per_seq, n_prompt_tiles, seq_per_tile, sample_len,
                   proj_hbm, ph_ref, ch_ref, wg_ref, ps_ref, lng_ref, lnb_ref,
                   m_ref, bias_ref, cw_ref, oa_ref, ob_ref, oc_ref, zl_ref, al_ref, vo_ref, zo_ref,
                   pool_hist, conv_hist, slabs, slab_sems):
    i = pl.program_id(0)
    n_tiles = pl.num_programs(0)
    shared = (wg_ref, ps_ref, lng_ref, lnb_ref, m_ref, bias_ref, cw_ref, oa_ref, ob_ref, oc_ref)

    def slab_copies(tile):
        slot = tile % SLAB_RING
        first = tile * ROW_TILE
        rows = pl.ds(first if isinstance(first, int) else pl.multiple_of(first, ROW_TILE), ROW_TILE)
        return [pltpu.make_async_copy(proj_hbm.at[rows, pl.ds(c * MIX_WIDTH, MIX_WIDTH)], slabs.at[slot, c],
                                      slab_sems.at[slot]) for c in range(6)]

    @pl.when(i == 0)
    def _():
        for tile in range(SLAB_RING - 1):
            for c in slab_copies(tile):
                c.start()

    @pl.when(i + SLAB_RING - 1 < n_tiles)
    def _():
        for c in slab_copies(i + SLAB_RING - 1):
            c.start()

    for c in slab_copies(i):
        c.wait()
    cur = i % SLAB_RING
    a_ref, u_ref, v_ref, ci_ref, cb_ref, cc_ref = (slabs.at[cur, c] for c in range(6))

    @pl.when(i < n_prompt_tiles)
    def _():
        t = i % tiles_per_seq

        @pl.when(t == 0)
        def _():
            pool_hist[...] = jnp.zeros_like(pool_hist)
            conv_hist[...] = jnp.zeros_like(conv_hist)

        pos = t * ROW_TILE + lax.broadcasted_iota(jnp.int32, (ROW_TILE, 1), 0)
        _, z = _branch_math(1, ROW_TILE, pos, a_ref, u_ref, v_ref, ci_ref, cb_ref, cc_ref, pool_hist[...],
                            conv_hist[...], *shared)
        pool_hist[...] = a_ref[ROW_TILE - POOL_HALO:, :]
        conv_hist[...] = z[ROW_TILE - CONV_HALO:, :]
        zl_ref[...] = z[ROW_TILE - CONV_HALO:, :]
        al_ref[...] = a_ref[ROW_TILE - POOL_HALO:, :]

    @pl.when(i >= n_prompt_tiles)
    def _():
        r = lax.broadcasted_iota(jnp.int32, (seq_per_tile * sample_len, 1), 0)
        pos = PAST_LEN + (r % sample_len)
        v, z = _branch_math(seq_per_tile, sample_len, pos, a_ref, u_ref, v_ref, ci_ref, cb_ref, cc_ref,
                            ph_ref[...], ch_ref[...], *shared)
        vo_ref[...] = v
        zo_ref[...] = z


def _branches(proj, layer, n_prompt_seq, prompt_len, n_sample_seq, sample_len, pool_state, conv_state,
              wg_bf, pool_scale, ln_g, ln_b, gate_mats, gate_bias, conv_w):
    t_total = proj.shape[0]
    tiles_per_seq = prompt_len // ROW_TILE
    npt = n_prompt_seq * tiles_per_seq
    seq_per_tile = ROW_TILE // sample_len
    nst = n_sample_seq // seq_per_tile
    sample_rows = n_sample_seq * sample_len
    act = jax.ShapeDtypeStruct((t_total, MIX_WIDTH), BF16)
    sample_f32 = jax.ShapeDtypeStruct((sample_rows, MIX_WIDTH), F32)

    def sample_tile(i):
        return jnp.maximum(i - npt, 0)

    def kind(i):
        return jnp.where(i >= npt, 1, 0)

    vec_spec = pl.BlockSpec((None, 1, MIX_WIDTH), lambda i: (layer, 0, 0))
    in_specs = [
        pl.BlockSpec(memory_space=pl.ANY),
        pl.BlockSpec((None, seq_per_tile * POOL_HALO, MIX_WIDTH), lambda i: (layer, sample_tile(i), 0)),
        pl.BlockSpec((None, seq_per_tile * CONV_HALO, MIX_WIDTH), lambda i: (layer, sample_tile(i), 0)),
        pl.BlockSpec((None, POOL_GROUPS, POOL_GROUP_DIM, POOL_GROUP_DIM), lambda i: (layer, 0, 0, 0)),
        vec_spec, vec_spec, vec_spec,
        pl.BlockSpec((None, None, SG_HEADS, CHUNK, CHUNK), lambda i: (layer, kind(i), 0, 0, 0)),
        pl.BlockSpec((None, None, CHUNK, MIX_WIDTH), lambda i: (layer, kind(i), 0, 0)),
        pl.BlockSpec((None, 3, MIX_WIDTH), lambda i: (layer, 0, 0)),
    ]
    row_spec = pl.BlockSpec((ROW_TILE, MIX_WIDTH), lambda i: (i, 0))
    out_specs = [
        row_spec, row_spec, row_spec,
        pl.BlockSpec((None, CONV_HALO, MIX_WIDTH), lambda i: (jnp.minimum(i // tiles_per_seq, n_prompt_seq - 1), 0, 0)),
        pl.BlockSpec((None, POOL_HALO, MIX_WIDTH), lambda i: (jnp.minimum(i // tiles_per_seq, n_prompt_seq - 1), 0, 0)),
        pl.BlockSpec((ROW_TILE, MIX_WIDTH), lambda i: (sample_tile(i), 0)),
        pl.BlockSpec((ROW_TILE, MIX_WIDTH), lambda i: (sample_tile(i), 0)),
    ]
    return pl.pallas_call(
        functools.partial(_branch_kernel, tiles_per_seq, npt, seq_per_tile, sample_len),
        grid=(npt + nst,),
        in_specs=in_specs,
        out_specs=out_specs,
        out_shape=[act, act, act, jax.ShapeDtypeStruct((n_prompt_seq, CONV_HALO, MIX_WIDTH), F32),
                   jax.ShapeDtypeStruct((n_prompt_seq, POOL_HALO, MIX_WIDTH), F32), sample_f32, sample_f32],
        scratch_shapes=[pltpu.VMEM((POOL_HALO, MIX_WIDTH), F32), pltpu.VMEM((CONV_HALO, MIX_WIDTH), F32),
                        pltpu.VMEM((SLAB_RING, 6, ROW_TILE, MIX_WIDTH), F32), pltpu.SemaphoreType.DMA((SLAB_RING,))],
        compiler_params=_cparams(("arbitrary",)),
        name="branches",
    )(proj, pool_state, conv_state, wg_bf, pool_scale, ln_g, ln_b,
      gate_mats, gate_bias, conv_w)


def _merge_kernel(oa_ref, ob_ref, oc_ref, wa_ref, wb_ref, wc_ref, ga_ref, gb_ref, gc_ref, o_ref):
    m = jax.nn.sigmoid(ga_ref[...]) * _mm(oa_ref[...], wa_ref[...])
    m = m + jax.nn.sigmoid(gb_ref[...]) * _mm(ob_ref[...], wb_ref[...])
    m = m + jax.nn.sigmoid(gc_ref[...]) * _mm(oc_ref[...], wc_ref[...])
    o_ref[...] = m.astype(BF16)


def _merge(oa, ob, oc, w_branch, proj, layer, tm=768, tn=512):
    t = oa.shape[0]
    act_spec = pl.BlockSpec((tm, MIX_WIDTH), lambda j, i: (i, 0))

    def wspec(b):
        return pl.BlockSpec((None, None, MIX_WIDTH, tn), lambda j, i: (layer, b, 0, j))

    def gspec(b):
        off = (GATE_COL0 + b * D_MODEL) // tn
        return pl.BlockSpec((tm, tn), lambda j, i: (i, off + j))

    return pl.pallas_call(
        _merge_kernel,
        grid=(D_MODEL // tn, t // tm),
        in_specs=[act_spec, act_spec, act_spec, wspec(0), wspec(1), wspec(2), gspec(0), gspec(1), gspec(2)],
        out_specs=pl.BlockSpec((tm, tn), lambda j, i: (i, j)),
        out_shape=jax.ShapeDtypeStruct((t, D_MODEL), BF16),
        compiler_params=_cparams(("arbitrary", "arbitrary")),
        name="merge",
    )(oa, ob, oc, w_branch, w_branch, w_branch, proj, proj, proj)


def _out_kernel2(n_prompt_tiles, a_ref, w_ref, xp_ref, xs_ref, o_ref):
    i = pl.program_id(1)
    h = _mm(a_ref[...], w_ref[...])

    @pl.when(i < n_prompt_tiles)
    def _():
        o_ref[...] = xp_ref[...] + h

    @pl.when(i >= n_prompt_tiles)
    def _():
        o_ref[...] = xs_ref[...] + h


def _out_kernel1(a_ref, w_ref, x_ref, o_ref):
    o_ref[...] = x_ref[...] + _mm(a_ref[...], w_ref[...])


def _out_proj(merged, w_out, layer, x_parts, tm=1024, tn=512):
    t = merged.shape[0]
    common = dict(
        grid=(D_MODEL // tn, t // tm),
        out_specs=pl.BlockSpec((tm, tn), lambda j, i: (i, j)),
        out_shape=jax.ShapeDtypeStruct((t, D_MODEL), F32),
        compiler_params=_cparams(("arbitrary", "arbitrary")),
        name="out_proj",
    )
    a_spec = pl.BlockSpec((tm, D_MODEL), lambda j, i: (i, 0))
    w_spec = pl.BlockSpec((None, D_MODEL, tn), lambda j, i: (layer, 0, j))
    if len(x_parts) == 1:
        return pl.pallas_call(
            _out_kernel1,
            in_specs=[a_spec, w_spec, pl.BlockSpec((tm, tn), lambda j, i: (i, j))],
            **common,
        )(merged, w_out, x_parts[0])
    xp, xs = x_parts
    npt = xp.shape[0] // tm
    return pl.pallas_call(
        functools.partial(_out_kernel2, npt),
        in_specs=[
            a_spec, w_spec,
            pl.BlockSpec((tm, tn), lambda j, i: (jnp.minimum(i, npt - 1), j)),
            pl.BlockSpec((tm, tn), lambda j, i: (jnp.maximum(i - npt, 0), j)),
        ],
        **common,
    )(merged, w_out, xp, xs)


def _router_kernel(x_ref, g_ref, wr_ref, br_ref, xn_ref, eid_ref, gate_ref):
    xb = _rms(x_ref[...], g_ref[...]).astype(BF16)
    lo = pltpu.bitcast(xb[:, :PACKED_WIDTH].astype(F32), jnp.uint32) >> 16
    hi = pltpu.bitcast(xb[:, PACKED_WIDTH:].astype(F32), jnp.uint32) & jnp.uint32(0xFFFF0000)
    xn_ref[...] = hi | lo
    logits = jnp.dot(xb, wr_ref[...], preferred_element_type=F32) + br_ref[...]
    tm = logits.shape[0]
    lane = lax.broadcasted_iota(jnp.int32, (tm, LANES), 1)
    neg = jnp.float32(-jnp.inf)
    far = jnp.int32(LANES)

    is_g = lane < N_EXPERT_GROUPS
    glog = jnp.where(is_g, logits, neg)
    gmax = jnp.max(glog, axis=1, keepdims=True)
    gsel = jnp.min(jnp.where(glog == gmax, lane, far), axis=1, keepdims=True)
    gsum = jnp.sum(jnp.where(is_g, jnp.exp(glog - gmax), 0.0), axis=1, keepdims=True)
    gp = 1.0 / gsum

    lo = N_EXPERT_GROUPS + gsel * EXPERTS_PER_GROUP
    in_grp = jnp.logical_and(lane >= lo, lane < lo + EXPERTS_PER_GROUP)
    el = jnp.where(in_grp, logits, neg)
    m1 = jnp.max(el, axis=1, keepdims=True)
    i1 = jnp.min(jnp.where(el == m1, lane, far), axis=1, keepdims=True)
    el2 = jnp.where(lane == i1, neg, el)
    m2 = jnp.max(el2, axis=1, keepdims=True)
    i2 = jnp.min(jnp.where(el2 == m2, lane, far), axis=1, keepdims=True)
    e2 = jnp.exp(m2 - m1)
    den = 1.0 + e2
    g1 = gp * (1.0 / den)
    g2 = gp * (e2 / den)

    eid_ref[...] = jnp.where(lane == 0, i1 - N_EXPERT_GROUPS, jnp.where(lane == 1, i2 - N_EXPERT_GROUPS, 0))
    gate_ref[...] = jnp.where(lane == 0, g1, jnp.where(lane == 1, g2, 0.0))


def _router(x1, g, wr, br, tm=256):
    t = x1.shape[0]
    return pl.pallas_call(
        _router_kernel,
        grid=(t // tm,),
        in_specs=[
            pl.BlockSpec((tm, D_MODEL), lambda i: (i, 0)),
            pl.BlockSpec((1, D_MODEL), lambda i: (0, 0)),
            pl.BlockSpec((D_MODEL, LANES), lambda i: (0, 0)),
            pl.BlockSpec((1, LANES), lambda i: (0, 0)),
        ],
        out_specs=[
            pl.BlockSpec((tm, PACKED_WIDTH), lambda i: (i, 0)),
            pl.BlockSpec((tm, LANES), lambda i: (i, 0)),
            pl.BlockSpec((tm, LANES), lambda i: (i, 0)),
        ],
        out_shape=[
            jax.ShapeDtypeStruct((t, PACKED_WIDTH), jnp.uint32),
            jax.ShapeDtypeStruct((t, LANES), jnp.int32),
            jax.ShapeDtypeStruct((t, LANES), F32),
        ],
        compiler_params=_cparams(("arbitrary",)),
        name="router",
    )(x1, g, wr, br)


def _dispatch_kernel(tm, dest_ref, x_ref, xs_in, xs_hbm, sem):
    del xs_in
    base = pl.program_id(0) * tm

    def body(r, c):
        for k in range(TOP_K):
            slot_row = dest_ref[(base + r) * TOP_K + k]
            pltpu.make_async_copy(x_ref.at[pl.ds(r, 1)], xs_hbm.at[pl.ds(slot_row, 1)], sem).start()
        return c

    lax.fori_loop(0, tm, body, 0, unroll=4)
    for _ in range(TOP_K):
        pltpu.make_async_copy(x_ref, xs_hbm.at[pl.ds(0, tm)], sem).wait()


def _dispatch(xn_packed, dest_flat, xs_init, tm=512):
    t = xn_packed.shape[0]
    cap = xs_init.shape[0]
    return pl.pallas_call(
        functools.partial(_dispatch_kernel, tm),
        grid_spec=pltpu.PrefetchScalarGridSpec(
            num_scalar_prefetch=1,
            grid=(t // tm,),
            in_specs=[pl.BlockSpec((tm, PACKED_WIDTH), lambda i, d: (i, 0)), pl.BlockSpec(memory_space=pl.ANY)],
            out_specs=pl.BlockSpec(memory_space=pl.ANY),
            scratch_shapes=[pltpu.SemaphoreType.DMA],
        ),
        out_shape=jax.ShapeDtypeStruct((cap, PACKED_WIDTH), jnp.uint32),
        input_output_aliases={2: 0},
        compiler_params=_cparams(("arbitrary",)),
        name="dispatch",
    )(dest_flat, xn_packed, xs_init)


def _stream_panels(first_ref, pidx_ref, meta_ref, pass_idx, n_pass, b, n_slots, panel_copies):
    n_distinct = meta_ref[1]
    n_panels = n_pass * n_distinct
    ahead = n_slots - 1

    def start(q):
        for c in panel_copies(q // n_distinct, q % n_distinct, q % n_slots):
            c.start()

    @pl.when(jnp.logical_and(pass_idx == 0, b == 0))
    def _():
        for q in range(ahead):
            @pl.when(q < n_panels)
            def _():
                start(q)

    p = pass_idx * n_distinct + pidx_ref[b]
    slot = p % n_slots

    @pl.when(first_ref[b] == 1)
    def _():
        for c in panel_copies(pass_idx, pidx_ref[b], slot):
            c.wait()

        @pl.when(p + ahead < n_panels)
        def _():
            start(p + ahead)

    return slot


UP_SLOTS = 3
DOWN_SLOTS = 2


def _expert_up_kernel(layer, tf, first_ref, pidx_ref, pe_ref, meta_ref, x_ref, w_hbm, o_ref, stage, sems):
    f = pl.program_id(0)
    b = pl.program_id(1)

    def panel_copies(fq, k, slot):
        e = pe_ref[k]
        col = pl.multiple_of(fq * tf, tf)
        return [pltpu.make_async_copy(w_hbm.at[layer, e, :, pl.ds(half * D_EXPERT + col, tf)],
                                      stage.at[slot, half], sems.at[slot]) for half in range(2)]

    slot = _stream_panels(first_ref, pidx_ref, meta_ref, f, pl.num_programs(0), b, UP_SLOTS, panel_copies)

    @pl.when(b < meta_ref[0])
    def _():
        words = x_ref[...]
        lo = pltpu.bitcast(words << 16, F32).astype(BF16)
        hi = pltpu.bitcast(words & jnp.uint32(0xFFFF0000), F32).astype(BF16)
        h1 = _mm(lo, stage[slot, 0, :PACKED_WIDTH, :]) + _mm(hi, stage[slot, 0, PACKED_WIDTH:, :])
        h3 = _mm(lo, stage[slot, 1, :PACKED_WIDTH, :]) + _mm(hi, stage[slot, 1, PACKED_WIDTH:, :])
        o_ref[...] = (h1 * jax.nn.sigmoid(h1) * h3).astype(BF16)

    @pl.when(b >= meta_ref[0])
    def _():
        o_ref[...] = jnp.zeros_like(o_ref)


def _expert_up(xs, w_ei, layer, sched, tf=512):
    cap = xs.shape[0]
    nb = cap // EXPERT_ROWS
    nf = D_EXPERT // tf
    return pl.pallas_call(
        functools.partial(_expert_up_kernel, layer, tf),
        grid_spec=pltpu.PrefetchScalarGridSpec(
            num_scalar_prefetch=4,
            grid=(nf, nb),
            in_specs=[
                pl.BlockSpec((EXPERT_ROWS, PACKED_WIDTH), lambda f, b, fi, pi, pe, meta: (jnp.minimum(b, meta[0] - 1), 0)),
                pl.BlockSpec(memory_space=pl.ANY),
            ],
            out_specs=pl.BlockSpec((EXPERT_ROWS, tf), lambda f, b, fi, pi, pe, meta: (b, f)),
            scratch_shapes=[pltpu.VMEM((UP_SLOTS, 2, D_MODEL, tf), F32), pltpu.SemaphoreType.DMA((UP_SLOTS,))],
        ),
        out_shape=jax.ShapeDtypeStruct((cap, D_EXPERT), BF16),
        compiler_params=_cparams(("arbitrary", "arbitrary")),
        name="expert_up",
    )(*sched, xs, w_ei)


def _expert_down_kernel(layer, first_ref, pidx_ref, pe_ref, meta_ref, h_ref, w_hbm, o_ref, stage, sems):
    b = pl.program_id(0)

    def panel_copies(fq, k, slot):
        del fq
        return [pltpu.make_async_copy(w_hbm.at[layer, pe_ref[k]], stage.at[slot], sems.at[slot])]

    slot = _stream_panels(first_ref, pidx_ref, meta_ref, 0, 1, b, DOWN_SLOTS, panel_copies)

    @pl.when(b < meta_ref[0])
    def _():
        o_ref[...] = _mm(h_ref[...], stage[slot])

    @pl.when(b >= meta_ref[0])
    def _():
        o_ref[...] = jnp.zeros_like(o_ref)


def _expert_down(hs, w_eo, layer, sched):
    cap = hs.shape[0]
    nb = cap // EXPERT_ROWS
    return pl.pallas_call(
        functools.partial(_expert_down_kernel, layer),
        grid_spec=pltpu.PrefetchScalarGridSpec(
            num_scalar_prefetch=4,
            grid=(nb,),
            in_specs=[
                pl.BlockSpec((EXPERT_ROWS, D_EXPERT), lambda b, fi, pi, pe, meta: (jnp.minimum(b, meta[0] - 1), 0)),
                pl.BlockSpec(memory_space=pl.ANY),
            ],
            out_specs=pl.BlockSpec((EXPERT_ROWS, D_MODEL), lambda b, fi, pi, pe, meta: (b, 0)),
            scratch_shapes=[pltpu.VMEM((DOWN_SLOTS, D_EXPERT, D_MODEL), F32), pltpu.SemaphoreType.DMA((DOWN_SLOTS,))],
        ),
        out_shape=jax.ShapeDtypeStruct((cap, D_MODEL), F32),
        compiler_params=_cparams(("arbitrary",)),
        name="expert_down",
    )(*sched, hs, w_eo)


def _gather_rows(idx_ref, idx_base, idx_stride, n_rows, src_hbm, dst, sem):
    def body(r, c):
        row = idx_ref[idx_base + r * idx_stride]
        pltpu.make_async_copy(src_hbm.at[pl.ds(row, 1)], dst.at[pl.ds(r, 1)], sem).start()
        return c

    lax.fori_loop(0, n_rows, body, 0, unroll=8)


def _wait_rows(n_rows, src_hbm, dst, sem):
    pltpu.make_async_copy(src_hbm.at[pl.ds(0, n_rows)], dst, sem).wait()


def _combine_kernel(tm, split, dest_ref, x_ref, gate_ref, g_ref, ys_hbm, *rest):
    if split is None:
        x2_ref, xn_ref, ybuf, sem = rest
    else:
        yp_ref, ysm_ref, ybuf, sem = rest
    i = pl.program_id(0)

    def issue(tile):
        slot = tile % 2
        for k in range(TOP_K):
            _gather_rows(dest_ref, tile * tm * TOP_K + k, TOP_K, tm, ys_hbm, ybuf.at[slot, k], sem.at[slot])

    @pl.when(i == 0)
    def _():
        issue(i)

    @pl.when(i + 1 < pl.num_programs(0))
    def _():
        issue(i + 1)

    slot = i % 2
    for k in range(TOP_K):
        _wait_rows(tm, ys_hbm, ybuf.at[slot, k], sem.at[slot])
    gates = gate_ref[...]
    x2 = x_ref[...] + (ybuf[slot, 0] * gates[:, 0:1] + ybuf[slot, 1] * gates[:, 1:2])
    if split is None:
        x2_ref[...] = x2
        xn_ref[...] = _rms(x2, g_ref[...]).astype(xn_ref.dtype)
    else:
        y = _rms(x2, g_ref[...])

        @pl.when(i < split)
        def _():
            yp_ref[...] = y

        @pl.when(i >= split)
        def _():
            ysm_ref[...] = y


def _combine(x1, gates, dest_flat, ys, g, final_split=None, tm=256):
    t = x1.shape[0]
    in_specs = [
        pl.BlockSpec((tm, D_MODEL), lambda i, d: (i, 0)),
        pl.BlockSpec((tm, LANES), lambda i, d: (i, 0)),
        pl.BlockSpec((1, D_MODEL), lambda i, d: (0, 0)),
        pl.BlockSpec(memory_space=pl.ANY),
    ]
    if final_split is None:
        split = None
        out_specs = [pl.BlockSpec((tm, D_MODEL), lambda i, d: (i, 0)),
                     pl.BlockSpec((tm, D_MODEL), lambda i, d: (i, 0))]
        out_shape = [jax.ShapeDtypeStruct((t, D_MODEL), F32), jax.ShapeDtypeStruct((t, D_MODEL), BF16)]
    else:
        split = final_split // tm
        out_specs = [pl.BlockSpec((tm, D_MODEL), lambda i, d: (jnp.minimum(i, split - 1), 0)),
                     pl.BlockSpec((tm, D_MODEL), lambda i, d: (jnp.maximum(i - split, 0), 0))]
        out_shape = [jax.ShapeDtypeStruct((final_split, D_MODEL), F32),
                     jax.ShapeDtypeStruct((t - final_split, D_MODEL), F32)]
    return pl.pallas_call(
        functools.partial(_combine_kernel, tm, split),
        grid_spec=pltpu.PrefetchScalarGridSpec(
            num_scalar_prefetch=1,
            grid=(t // tm,),
            in_specs=in_specs,
            out_specs=out_specs,
            scratch_shapes=[pltpu.VMEM((2, TOP_K, tm, D_MODEL), F32), pltpu.SemaphoreType.DMA((2,))],
        ),
        out_shape=out_shape,
        compiler_params=_cparams(("arbitrary",)),
        name="combine",
    )(dest_flat, x1, gates, g, ys)


def _routing_tables(eid, n_blocks):
    e_flat = eid.reshape(-1)
    onehot = (e_flat[:, None] == jnp.arange(N_EXPERTS, dtype=jnp.int32)[None, :]).astype(jnp.int32)
    csum = jnp.cumsum(onehot, axis=0)
    rank = jnp.sum(csum * onehot, axis=1) - 1
    counts = csum[-1]
    blocks = (counts + EXPERT_ROWS - 1) // EXPERT_ROWS
    blk_end = jnp.cumsum(blocks)
    blk_start = blk_end - blocks
    dest = (blk_start[e_flat] * EXPERT_ROWS + rank).astype(jnp.int32)
    n_used = blk_end[-1]

    has = blocks > 0
    order = jnp.cumsum(has.astype(jnp.int32)) - 1
    panel_expert = jnp.argsort(jnp.logical_not(has), stable=True)
    blk_ids = jnp.arange(n_blocks, dtype=jnp.int32)
    blk_e = jnp.minimum(jnp.searchsorted(blk_end, blk_ids, side="right"), N_EXPERTS - 1)
    used = blk_ids < n_used
    first = jnp.logical_and(used, blk_ids == blk_start[blk_e])
    pidx = jnp.where(used, order[blk_e], 0)
    meta = jnp.stack([n_used, jnp.sum(has.astype(jnp.int32))])
    i32 = lambda a: a.astype(jnp.int32)
    return dest, (i32(first), i32(pidx), i32(panel_expert), i32(meta))


def _gate_tables(w_spatial, b_spatial, sample_len):
    depth = w_spatial.shape[0]
    tril = jnp.tril(jnp.ones((CHUNK, CHUNK), dtype=bool))
    full = jnp.where(tril[None, None], w_spatial, 0)
    reps = CHUNK // sample_len
    small = jnp.where(tril[None, None, :sample_len, :sample_len], w_spatial[:, :, :sample_len, :sample_len], 0)
    eye = jnp.eye(reps, dtype=w_spatial.dtype)
    blockdiag = jnp.einsum("ab,lhts->lhatbs", eye, small).reshape(depth, SG_HEADS, CHUNK, CHUNK)
    mats = jnp.stack([full, blockdiag], axis=1).astype(BF16)
    bias_full = jnp.transpose(b_spatial, (0, 2, 1))
    bias_small = jnp.tile(bias_full[:, :sample_len, :], (1, reps, 1))
    bias = jnp.stack([bias_full, bias_small], axis=1)
    bias = jnp.repeat(bias, SG_HEAD_DIM, axis=-1)
    return mats, bias


def kernel(x_prompt, x_sample, state_pool, state_conv, norm_mix, w_in, w_pool_group, pool_scale, sg_norm_g, sg_norm_b, w_spatial, b_spatial, conv_w, w_branch, w_out, norm_ffn, w_router_group, b_router_group, w_router_expert, b_router_expert, w_expert_in, w_expert_out, norm_final):
    depth = w_in.shape[0]
    bp, lp, _ = x_prompt.shape
    bs, ls, _ = x_sample.shape
    tp, ts = bp * lp, bs * ls
    t = tp + ts
    assert lp % ROW_TILE == 0 and ROW_TILE % ls == 0 and bs % (ROW_TILE // ls) == 0 and ls >= CONV_STATE

    xp = x_prompt.reshape(tp, D_MODEL)
    xs = x_sample.reshape(ts, D_MODEL)

    gate_mats, gate_bias = _gate_tables(w_spatial, b_spatial, ls)
    wg_bf = w_pool_group.astype(BF16)
    vec = lambda a: a.reshape(depth, 1, -1)
    wr = jnp.concatenate([w_router_group, w_router_expert,
                          jnp.zeros((depth, D_MODEL, LANES - N_EXPERT_GROUPS - N_EXPERTS), F32)], axis=-1).astype(BF16)
    br = jnp.concatenate([b_router_group, b_router_expert,
                          jnp.zeros((depth, LANES - N_EXPERT_GROUPS - N_EXPERTS), F32)], axis=-1).reshape(depth, 1, LANES)
    pool_hist = jnp.pad(state_pool, ((0, 0), (0, 0), (POOL_HALO - POOL_STATE, 0), (0, 0))).reshape(
        depth, bs * POOL_HALO, MIX_WIDTH)
    conv_hist = jnp.pad(state_conv, ((0, 0), (0, 0), (CONV_HALO - CONV_STATE, 0), (0, 0))).reshape(
        depth, bs * CONV_HALO, MIX_WIDTH)

    n_assign = t * TOP_K
    n_blocks = n_assign // EXPERT_ROWS + N_EXPERTS

    pool_p, pool_s, conv_p, conv_s, v_out = [], [], [], [], []
    xs_sorted = jnp.zeros((n_blocks * EXPERT_ROWS, PACKED_WIDTH), jnp.uint32)
    x_parts = (xp, xs)
    xn = _norm_in(xp, xs, norm_mix[0].reshape(1, D_MODEL))
    y_p = y_s = None
    for l in range(depth):
        proj = _in_proj(xn, w_in, l)
        oa, ob, oc, z_tail, a_tail, v_s, z_s = _branches(
            proj, l, bp, lp, bs, ls, pool_hist, conv_hist, wg_bf, vec(pool_scale), vec(sg_norm_g),
            vec(sg_norm_b), gate_mats, gate_bias, conv_w)
        merged = _merge(oa, ob, oc, w_branch, proj, l)
        x1 = _out_proj(merged, w_out, l, x_parts)

        xn2, eid, gates = _router(x1, norm_ffn[l].reshape(1, D_MODEL), wr[l], br[l])
        dest, sched = _routing_tables(eid[:, :TOP_K], n_blocks)
        xs_sorted = _dispatch(xn2, dest, xs_sorted)
        hs = _expert_up(xs_sorted, w_expert_in, l, sched)
        ys = _expert_down(hs, w_expert_out, l, sched)
        if l + 1 < depth:
            x2, xn = _combine(x1, gates, dest, ys, norm_mix[l + 1].reshape(1, D_MODEL))
            x_parts = (x2,)
        else:
            y_p, y_s = _combine(x1, gates, dest, ys, norm_final.reshape(1, D_MODEL), final_split=tp)

        a_s = proj[tp:, :MIX_WIDTH].reshape(bs, ls, MIX_WIDTH)
        pool_p.append(a_tail[:, POOL_HALO - POOL_STATE:])
        pool_s.append(jnp.concatenate([state_pool[l], a_s], axis=1)[:, -POOL_STATE:])
        conv_p.append(z_tail[:, CONV_HALO - CONV_STATE:])
        conv_s.append(z_s.reshape(bs, ls, MIX_WIDTH)[:, ls - CONV_STATE:])
        v_out.append(v_s.reshape(bs, ls, MIX_WIDTH))

    return (y_p.reshape(bp, lp, D_MODEL), y_s.reshape(bs, ls, D_MODEL), jnp.stack(pool_p), jnp.stack(pool_s),
            jnp.stack(conv_p), jnp.stack(conv_s), jnp.stack(v_out))
```

```python
import functools

import jax
import jax.numpy as jnp
from jax import lax
from jax.experimental import pallas as pl
from jax.experimental.pallas import tpu as pltpu

F32 = jnp.float32
BF16 = jnp.bfloat16

D_MODEL = 4096
MIX_WIDTH = D_MODEL // 2
POOL_GROUPS = 4
POOL_WINDOWS = (2, 4, 8, 16)
POOL_GROUP_DIM = MIX_WIDTH // POOL_GROUPS
POOL_STATE = 15
SG_HEADS = 8
SG_HEAD_DIM = MIX_WIDTH // SG_HEADS
CHUNK = 128
CONV_STATE = 2
IN_COLS = MIX_WIDTH * 6 + 3 * D_MODEL
GATE_COL0 = MIX_WIDTH * 6
N_EXPERT_GROUPS = 4
EXPERTS_PER_GROUP = 8
N_EXPERTS = N_EXPERT_GROUPS * EXPERTS_PER_GROUP
TOP_K = 2
D_EXPERT = D_MODEL // 4
PAST_LEN = 16384
EPS = 1e-6

LANES = 128
SUBLANES = 8
POOL_HALO = 16
CONV_HALO = 8
ROW_TILE = 128
EXPERT_ROWS = 256
PACKED_WIDTH = D_MODEL // 2
VMEM_LIMIT = 56 * 1024 * 1024


def _cparams(sem):
    return pltpu.CompilerParams(dimension_semantics=sem, vmem_limit_bytes=VMEM_LIMIT)


def _gelu_tanh(x):
    c = 0.7978845608028654
    return 0.5 * x * (1.0 + jnp.tanh(c * (x + 0.044715 * (x * x * x))))


def _rms(x, g):
    return x * lax.rsqrt(jnp.mean(x * x, axis=-1, keepdims=True) + EPS) * g


def _norm_in_kernel(n_prompt_tiles, xp_ref, xs_ref, g_ref, o_ref):
    i = pl.program_id(0)

    @pl.when(i < n_prompt_tiles)
    def _():
        o_ref[...] = _rms(xp_ref[...], g_ref[...]).astype(BF16)

    @pl.when(i >= n_prompt_tiles)
    def _():
        o_ref[...] = _rms(xs_ref[...], g_ref[...]).astype(BF16)


def _norm_in(xp, xs, g, tm=512):
    tp, ts = xp.shape[0], xs.shape[0]
    npt, nst = tp // tm, ts // tm
    return pl.pallas_call(
        functools.partial(_norm_in_kernel, npt),
        grid=(npt + nst,),
        in_specs=[
            pl.BlockSpec((tm, D_MODEL), lambda i: (jnp.minimum(i, npt - 1), 0)),
            pl.BlockSpec((tm, D_MODEL), lambda i: (jnp.maximum(i - npt, 0), 0)),
            pl.BlockSpec((1, D_MODEL), lambda i: (0, 0)),
        ],
        out_specs=pl.BlockSpec((tm, D_MODEL), lambda i: (i, 0)),
        out_shape=jax.ShapeDtypeStruct((tp + ts, D_MODEL), BF16),
        compiler_params=_cparams(("arbitrary",)),
        name="norm_in",
    )(xp, xs, g)


def _mm(a, w):
    return lax.dot_general(a, w, (((1,), (0,)), ((), ())), preferred_element_type=F32)


def _panel_matmul_kernel(a_ref, w_ref, o_ref):
    o_ref[...] = _mm(a_ref[...], w_ref[...])


def _in_proj(xn, w_in, layer, tm=768, tn=1024):
    t = xn.shape[0]
    return pl.pallas_call(
        _panel_matmul_kernel,
        grid=(IN_COLS // tn, t // tm),
        in_specs=[
            pl.BlockSpec((tm, D_MODEL), lambda j, i: (i, 0)),
            pl.BlockSpec((None, D_MODEL, tn), lambda j, i: (layer, 0, j)),
        ],
        out_specs=pl.BlockSpec((tm, tn), lambda j, i: (i, j)),
        out_shape=jax.ShapeDtypeStruct((t, IN_COLS), F32),
        compiler_params=_cparams(("arbitrary", "arbitrary")),
        name="in_proj",
    )(xn, w_in)


def _window_rows(ext, n_seq, halo, rows):
    if n_seq == 1:
        return ext[halo:, :]
    c = ext.shape[-1]
    return ext.reshape(n_seq, halo + rows, c)[:, halo:, :].reshape(n_seq * rows, c)


def _stack_history(hist, cur, n_seq, halo, rows):
    if n_seq == 1:
        return jnp.concatenate([hist, cur], axis=0)
    c = cur.shape[-1]
    ext = jnp.concatenate([hist.reshape(n_seq, halo, c), cur.reshape(n_seq, rows, c)], axis=1)
    return ext.reshape(n_seq * (halo + rows), c)


def _branch_math(n_seq, rows, pos, a_ref, u_ref, v_ref, ci_ref, cb_ref, cc_ref, pool_hist, conv_hist,
                 wg_ref, ps_ref, lng_ref, lnb_ref, m_ref, bias_ref, cw_ref, oa_ref, ob_ref, oc_ref):
    for g, w in enumerate(POOL_WINDOWS):
        cols = slice(g * POOL_GROUP_DIM, (g + 1) * POOL_GROUP_DIM)
        a_g = a_ref[:, cols]
        s = _stack_history(pool_hist[:, cols], a_g, n_seq, POOL_HALO, rows)
        k = 1
        while k < w:
            s = s + pltpu.roll(s, k, 0)
            k *= 2
        s = _window_rows(s, n_seq, POOL_HALO, rows)
        cnt = jnp.minimum(w, pos + 1).astype(F32)
        pooled = (s / cnt - a_g).astype(BF16)
        out = jnp.dot(pooled, wg_ref[g], preferred_element_type=F32) * ps_ref[:, cols]
        oa_ref[:, cols] = out.astype(BF16)

    vg = _gelu_tanh(v_ref[...])
    mu = jnp.mean(vg, axis=-1, keepdims=True)
    xc = vg - mu
    v = xc * lax.rsqrt(jnp.mean(xc * xc, axis=-1, keepdims=True) + EPS) * lng_ref[...] + lnb_ref[...]
    vb = v.astype(BF16)
    for h in range(SG_HEADS):
        cols = slice(h * SG_HEAD_DIM, (h + 1) * SG_HEAD_DIM)
        sp = jnp.dot(m_ref[h], vb[:, cols], preferred_element_type=F32) + bias_ref[:, cols]
        ob_ref[:, cols] = (_gelu_tanh(u_ref[:, cols]) * sp).astype(BF16)

    z = cc_ref[...] * ci_ref[...]
    e = _stack_history(conv_hist, z, n_seq, CONV_HALO, rows)
    y = cw_ref[0:1, :] * pltpu.roll(e, 2, 0)
    y = y + cw_ref[1:2, :] * pltpu.roll(e, 1, 0)
    y = y + cw_ref[2:3, :] * e
    oc_ref[...] = (cb_ref[...] * _window_rows(y, n_seq, CONV_HALO, rows)).astype(BF16)
    return v, z


SLAB_RING = 4


def _branch_kernel(tiles_per_seq, n_prompt_tiles, seq_per_tile, sample_len,
                   proj_hbm, ph_ref, ch_ref, wg_ref, ps_ref, lng_ref, lnb_ref,
                   m_ref, bias_ref, cw_ref, oa_ref, ob_ref, oc_ref, zl_ref, al_ref, vo_ref, zo_ref,
                   pool_hist, conv_hist, slabs, slab_sems):
    i = pl.program_id(0)
    n_tiles = pl.num_programs(0)
    shared = (wg_ref, ps_ref, lng_ref, lnb_ref, m_ref, bias_ref, cw_ref, oa_ref, ob_ref, oc_ref)

    def slab_copies(tile):
        slot = tile % SLAB_RING
        first = tile * ROW_TILE
        rows = pl.ds(first if isinstance(first, int) else pl.multiple_of(first, ROW_TILE), ROW_TILE)
        return [pltpu.make_async_copy(proj_hbm.at[rows, pl.ds(c * MIX_WIDTH, MIX_WIDTH)], slabs.at[slot, c],
                                      slab_sems.at[slot]) for c in range(6)]

    @pl.when(i == 0)
    def _():
        for tile in range(SLAB_RING - 1):
            for c in slab_copies(tile):
                c.start()

    @pl.when(i + SLAB_RING - 1 < n_tiles)
    def _():
        for c in slab_copies(i + SLAB_RING - 1):
            c.start()

    for c in slab_copies(i):
        c.wait()
    cur = i % SLAB_RING
    a_ref, u_ref, v_ref, ci_ref, cb_ref, cc_ref = (slabs.at[cur, c] for c in range(6))

    @pl.when(i < n_prompt_tiles)
    def _():
        t = i % tiles_per_seq

        @pl.when(t == 0)
        def _():
            pool_hist[...] = jnp.zeros_like(pool_hist)
            conv_hist[...] = jnp.zeros_like(conv_hist)

        pos = t * ROW_TILE + lax.broadcasted_iota(jnp.int32, (ROW_TILE, 1), 0)
        _, z = _branch_math(1, ROW_TILE, pos, a_ref, u_ref, v_ref, ci_ref, cb_ref, cc_ref, pool_hist[...],
                            conv_hist[...], *shared)
        pool_hist[...] = a_ref[ROW_TILE - POOL_HALO:, :]
        conv_hist[...] = z[ROW_TILE - CONV_HALO:, :]
        zl_ref[...] = z[ROW_TILE - CONV_HALO:, :]
        al_ref[...] = a_ref[ROW_TILE - POOL_HALO:, :]

    @pl.when(i >= n_prompt_tiles)
    def _():
        r = lax.broadcasted_iota(jnp.int32, (seq_per_tile * sample_len, 1), 0)
        pos = PAST_LEN + (r % sample_len)
        v, z = _branch_math(seq_per_tile, sample_len, pos, a_ref, u_ref, v_ref, ci_ref, cb_ref, cc_ref,
                            ph_ref[...], ch_ref[...], *shared)
        vo_ref[...] = v
        zo_ref[...] = z


def _branches(proj, layer, n_prompt_seq, prompt_len, n_sample_seq, sample_len, pool_state, conv_state,
              wg_bf, pool_scale, ln_g, ln_b, gate_mats, gate_bias, conv_w):
    t_total = proj.shape[0]
    tiles_per_seq = prompt_len // ROW_TILE
    npt = n_prompt_seq * tiles_per_seq
    seq_per_tile = ROW_TILE // sample_len
    nst = n_sample_seq // seq_per_tile
    sample_rows = n_sample_seq * sample_len
    act = jax.ShapeDtypeStruct((t_total, MIX_WIDTH), BF16)
    sample_f32 = jax.ShapeDtypeStruct((sample_rows, MIX_WIDTH), F32)

    def sample_tile(i):
        return jnp.maximum(i - npt, 0)

    def kind(i):
        return jnp.where(i >= npt, 1, 0)

    vec_spec = pl.BlockSpec((None, 1, MIX_WIDTH), lambda i: (layer, 0, 0))
    in_specs = [
        pl.BlockSpec(memory_space=pl.ANY),
        pl.BlockSpec((None, seq_per_tile * POOL_HALO, MIX_WIDTH), lambda i: (layer, sample_tile(i), 0)),
        pl.BlockSpec((None, seq_per_tile * CONV_HALO, MIX_WIDTH), lambda i: (layer, sample_tile(i), 0)),
        pl.BlockSpec((None, POOL_GROUPS, POOL_GROUP_DIM, POOL_GROUP_DIM), lambda i: (layer, 0, 0, 0)),
        vec_spec, vec_spec, vec_spec,
        pl.BlockSpec((None, None, SG_HEADS, CHUNK, CHUNK), lambda i: (layer, kind(i), 0, 0, 0)),
        pl.BlockSpec((None, None, CHUNK, MIX_WIDTH), lambda i: (layer, kind(i), 0, 0)),
        pl.BlockSpec((None, 3, MIX_WIDTH), lambda i: (layer, 0, 0)),
    ]
    row_spec = pl.BlockSpec((ROW_TILE, MIX_WIDTH), lambda i: (i, 0))
    out_specs = [
        row_spec, row_spec, row_spec,
        pl.BlockSpec((None, CONV_HALO, MIX_WIDTH), lambda i: (jnp.minimum(i // tiles_per_seq, n_prompt_seq - 1), 0, 0)),
        pl.BlockSpec((None, POOL_HALO, MIX_WIDTH), lambda i: (jnp.minimum(i // tiles_per_seq, n_prompt_seq - 1), 0, 0)),
        pl.BlockSpec((ROW_TILE, MIX_WIDTH), lambda i: (sample_tile(i), 0)),
        pl.BlockSpec((ROW_TILE, MIX_WIDTH), lambda i: (sample_tile(i), 0)),
    ]
    return pl.pallas_call(
        functools.partial(_branch_kernel, tiles_per_seq, npt, seq_per_tile, sample_len),
        grid=(npt + nst,),
        in_specs=in_specs,
        out_specs=out_specs,
        out_shape=[act, act, act, jax.ShapeDtypeStruct((n_prompt_seq, CONV_HALO, MIX_WIDTH), F32),
                   jax.ShapeDtypeStruct((n_prompt_seq, POOL_HALO, MIX_WIDTH), F32), sample_f32, sample_f32],
        scratch_shapes=[pltpu.VMEM((POOL_HALO, MIX_WIDTH), F32), pltpu.VMEM((CONV_HALO, MIX_WIDTH), F32),
                        pltpu.VMEM((SLAB_RING, 6, ROW_TILE, MIX_WIDTH), F32), pltpu.SemaphoreType.DMA((SLAB_RING,))],
        compiler_params=_cparams(("arbitrary",)),
        name="branches",
    )(proj, pool_state, conv_state, wg_bf, pool_scale, ln_g, ln_b,
      gate_mats, gate_bias, conv_w)


def _merge_kernel(oa_ref, ob_ref, oc_ref, wa_ref, wb_ref, wc_ref, ga_ref, gb_ref, gc_ref, o_ref):
    m = jax.nn.sigmoid(ga_ref[...]) * _mm(oa_ref[...], wa_ref[...])
    m = m + jax.nn.sigmoid(gb_ref[...]) * _mm(ob_ref[...], wb_ref[...])
    m = m + jax.nn.sigmoid(gc_ref[...]) * _mm(oc_ref[...], wc_ref[...])
    o_ref[...] = m.astype(BF16)


def _merge(oa, ob, oc, w_branch, proj, layer, tm=768, tn=512):
    t = oa.shape[0]
    act_spec = pl.BlockSpec((tm, MIX_WIDTH), lambda j, i: (i, 0))

    def wspec(b):
        return pl.BlockSpec((None, None, MIX_WIDTH, tn), lambda j, i: (layer, b, 0, j))

    def gspec(b):
        off = (GATE_COL0 + b * D_MODEL) // tn
        return pl.BlockSpec((tm, tn), lambda j, i: (i, off + j))

    return pl.pallas_call(
        _merge_kernel,
        grid=(D_MODEL // tn, t // tm),
        in_specs=[act_spec, act_spec, act_spec, wspec(0), wspec(1), wspec(2), gspec(0), gspec(1), gspec(2)],
        out_specs=pl.BlockSpec((tm, tn), lambda j, i: (i, j)),
        out_shape=jax.ShapeDtypeStruct((t, D_MODEL), BF16),
        compiler_params=_cparams(("arbitrary", "arbitrary")),
        name="merge",
    )(oa, ob, oc, w_branch, w_branch, w_branch, proj, proj, proj)


def _out_kernel2(n_prompt_tiles, a_ref, w_ref, xp_ref, xs_ref, o_ref):
    i = pl.program_id(1)
    h = _mm(a_ref[...], w_ref[...])

    @pl.when(i < n_prompt_tiles)
    def _():
        o_ref[...] = xp_ref[...] + h

    @pl.when(i >= n_prompt_tiles)
    def _():
        o_ref[...] = xs_ref[...] + h


def _out_kernel1(a_ref, w_ref, x_ref, o_ref):
    o_ref[...] = x_ref[...] + _mm(a_ref[...], w_ref[...])


def _out_proj(merged, w_out, layer, x_parts, tm=1024, tn=512):
    t = merged.shape[0]
    common = dict(
        grid=(D_MODEL // tn, t // tm),
        out_specs=pl.BlockSpec((tm, tn), lambda j, i: (i, j)),
        out_shape=jax.ShapeDtypeStruct((t, D_MODEL), F32),
        compiler_params=_cparams(("arbitrary", "arbitrary")),
        name="out_proj",
    )
    a_spec = pl.BlockSpec((tm, D_MODEL), lambda j, i: (i, 0))
    w_spec = pl.BlockSpec((None, D_MODEL, tn), lambda j, i: (layer, 0, j))
    if len(x_parts) == 1:
        return pl.pallas_call(
            _out_kernel1,
            in_specs=[a_spec, w_spec, pl.BlockSpec((tm, tn), lambda j, i: (i, j))],
            **common,
        )(merged, w_out, x_parts[0])
    xp, xs = x_parts
    npt = xp.shape[0] // tm
    return pl.pallas_call(
        functools.partial(_out_kernel2, npt),
        in_specs=[
            a_spec, w_spec,
            pl.BlockSpec((tm, tn), lambda j, i: (jnp.minimum(i, npt - 1), j)),
            pl.BlockSpec((tm, tn), lambda j, i: (jnp.maximum(i - npt, 0), j)),
        ],
        **common,
    )(merged, w_out, xp, xs)


def _router_kernel(x_ref, g_ref, wr_ref, br_ref, xn_ref, eid_ref, gate_ref):
    xb = _rms(x_ref[...], g_ref[...]).astype(BF16)
    lo = pltpu.bitcast(xb[:, :PACKED_WIDTH].astype(F32), jnp.uint32) >> 16
    hi = pltpu.bitcast(xb[:, PACKED_WIDTH:].astype(F32), jnp.uint32) & jnp.uint32(0xFFFF0000)
    xn_ref[...] = hi | lo
    logits = jnp.dot(xb, wr_ref[...], preferred_element_type=F32) + br_ref[...]
    tm = logits.shape[0]
    lane = lax.broadcasted_iota(jnp.int32, (tm, LANES), 1)
    neg = jnp.float32(-jnp.inf)
    far = jnp.int32(LANES)

    is_g = lane < N_EXPERT_GROUPS
    glog = jnp.where(is_g, logits, neg)
    gmax = jnp.max(glog, axis=1, keepdims=True)
    gsel = jnp.min(jnp.where(glog == gmax, lane, far), axis=1, keepdims=True)
    gsum = jnp.sum(jnp.where(is_g, jnp.exp(glog - gmax), 0.0), axis=1, keepdims=True)
    gp = 1.0 / gsum

    lo = N_EXPERT_GROUPS + gsel * EXPERTS_PER_GROUP
    in_grp = jnp.logical_and(lane >= lo, lane < lo + EXPERTS_PER_GROUP)
    el = jnp.where(in_grp, logits, neg)
    m1 = jnp.max(el, axis=1, keepdims=True)
    i1 = jnp.min(jnp.where(el == m1, lane, far), axis=1, keepdims=True)
    el2 = jnp.where(lane == i1, neg, el)
    m2 = jnp.max(el2, axis=1, keepdims=True)
    i2 = jnp.min(jnp.where(el2 == m2, lane, far), axis=1, keepdims=True)
    e2 = jnp.exp(m2 - m1)
    den = 1.0 + e2
    g1 = gp * (1.0 / den)
    g2 = gp * (e2 / den)

    eid_ref[...] = jnp.where(lane == 0, i1 - N_EXPERT_GROUPS, jnp.where(lane == 1, i2 - N_EXPERT_GROUPS, 0))
    gate_ref[...] = jnp.where(lane == 0, g1, jnp.where(lane == 1, g2, 0.0))


def _router(x1, g, wr, br, tm=512):
    t = x1.shape[0]
    return pl.pallas_call(
        _router_kernel,
        grid=(t // tm,),
        in_specs=[
            pl.BlockSpec((tm, D_MODEL), lambda i: (i, 0)),
            pl.BlockSpec((1, D_MODEL), lambda i: (0, 0)),
            pl.BlockSpec((D_MODEL, LANES), lambda i: (0, 0)),
            pl.BlockSpec((1, LANES), lambda i: (0, 0)),
        ],
        out_specs=[
            pl.BlockSpec((tm, PACKED_WIDTH), lambda i: (i, 0)),
            pl.BlockSpec((tm, LANES), lambda i: (i, 0)),
            pl.BlockSpec((tm, LANES), lambda i: (i, 0)),
        ],
        out_shape=[
            jax.ShapeDtypeStruct((t, PACKED_WIDTH), jnp.uint32),
            jax.ShapeDtypeStruct((t, LANES), jnp.int32),
            jax.ShapeDtypeStruct((t, LANES), F32),
        ],
        compiler_params=_cparams(("arbitrary",)),
        name="router",
    )(x1, g, wr, br)


def _dispatch_kernel(tm, dest_ref, x_ref, xs_in, xs_hbm, sem):
    del xs_in
    base = pl.program_id(0) * tm

    def body(r, c):
        for k in range(TOP_K):
            slot_row = dest_ref[(base + r) * TOP_K + k]
            pltpu.make_async_copy(x_ref.at[pl.ds(r, 1)], xs_hbm.at[pl.ds(slot_row, 1)], sem).start()
        return c

    lax.fori_loop(0, tm, body, 0, unroll=4)
    for _ in range(TOP_K):
        pltpu.make_async_copy(x_ref, xs_hbm.at[pl.ds(0, tm)], sem).wait()


def _dispatch(xn_packed, dest_flat, xs_init, tm=512):
    t = xn_packed.shape[0]
    cap = xs_init.shape[0]
    return pl.pallas_call(
        functools.partial(_dispatch_kernel, tm),
        grid_spec=pltpu.PrefetchScalarGridSpec(
            num_scalar_prefetch=1,
            grid=(t // tm,),
            in_specs=[pl.BlockSpec((tm, PACKED_WIDTH), lambda i, d: (i, 0)), pl.BlockSpec(memory_space=pl.ANY)],
            out_specs=pl.BlockSpec(memory_space=pl.ANY),
            scratch_shapes=[pltpu.SemaphoreType.DMA],
        ),
        out_shape=jax.ShapeDtypeStruct((cap, PACKED_WIDTH), jnp.uint32),
        input_output_aliases={2: 0},
        compiler_params=_cparams(("arbitrary",)),
        name="dispatch",
    )(dest_flat, xn_packed, xs_init)


def _stream_panels(first_ref, pidx_ref, meta_ref, pass_idx, n_pass, b, n_slots, panel_copies):
    n_distinct = meta_ref[1]
    n_panels = n_pass * n_distinct
    ahead = n_slots - 1

    def start(q):
        for c in panel_copies(q // n_distinct, q % n_distinct, q % n_slots):
            c.start()

    @pl.when(jnp.logical_and(pass_idx == 0, b == 0))
    def _():
        for q in range(ahead):
            @pl.when(q < n_panels)
            def _():
                start(q)

    p = pass_idx * n_distinct + pidx_ref[b]
    slot = p % n_slots

    @pl.when(first_ref[b] == 1)
    def _():
        for c in panel_copies(pass_idx, pidx_ref[b], slot):
            c.wait()

        @pl.when(p + ahead < n_panels)
        def _():
            start(p + ahead)

    return slot


UP_SLOTS = 3
DOWN_SLOTS = 2


def _expert_up_kernel(layer, tf, first_ref, pidx_ref, pe_ref, meta_ref, x_ref, w_hbm, o_ref, stage, sems):
    f = pl.program_id(0)
    b = pl.program_id(1)

    def panel_copies(fq, k, slot):
        e = pe_ref[k]
        col = pl.multiple_of(fq * tf, tf)
        return [pltpu.make_async_copy(w_hbm.at[layer, e, :, pl.ds(half * D_EXPERT + col, tf)],
                                      stage.at[slot, half], sems.at[slot]) for half in range(2)]

    slot = _stream_panels(first_ref, pidx_ref, meta_ref, f, pl.num_programs(0), b, UP_SLOTS, panel_copies)

    @pl.when(b < meta_ref[0])
    def _():
        words = x_ref[...]
        lo = pltpu.bitcast(words << 16, F32).astype(BF16)
        hi = pltpu.bitcast(words & jnp.uint32(0xFFFF0000), F32).astype(BF16)
        h1 = _mm(lo, stage[slot, 0, :PACKED_WIDTH, :]) + _mm(hi, stage[slot, 0, PACKED_WIDTH:, :])
        h3 = _mm(lo, stage[slot, 1, :PACKED_WIDTH, :]) + _mm(hi, stage[slot, 1, PACKED_WIDTH:, :])
        o_ref[...] = (h1 * jax.nn.sigmoid(h1) * h3).astype(BF16)

    @pl.when(b >= meta_ref[0])
    def _():
        o_ref[...] = jnp.zeros_like(o_ref)


def _expert_up(xs, w_ei, layer, sched, tf=512):
    cap = xs.shape[0]
    nb = cap // EXPERT_ROWS
    nf = D_EXPERT // tf
    return pl.pallas_call(
        functools.partial(_expert_up_kernel, layer, tf),
        grid_spec=pltpu.PrefetchScalarGridSpec(
            num_scalar_prefetch=4,
            grid=(nf, nb),
            in_specs=[
                pl.BlockSpec((EXPERT_ROWS, PACKED_WIDTH), lambda f, b, fi, pi, pe, meta: (jnp.minimum(b, meta[0] - 1), 0)),
                pl.BlockSpec(memory_space=pl.ANY),
            ],
            out_specs=pl.BlockSpec((EXPERT_ROWS, tf), lambda f, b, fi, pi, pe, meta: (b, f)),
            scratch_shapes=[pltpu.VMEM((UP_SLOTS, 2, D_MODEL, tf), F32), pltpu.SemaphoreType.DMA((UP_SLOTS,))],
        ),
        out_shape=jax.ShapeDtypeStruct((cap, D_EXPERT), BF16),
        compiler_params=_cparams(("arbitrary", "arbitrary")),
        name="expert_up",
    )(*sched, xs, w_ei)


def _expert_down_kernel(layer, first_ref, pidx_ref, pe_ref, meta_ref, h_ref, w_hbm, o_ref, stage, sems):
    b = pl.program_id(0)

    def panel_copies(fq, k, slot):
        del fq
        return [pltpu.make_async_copy(w_hbm.at[layer, pe_ref[k]], stage.at[slot], sems.at[slot])]

    slot = _stream_panels(first_ref, pidx_ref, meta_ref, 0, 1, b, DOWN_SLOTS, panel_copies)

    @pl.when(b < meta_ref[0])
    def _():
        o_ref[...] = _mm(h_ref[...], stage[slot])

    @pl.when(b >= meta_ref[0])
    def _():
        o_ref[...] = jnp.zeros_like(o_ref)


def _expert_down(hs, w_eo, layer, sched):
    cap = hs.shape[0]
    nb = cap // EXPERT_ROWS
    return pl.pallas_call(
        functools.partial(_expert_down_kernel, layer),
        grid_spec=pltpu.PrefetchScalarGridSpec(
            num_scalar_prefetch=4,
            grid=(nb,),
            in_specs=[
                pl.BlockSpec((EXPERT_ROWS, D_EXPERT), lambda b, fi, pi, pe, meta: (jnp.minimum(b, meta[0] - 1), 0)),
                pl.BlockSpec(memory_space=pl.ANY),
            ],
            out_specs=pl.BlockSpec((EXPERT_ROWS, D_MODEL), lambda b, fi, pi, pe, meta: (b, 0)),
            scratch_shapes=[pltpu.VMEM((DOWN_SLOTS, D_EXPERT, D_MODEL), F32), pltpu.SemaphoreType.DMA((DOWN_SLOTS,))],
        ),
        out_shape=jax.ShapeDtypeStruct((cap, D_MODEL), F32),
        compiler_params=_cparams(("arbitrary",)),
        name="expert_down",
    )(*sched, hs, w_eo)


def _gather_rows(idx_ref, idx_base, idx_stride, n_rows, src_hbm, dst, sem):
    def body(r, c):
        row = idx_ref[idx_base + r * idx_stride]
        pltpu.make_async_copy(src_hbm.at[pl.ds(row, 1)], dst.at[pl.ds(r, 1)], sem).start()
        return c

    lax.fori_loop(0, n_rows, body, 0, unroll=8)


def _wait_rows(n_rows, src_hbm, dst, sem):
    pltpu.make_async_copy(src_hbm.at[pl.ds(0, n_rows)], dst, sem).wait()


def _combine_kernel(tm, split, dest_ref, x_ref, gate_ref, g_ref, ys_hbm, *rest):
    if split is None:
        x2_ref, xn_ref, ybuf, sem = rest
    else:
        yp_ref, ysm_ref, ybuf, sem = rest
    i = pl.program_id(0)

    def issue(tile):
        slot = tile % 2
        for k in range(TOP_K):
            _gather_rows(dest_ref, tile * tm * TOP_K + k, TOP_K, tm, ys_hbm, ybuf.at[slot, k], sem.at[slot])

    @pl.when(i == 0)
    def _():
        issue(i)

    @pl.when(i + 1 < pl.num_programs(0))
    def _():
        issue(i + 1)

    slot = i % 2
    for k in range(TOP_K):
        _wait_rows(tm, ys_hbm, ybuf.at[slot, k], sem.at[slot])
    gates = gate_ref[...]
    x2 = x_ref[...] + (ybuf[slot, 0] * gates[:, 0:1] + ybuf[slot, 1] * gates[:, 1:2])
    if split is None:
        x2_ref[...] = x2
        xn_ref[...] = _rms(x2, g_ref[...]).astype(xn_ref.dtype)
    else:
        y = _rms(x2, g_ref[...])

        @pl.when(i < split)
        def _():
            yp_ref[...] = y

        @pl.when(i >= split)
        def _():
            ysm_ref[...] = y


def _combine(x1, gates, dest_flat, ys, g, final_split=None, tm=256):
    t = x1.shape[0]
    in_specs = [
        pl.BlockSpec((tm, D_MODEL), lambda i, d: (i, 0)),
        pl.BlockSpec((tm, LANES), lambda i, d: (i, 0)),
        pl.BlockSpec((1, D_MODEL), lambda i, d: (0, 0)),
        pl.BlockSpec(memory_space=pl.ANY),
    ]
    if final_split is None:
        split = None
        out_specs = [pl.BlockSpec((tm, D_MODEL), lambda i, d: (i, 0)),
                     pl.BlockSpec((tm, D_MODEL), lambda i, d: (i, 0))]
        out_shape = [jax.ShapeDtypeStruct((t, D_MODEL), F32), jax.ShapeDtypeStruct((t, D_MODEL), BF16)]
    else:
        split = final_split // tm
        out_specs = [pl.BlockSpec((tm, D_MODEL), lambda i, d: (jnp.minimum(i, split - 1), 0)),
                     pl.BlockSpec((tm, D_MODEL), lambda i, d: (jnp.maximum(i - split, 0), 0))]
        out_shape = [jax.ShapeDtypeStruct((final_split, D_MODEL), F32),
                     jax.ShapeDtypeStruct((t - final_split, D_MODEL), F32)]
    return pl.pallas_call(
        functools.partial(_combine_kernel, tm, split),
        grid_spec=pltpu.PrefetchScalarGridSpec(
            num_scalar_prefetch=1,
            grid=(t // tm,),
            in_specs=in_specs,
            out_specs=out_specs,
            scratch_shapes=[pltpu.VMEM((2, TOP_K, tm, D_MODEL), F32), pltpu.SemaphoreType.DMA((2,))],
        ),
        out_shape=out_shape,
        compiler_params=_cparams(("arbitrary",)),
        name="combine",
    )(dest_flat, x1, gates, g, ys)


def _routing_tables(eid, n_blocks):
    e_flat = eid.reshape(-1)
    onehot = (e_flat[:, None] == jnp.arange(N_EXPERTS, dtype=jnp.int32)[None, :]).astype(jnp.int32)
    csum = jnp.cumsum(onehot, axis=0)
    rank = jnp.sum(csum * onehot, axis=1) - 1
    counts = csum[-1]
    blocks = (counts + EXPERT_ROWS - 1) // EXPERT_ROWS
    blk_end = jnp.cumsum(blocks)
    blk_start = blk_end - blocks
    dest = (blk_start[e_flat] * EXPERT_ROWS + rank).astype(jnp.int32)
    n_used = blk_end[-1]

    has = blocks > 0
    order = jnp.cumsum(has.astype(jnp.int32)) - 1
    panel_expert = jnp.argsort(jnp.logical_not(has), stable=True)
    blk_ids = jnp.arange(n_blocks, dtype=jnp.int32)
    blk_e = jnp.minimum(jnp.searchsorted(blk_end, blk_ids, side="right"), N_EXPERTS - 1)
    used = blk_ids < n_used
    first = jnp.logical_and(used, blk_ids == blk_start[blk_e])
    pidx = jnp.where(used, order[blk_e], 0)
    meta = jnp.stack([n_used, jnp.sum(has.astype(jnp.int32))])
    i32 = lambda a: a.astype(jnp.int32)
    return dest, (i32(first), i32(pidx), i32(panel_expert), i32(meta))


def _gate_tables(w_spatial, b_spatial, sample_len):
    depth = w_spatial.shape[0]
    tril = jnp.tril(jnp.ones((CHUNK, CHUNK), dtype=bool))
    full = jnp.where(tril[None, None], w_spatial, 0)
    reps = CHUNK // sample_len
    small = jnp.where(tril[None, None, :sample_len, :sample_len], w_spatial[:, :, :sample_len, :sample_len], 0)
    eye = jnp.eye(reps, dtype=w_spatial.dtype)
    blockdiag = jnp.einsum("ab,lhts->lhatbs", eye, small).reshape(depth, SG_HEADS, CHUNK, CHUNK)
    mats = jnp.stack([full, blockdiag], axis=1).astype(BF16)
    bias_full = jnp.transpose(b_spatial, (0, 2, 1))
    bias_small = jnp.tile(bias_full[:, :sample_len, :], (1, reps, 1))
    bias = jnp.stack([bias_full, bias_small], axis=1)
    bias = jnp.repeat(bias, SG_HEAD_DIM, axis=-1)
    return mats, bias


def kernel(x_prompt, x_sample, state_pool, state_conv, norm_mix, w_in, w_pool_group, pool_scale, sg_norm_g, sg_norm_b, w_spatial, b_spatial, conv_w, w_branch, w_out, norm_ffn, w_router_group, b_router_group, w_router_expert, b_router_expert, w_expert_in, w_expert_out, norm_final):
    depth = w_in.shape[0]
    bp, lp, _ = x_prompt.shape
    bs, ls, _ = x_sample.shape
    tp, ts = bp * lp, bs * ls
    t = tp + ts
    assert lp % ROW_TILE == 0 and ROW_TILE % ls == 0 and bs % (ROW_TILE // ls) == 0 and ls >= CONV_STATE

    xp = x_prompt.reshape(tp, D_MODEL)
    xs = x_sample.reshape(ts, D_MODEL)

    gate_mats, gate_bias = _gate_tables(w_spatial, b_spatial, ls)
    wg_bf = w_pool_group.astype(BF16)
    vec = lambda a: a.reshape(depth, 1, -1)
    wr = jnp.concatenate([w_router_group, w_router_expert,
                          jnp.zeros((depth, D_MODEL, LANES - N_EXPERT_GROUPS - N_EXPERTS), F32)], axis=-1).astype(BF16)
    br = jnp.concatenate([b_router_group, b_router_expert,
                          jnp.zeros((depth, LANES - N_EXPERT_GROUPS - N_EXPERTS), F32)], axis=-1).reshape(depth, 1, LANES)
    pool_hist = jnp.pad(state_pool, ((0, 0), (0, 0), (POOL_HALO - POOL_STATE, 0), (0, 0))).reshape(
        depth, bs * POOL_HALO, MIX_WIDTH)
    conv_hist = jnp.pad(state_conv, ((0, 0), (0, 0), (CONV_HALO - CONV_STATE, 0), (0, 0))).reshape(
        depth, bs * CONV_HALO, MIX_WIDTH)

    n_assign = t * TOP_K
    n_blocks = n_assign // EXPERT_ROWS + N_EXPERTS

    pool_p, pool_s, conv_p, conv_s, v_out = [], [], [], [], []
    xs_sorted = jnp.zeros((n_blocks * EXPERT_ROWS, PACKED_WIDTH), jnp.uint32)
    x_parts = (xp, xs)
    xn = _norm_in(xp, xs, norm_mix[0].reshape(1, D_MODEL))
    y_p = y_s = None
    for l in range(depth):
        proj = _in_proj(xn, w_in, l)
        oa, ob, oc, z_tail, a_tail, v_s, z_s = _branches(
            proj, l, bp, lp, bs, ls, pool_hist, conv_hist, wg_bf, vec(pool_scale), vec(sg_norm_g),
            vec(sg_norm_b), gate_mats, gate_bias, conv_w)
        merged = _merge(oa, ob, oc, w_branch, proj, l)
        x1 = _out_proj(merged, w_out, l, x_parts)

        xn2, eid, gates = _router(x1, norm_ffn[l].reshape(1, D_MODEL), wr[l], br[l])
        dest, sched = _routing_tables(eid[:, :TOP_K], n_blocks)
        xs_sorted = _dispatch(xn2, dest, xs_sorted)
        hs = _expert_up(xs_sorted, w_expert_in, l, sched)
        ys = _expert_down(hs, w_expert_out, l, sched)
        if l + 1 < depth:
            x2, xn = _combine(x1, gates, dest, ys, norm_mix[l + 1].reshape(1, D_MODEL))
            x_parts = (x2,)
        else:
            y_p, y_s = _combine(x1, gates, dest, ys, norm_final.reshape(1, D_MODEL), final_split=tp)

        a_s = proj[tp:, :MIX_WIDTH].reshape(bs, ls, MIX_WIDTH)
        pool_p.append(a_tail[:, POOL_HALO - POOL_STATE:])
        pool_s.append(jnp.concatenate([state_pool[l], a_s], axis=1)[:, -POOL_STATE:])
        conv_p.append(z_tail[:, CONV_HALO - CONV_STATE:])
        conv_s.append(z_s.reshape(bs, ls, MIX_WIDTH)[:, ls - CONV_STATE:])
        v_out.append(v_s.reshape(bs, ls, MIX_WIDTH))

    return (y_p.reshape(bp, lp, D_MODEL), y_s.reshape(bs, ls, D_MODEL), jnp.stack(pool_p), jnp.stack(pool_s),
            jnp.stack(conv_p), jnp.stack(conv_s), jnp.stack(v_out))
```

```python
import functools

import jax
import jax.numpy as jnp
from jax import lax
from jax.experimental import pallas as pl
from jax.experimental.pallas import tpu as pltpu

F32 = jnp.float32
BF16 = jnp.bfloat16

D_MODEL = 4096
MIX_WIDTH = D_MODEL // 2
POOL_GROUPS = 4
POOL_WINDOWS = (2, 4, 8, 16)
POOL_GROUP_DIM = MIX_WIDTH // POOL_GROUPS
POOL_STATE = 15
SG_HEADS = 8
SG_HEAD_DIM = MIX_WIDTH // SG_HEADS
CHUNK = 128
CONV_STATE = 2
IN_COLS = MIX_WIDTH * 6 + 3 * D_MODEL
GATE_COL0 = MIX_WIDTH * 6
N_EXPERT_GROUPS = 4
EXPERTS_PER_GROUP = 8
N_EXPERTS = N_EXPERT_GROUPS * EXPERTS_PER_GROUP
TOP_K = 2
D_EXPERT = D_MODEL // 4
PAST_LEN = 16384
EPS = 1e-6

LANES = 128
SUBLANES = 8
POOL_HALO = 16
CONV_HALO = 8
ROW_TILE = 128
EXPERT_ROWS = 256
PACKED_WIDTH = D_MODEL // 2
VMEM_LIMIT = 56 * 1024 * 1024


def _cparams(sem):
    return pltpu.CompilerParams(dimension_semantics=sem, vmem_limit_bytes=VMEM_LIMIT)


def _gelu_tanh(x):
    c = 0.7978845608028654
    return 0.5 * x * (1.0 + jnp.tanh(c * (x + 0.044715 * (x * x * x))))


def _rms(x, g):
    return x * lax.rsqrt(jnp.mean(x * x, axis=-1, keepdims=True) + EPS) * g


def _norm_in_kernel(n_prompt_tiles, xp_ref, xs_ref, g_ref, o_ref):
    i = pl.program_id(0)

    @pl.when(i < n_prompt_tiles)
    def _():
        o_ref[...] = _rms(xp_ref[...], g_ref[...]).astype(BF16)

    @pl.when(i >= n_prompt_tiles)
    def _():
        o_ref[...] = _rms(xs_ref[...], g_ref[...]).astype(BF16)


def _norm_in(xp, xs, g, tm=512):
    tp, ts = xp.shape[0], xs.shape[0]
    npt, nst = tp // tm, ts // tm
    return pl.pallas_call(
        functools.partial(_norm_in_kernel, npt),
        grid=(npt + nst,),
        in_specs=[
            pl.BlockSpec((tm, D_MODEL), lambda i: (jnp.minimum(i, npt - 1), 0)),
            pl.BlockSpec((tm, D_MODEL), lambda i: (jnp.maximum(i - npt, 0), 0)),
            pl.BlockSpec((1, D_MODEL), lambda i: (0, 0)),
        ],
        out_specs=pl.BlockSpec((tm, D_MODEL), lambda i: (i, 0)),
        out_shape=jax.ShapeDtypeStruct((tp + ts, D_MODEL), BF16),
        compiler_params=_cparams(("arbitrary",)),
        name="norm_in",
    )(xp, xs, g)


def _mm(a, w):
    return lax.dot_general(a, w, (((1,), (0,)), ((), ())), preferred_element_type=F32)


def _panel_matmul_kernel(layer, tn, a_ref, w_hbm, o_ref, wbuf, sems):
    j = pl.program_id(0)
    i = pl.program_id(1)

    def panel_copy(panel):
        cols = pl.ds(pl.multiple_of(panel * tn, tn), tn)
        return pltpu.make_async_copy(w_hbm.at[layer, :, cols], wbuf.at[panel % 2], sems.at[panel % 2])

    @pl.when(jnp.logical_and(j == 0, i == 0))
    def _():
        panel_copy(j).start()

    @pl.when(i == 0)
    def _():
        panel_copy(j).wait()

        @pl.when(j + 1 < pl.num_programs(0))
        def _():
            panel_copy(j + 1).start()

    o_ref[...] = _mm(a_ref[...], wbuf[j % 2])


def _in_proj(xn, w_in, layer, tm=768, tn=1024):
    t = xn.shape[0]
    return pl.pallas_call(
        functools.partial(_panel_matmul_kernel, layer, tn),
        grid=(IN_COLS // tn, t // tm),
        in_specs=[
            pl.BlockSpec((tm, D_MODEL), lambda j, i: (i, 0)),
            pl.BlockSpec(memory_space=pl.ANY),
        ],
        out_specs=pl.BlockSpec((tm, tn), lambda j, i: (i, j)),
        out_shape=jax.ShapeDtypeStruct((t, IN_COLS), F32),
        scratch_shapes=[pltpu.VMEM((2, D_MODEL, tn), F32), pltpu.SemaphoreType.DMA((2,))],
        compiler_params=_cparams(("arbitrary", "arbitrary")),
        name="in_proj",
    )(xn, w_in)


def _window_rows(ext, n_seq, halo, rows):
    if n_seq == 1:
        return ext[halo:, :]
    c = ext.shape[-1]
    return ext.reshape(n_seq, halo + rows, c)[:, halo:, :].reshape(n_seq * rows, c)


def _stack_history(hist, cur, n_seq, halo, rows):
    if n_seq == 1:
        return jnp.concatenate([hist, cur], axis=0)
    c = cur.shape[-1]
    ext = jnp.concatenate([hist.reshape(n_seq, halo, c), cur.reshape(n_seq, rows, c)], axis=1)
    return ext.reshape(n_seq * (halo + rows), c)


def _branch_math(n_seq, rows, pos, a_ref, u_ref, v_ref, ci_ref, cb_ref, cc_ref, pool_hist, conv_hist,
                 wg_ref, ps_ref, lng_ref, lnb_ref, m_ref, bias_ref, cw_ref, oa_ref, ob_ref, oc_ref):
    for g, w in enumerate(POOL_WINDOWS):
        cols = slice(g * POOL_GROUP_DIM, (g + 1) * POOL_GROUP_DIM)
        a_g = a_ref[:, cols]
        s = _stack_history(pool_hist[:, cols], a_g, n_seq, POOL_HALO, rows)
        k = 1
        while k < w:
            s = s + pltpu.roll(s, k, 0)
            k *= 2
        s = _window_rows(s, n_seq, POOL_HALO, rows)
        cnt = jnp.minimum(w, pos + 1).astype(F32)
        pooled = (s / cnt - a_g).astype(BF16)
        out = jnp.dot(pooled, wg_ref[g], preferred_element_type=F32) * ps_ref[:, cols]
        oa_ref[:, cols] = out.astype(BF16)

    vg = _gelu_tanh(v_ref[...])
    mu = jnp.mean(vg, axis=-1, keepdims=True)
    xc = vg - mu
    v = xc * lax.rsqrt(jnp.mean(xc * xc, axis=-1, keepdims=True) + EPS) * lng_ref[...] + lnb_ref[...]
    vb = v.astype(BF16)
    for h in range(SG_HEADS):
        cols = slice(h * SG_HEAD_DIM, (h + 1) * SG_HEAD_DIM)
        sp = jnp.dot(m_ref[h], vb[:, cols], preferred_element_type=F32) + bias_ref[:, cols]
        ob_ref[:, cols] = (_gelu_tanh(u_ref[:, cols]) * sp).astype(BF16)

    z = cc_ref[...] * ci_ref[...]
    e = _stack_history(conv_hist, z, n_seq, CONV_HALO, rows)
    y = cw_ref[0:1, :] * pltpu.roll(e, 2, 0)
    y = y + cw_ref[1:2, :] * pltpu.roll(e, 1, 0)
    y = y + cw_ref[2:3, :] * e
    oc_ref[...] = (cb_ref[...] * _window_rows(y, n_seq, CONV_HALO, rows)).astype(BF16)
    return v, z


SLAB_RING = 4


def _branch_kernel(tiles_per_seq, n_prompt_tiles, seq_per_tile, sample_len,
                   proj_hbm, ph_ref, ch_ref, wg_ref, ps_ref, lng_ref, lnb_ref,
                   m_ref, bias_ref, cw_ref, oa_ref, ob_ref, oc_ref, zl_ref, al_ref, vo_ref, zo_ref,
                   pool_hist, conv_hist, slabs, slab_sems):
    i = pl.program_id(0)
    n_tiles = pl.num_programs(0)
    shared = (wg_ref, ps_ref, lng_ref, lnb_ref, m_ref, bias_ref, cw_ref, oa_ref, ob_ref, oc_ref)

    def slab_copies(tile):
        slot = tile % SLAB_RING
        first = tile * ROW_TILE
        rows = pl.ds(first if isinstance(first, int) else pl.multiple_of(first, ROW_TILE), ROW_TILE)
        return [pltpu.make_async_copy(proj_hbm.at[rows, pl.ds(c * MIX_WIDTH, MIX_WIDTH)], slabs.at[slot, c],
                                      slab_sems.at[slot]) for c in range(6)]

    @pl.when(i == 0)
    def _():
        for tile in range(SLAB_RING - 1):
            for c in slab_copies(tile):
                c.start()

    @pl.when(i + SLAB_RING - 1 < n_tiles)
    def _():
        for c in slab_copies(i + SLAB_RING - 1):
            c.start()

    for c in slab_copies(i):
        c.wait()
    cur = i % SLAB_RING
    a_ref, u_ref, v_ref, ci_ref, cb_ref, cc_ref = (slabs.at[cur, c] for c in range(6))

    @pl.when(i < n_prompt_tiles)
    def _():
        t = i % tiles_per_seq

        @pl.when(t == 0)
        def _():
            pool_hist[...] = jnp.zeros_like(pool_hist)
            conv_hist[...] = jnp.zeros_like(conv_hist)

        pos = t * ROW_TILE + lax.broadcasted_iota(jnp.int32, (ROW_TILE, 1), 0)
        _, z = _branch_math(1, ROW_TILE, pos, a_ref, u_ref, v_ref, ci_ref, cb_ref, cc_ref, pool_hist[...],
                            conv_hist[...], *shared)
        pool_hist[...] = a_ref[ROW_TILE - POOL_HALO:, :]
        conv_hist[...] = z[ROW_TILE - CONV_HALO:, :]
        zl_ref[...] = z[ROW_TILE - CONV_HALO:, :]
        al_ref[...] = a_ref[ROW_TILE - POOL_HALO:, :]

    @pl.when(i >= n_prompt_tiles)
    def _():
        r = lax.broadcasted_iota(jnp.int32, (seq_per_tile * sample_len, 1), 0)
        pos = PAST_LEN + (r % sample_len)
        v, z = _branch_math(seq_per_tile, sample_len, pos, a_ref, u_ref, v_ref, ci_ref, cb_ref, cc_ref,
                            ph_ref[...], ch_ref[...], *shared)
        vo_ref[...] = v
        zo_ref[...] = z


def _branches(proj, layer, n_prompt_seq, prompt_len, n_sample_seq, sample_len, pool_state, conv_state,
              wg_bf, pool_scale, ln_g, ln_b, gate_mats, gate_bias, conv_w):
    t_total = proj.shape[0]
    tiles_per_seq = prompt_len // ROW_TILE
    npt = n_prompt_seq * tiles_per_seq
    seq_per_tile = ROW_TILE // sample_len
    nst = n_sample_seq // seq_per_tile
    sample_rows = n_sample_seq * sample_len
    act = jax.ShapeDtypeStruct((t_total, MIX_WIDTH), BF16)
    sample_f32 = jax.ShapeDtypeStruct((sample_rows, MIX_WIDTH), F32)

    def sample_tile(i):
        return jnp.maximum(i - npt, 0)

    def kind(i):
        return jnp.where(i >= npt, 1, 0)

    vec_spec = pl.BlockSpec((None, 1, MIX_WIDTH), lambda i: (layer, 0, 0))
    in_specs = [
        pl.BlockSpec(memory_space=pl.ANY),
        pl.BlockSpec((None, seq_per_tile * POOL_HALO, MIX_WIDTH), lambda i: (layer, sample_tile(i), 0)),
        pl.BlockSpec((None, seq_per_tile * CONV_HALO, MIX_WIDTH), lambda i: (layer, sample_tile(i), 0)),
        pl.BlockSpec((None, POOL_GROUPS, POOL_GROUP_DIM, POOL_GROUP_DIM), lambda i: (layer, 0, 0, 0)),
        vec_spec, vec_spec, vec_spec,
        pl.BlockSpec((None, None, SG_HEADS, CHUNK, CHUNK), lambda i: (layer, kind(i), 0, 0, 0)),
        pl.BlockSpec((None, None, CHUNK, MIX_WIDTH), lambda i: (layer, kind(i), 0, 0)),
        pl.BlockSpec((None, 3, MIX_WIDTH), lambda i: (layer, 0, 0)),
    ]
    row_spec = pl.BlockSpec((ROW_TILE, MIX_WIDTH), lambda i: (i, 0))
    out_specs = [
        row_spec, row_spec, row_spec,
        pl.BlockSpec((None, CONV_HALO, MIX_WIDTH), lambda i: (jnp.minimum(i // tiles_per_seq, n_prompt_seq - 1), 0, 0)),
        pl.BlockSpec((None, POOL_HALO, MIX_WIDTH), lambda i: (jnp.minimum(i // tiles_per_seq, n_prompt_seq - 1), 0, 0)),
        pl.BlockSpec((ROW_TILE, MIX_WIDTH), lambda i: (sample_tile(i), 0)),
        pl.BlockSpec((ROW_TILE, MIX_WIDTH), lambda i: (sample_tile(i), 0)),
    ]
    return pl.pallas_call(
        functools.partial(_branch_kernel, tiles_per_seq, npt, seq_per_tile, sample_len),
        grid=(npt + nst,),
        in_specs=in_specs,
        out_specs=out_specs,
        out_shape=[act, act, act, jax.ShapeDtypeStruct((n_prompt_seq, CONV_HALO, MIX_WIDTH), F32),
                   jax.ShapeDtypeStruct((n_prompt_seq, POOL_HALO, MIX_WIDTH), F32), sample_f32, sample_f32],
        scratch_shapes=[pltpu.VMEM((POOL_HALO, MIX_WIDTH), F32), pltpu.VMEM((CONV_HALO, MIX_WIDTH), F32),
                        pltpu.VMEM((SLAB_RING, 6, ROW_TILE, MIX_WIDTH), F32), pltpu.SemaphoreType.DMA((SLAB_RING,))],
        compiler_params=_cparams(("arbitrary",)),
        name="branches",
    )(proj, pool_state, conv_state, wg_bf, pool_scale, ln_g, ln_b,
      gate_mats, gate_bias, conv_w)


def _merge_kernel(oa_ref, ob_ref, oc_ref, wa_ref, wb_ref, wc_ref, ga_ref, gb_ref, gc_ref, o_ref):
    m = jax.nn.sigmoid(ga_ref[...]) * _mm(oa_ref[...], wa_ref[...])
    m = m + jax.nn.sigmoid(gb_ref[...]) * _mm(ob_ref[...], wb_ref[...])
    m = m + jax.nn.sigmoid(gc_ref[...]) * _mm(oc_ref[...], wc_ref[...])
    o_ref[...] = m.astype(BF16)


def _merge(oa, ob, oc, w_branch, proj, layer, tm=768, tn=512):
    t = oa.shape[0]
    act_spec = pl.BlockSpec((tm, MIX_WIDTH), lambda j, i: (i, 0))

    def wspec(b):
        return pl.BlockSpec((None, None, MIX_WIDTH, tn), lambda j, i: (layer, b, 0, j))

    def gspec(b):
        off = (GATE_COL0 + b * D_MODEL) // tn
        return pl.BlockSpec((tm, tn), lambda j, i: (i, off + j))

    return pl.pallas_call(
        _merge_kernel,
        grid=(D_MODEL // tn, t // tm),
        in_specs=[act_spec, act_spec, act_spec, wspec(0), wspec(1), wspec(2), gspec(0), gspec(1), gspec(2)],
        out_specs=pl.BlockSpec((tm, tn), lambda j, i: (i, j)),
        out_shape=jax.ShapeDtypeStruct((t, D_MODEL), BF16),
        compiler_params=_cparams(("arbitrary", "arbitrary")),
        name="merge",
    )(oa, ob, oc, w_branch, w_branch, w_branch, proj, proj, proj)


def _out_kernel2(n_prompt_tiles, a_ref, w_ref, xp_ref, xs_ref, o_ref):
    i = pl.program_id(1)
    h = _mm(a_ref[...], w_ref[...])

    @pl.when(i < n_prompt_tiles)
    def _():
        o_ref[...] = xp_ref[...] + h

    @pl.when(i >= n_prompt_tiles)
    def _():
        o_ref[...] = xs_ref[...] + h


def _out_kernel1(a_ref, w_ref, x_ref, o_ref):
    o_ref[...] = x_ref[...] + _mm(a_ref[...], w_ref[...])


def _out_proj(merged, w_out, layer, x_parts, tm=1024, tn=512):
    t = merged.shape[0]
    common = dict(
        grid=(D_MODEL // tn, t // tm),
        out_specs=pl.BlockSpec((tm, tn), lambda j, i: (i, j)),
        out_shape=jax.ShapeDtypeStruct((t, D_MODEL), F32),
        compiler_params=_cparams(("arbitrary", "arbitrary")),
        name="out_proj",
    )
    a_spec = pl.BlockSpec((tm, D_MODEL), lambda j, i: (i, 0))
    w_spec = pl.BlockSpec((None, D_MODEL, tn), lambda j, i: (layer, 0, j))
    if len(x_parts) == 1:
        return pl.pallas_call(
            _out_kernel1,
            in_specs=[a_spec, w_spec, pl.BlockSpec((tm, tn), lambda j, i: (i, j))],
            **common,
        )(merged, w_out, x_parts[0])
    xp, xs = x_parts
    npt = xp.shape[0] // tm
    return pl.pallas_call(
        functools.partial(_out_kernel2, npt),
        in_specs=[
            a_spec, w_spec,
            pl.BlockSpec((tm, tn), lambda j, i: (jnp.minimum(i, npt - 1), j)),
            pl.BlockSpec((tm, tn), lambda j, i: (jnp.maximum(i - npt, 0), j)),
        ],
        **common,
    )(merged, w_out, xp, xs)


def _router_kernel(x_ref, g_ref, wr_ref, br_ref, xn_ref, eid_ref, gate_ref):
    xb = _rms(x_ref[...], g_ref[...]).astype(BF16)
    lo = pltpu.bitcast(xb[:, :PACKED_WIDTH].astype(F32), jnp.uint32) >> 16
    hi = pltpu.bitcast(xb[:, PACKED_WIDTH:].astype(F32), jnp.uint32) & jnp.uint32(0xFFFF0000)
    xn_ref[...] = hi | lo
    logits = jnp.dot(xb, wr_ref[...], preferred_element_type=F32) + br_ref[...]
    tm = logits.shape[0]
    lane = lax.broadcasted_iota(jnp.int32, (tm, LANES), 1)
    neg = jnp.float32(-jnp.inf)
    far = jnp.int32(LANES)

    is_g = lane < N_EXPERT_GROUPS
    glog = jnp.where(is_g, logits, neg)
    gmax = jnp.max(glog, axis=1, keepdims=True)
    gsel = jnp.min(jnp.where(glog == gmax, lane, far), axis=1, keepdims=True)
    gsum = jnp.sum(jnp.where(is_g, jnp.exp(glog - gmax), 0.0), axis=1, keepdims=True)
    gp = 1.0 / gsum

    lo = N_EXPERT_GROUPS + gsel * EXPERTS_PER_GROUP
    in_grp = jnp.logical_and(lane >= lo, lane < lo + EXPERTS_PER_GROUP)
    el = jnp.where(in_grp, logits, neg)
    m1 = jnp.max(el, axis=1, keepdims=True)
    i1 = jnp.min(jnp.where(el == m1, lane, far), axis=1, keepdims=True)
    el2 = jnp.where(lane == i1, neg, el)
    m2 = jnp.max(el2, axis=1, keepdims=True)
    i2 = jnp.min(jnp.where(el2 == m2, lane, far), axis=1, keepdims=True)
    e2 = jnp.exp(m2 - m1)
    den = 1.0 + e2
    g1 = gp * (1.0 / den)
    g2 = gp * (e2 / den)

    eid_ref[...] = jnp.where(lane == 0, i1 - N_EXPERT_GROUPS, jnp.where(lane == 1, i2 - N_EXPERT_GROUPS, 0))
    gate_ref[...] = jnp.where(lane == 0, g1, jnp.where(lane == 1, g2, 0.0))


def _router(x1, g, wr, br, tm=512):
    t = x1.shape[0]
    return pl.pallas_call(
        _router_kernel,
        grid=(t // tm,),
        in_specs=[
            pl.BlockSpec((tm, D_MODEL), lambda i: (i, 0)),
            pl.BlockSpec((1, D_MODEL), lambda i: (0, 0)),
            pl.BlockSpec((D_MODEL, LANES), lambda i: (0, 0)),
            pl.BlockSpec((1, LANES), lambda i: (0, 0)),
        ],
        out_specs=[
            pl.BlockSpec((tm, PACKED_WIDTH), lambda i: (i, 0)),
            pl.BlockSpec((tm, LANES), lambda i: (i, 0)),
            pl.BlockSpec((tm, LANES), lambda i: (i, 0)),
        ],
        out_shape=[
            jax.ShapeDtypeStruct((t, PACKED_WIDTH), jnp.uint32),
            jax.ShapeDtypeStruct((t, LANES), jnp.int32),
            jax.ShapeDtypeStruct((t, LANES), F32),
        ],
        compiler_params=_cparams(("arbitrary",)),
        name="router",
    )(x1, g, wr, br)


def _dispatch_kernel(tm, dest_ref, x_ref, xs_in, xs_hbm, sem):
    del xs_in
    base = pl.program_id(0) * tm

    def body(r, c):
        for k in range(TOP_K):
            slot_row = dest_ref[(base + r) * TOP_K + k]
            pltpu.make_async_copy(x_ref.at[pl.ds(r, 1)], xs_hbm.at[pl.ds(slot_row, 1)], sem).start()
        return c

    lax.fori_loop(0, tm, body, 0, unroll=4)
    for _ in range(TOP_K):
        pltpu.make_async_copy(x_ref, xs_hbm.at[pl.ds(0, tm)], sem).wait()


def _dispatch(xn_packed, dest_flat, xs_init, tm=512):
    t = xn_packed.shape[0]
    cap = xs_init.shape[0]
    return pl.pallas_call(
        functools.partial(_dispatch_kernel, tm),
        grid_spec=pltpu.PrefetchScalarGridSpec(
            num_scalar_prefetch=1,
            grid=(t // tm,),
            in_specs=[pl.BlockSpec((tm, PACKED_WIDTH), lambda i, d: (i, 0)), pl.BlockSpec(memory_space=pl.ANY)],
            out_specs=pl.BlockSpec(memory_space=pl.ANY),
            scratch_shapes=[pltpu.SemaphoreType.DMA],
        ),
        out_shape=jax.ShapeDtypeStruct((cap, PACKED_WIDTH), jnp.uint32),
        input_output_aliases={2: 0},
        compiler_params=_cparams(("arbitrary",)),
        name="dispatch",
    )(dest_flat, xn_packed, xs_init)


def _stream_panels(first_ref, pidx_ref, meta_ref, pass_idx, n_pass, b, n_slots, panel_copies):
    n_distinct = meta_ref[1]
    n_panels = n_pass * n_distinct
    ahead = n_slots - 1

    def start(q):
        for c in panel_copies(q // n_distinct, q % n_distinct, q % n_slots):
            c.start()

    @pl.when(jnp.logical_and(pass_idx == 0, b == 0))
    def _():
        for q in range(ahead):
            @pl.when(q < n_panels)
            def _():
                start(q)

    p = pass_idx * n_distinct + pidx_ref[b]
    slot = p % n_slots

    @pl.when(first_ref[b] == 1)
    def _():
        for c in panel_copies(pass_idx, pidx_ref[b], slot):
            c.wait()

        @pl.when(p + ahead < n_panels)
        def _():
            start(p + ahead)

    return slot


UP_SLOTS = 3
DOWN_SLOTS = 2


def _expert_up_kernel(layer, tf, first_ref, pidx_ref, pe_ref, meta_ref, x_ref, w_hbm, o_ref, stage, sems):
    f = pl.program_id(0)
    b = pl.program_id(1)

    def panel_copies(fq, k, slot):
        e = pe_ref[k]
        col = pl.multiple_of(fq * tf, tf)
        return [pltpu.make_async_copy(w_hbm.at[layer, e, :, pl.ds(half * D_EXPERT + col, tf)],
                                      stage.at[slot, half], sems.at[slot]) for half in range(2)]

    slot = _stream_panels(first_ref, pidx_ref, meta_ref, f, pl.num_programs(0), b, UP_SLOTS, panel_copies)

    @pl.when(b < meta_ref[0])
    def _():
        words = x_ref[...]
        lo = pltpu.bitcast(words << 16, F32).astype(BF16)
        hi = pltpu.bitcast(words & jnp.uint32(0xFFFF0000), F32).astype(BF16)
        h1 = _mm(lo, stage[slot, 0, :PACKED_WIDTH, :]) + _mm(hi, stage[slot, 0, PACKED_WIDTH:, :])
        h3 = _mm(lo, stage[slot, 1, :PACKED_WIDTH, :]) + _mm(hi, stage[slot, 1, PACKED_WIDTH:, :])
        o_ref[...] = (h1 * jax.nn.sigmoid(h1) * h3).astype(BF16)

    @pl.when(b >= meta_ref[0])
    def _():
        o_ref[...] = jnp.zeros_like(o_ref)


def _expert_up(xs, w_ei, layer, sched, tf=512):
    cap = xs.shape[0]
    nb = cap // EXPERT_ROWS
    nf = D_EXPERT // tf
    return pl.pallas_call(
        functools.partial(_expert_up_kernel, layer, tf),
        grid_spec=pltpu.PrefetchScalarGridSpec(
            num_scalar_prefetch=4,
            grid=(nf, nb),
            in_specs=[
                pl.BlockSpec((EXPERT_ROWS, PACKED_WIDTH), lambda f, b, fi, pi, pe, meta: (jnp.minimum(b, meta[0] - 1), 0)),
                pl.BlockSpec(memory_space=pl.ANY),
            ],
            out_specs=pl.BlockSpec((EXPERT_ROWS, tf), lambda f, b, fi, pi, pe, meta: (b, f)),
            scratch_shapes=[pltpu.VMEM((UP_SLOTS, 2, D_MODEL, tf), F32), pltpu.SemaphoreType.DMA((UP_SLOTS,))],
        ),
        out_shape=jax.ShapeDtypeStruct((cap, D_EXPERT), BF16),
        compiler_params=_cparams(("arbitrary", "arbitrary")),
        name="expert_up",
    )(*sched, xs, w_ei)


def _expert_down_kernel(layer, first_ref, pidx_ref, pe_ref, meta_ref, h_ref, w_hbm, o_ref, stage, sems):
    b = pl.program_id(0)

    def panel_copies(fq, k, slot):
        del fq
        return [pltpu.make_async_copy(w_hbm.at[layer, pe_ref[k]], stage.at[slot], sems.at[slot])]

    slot = _stream_panels(first_ref, pidx_ref, meta_ref, 0, 1, b, DOWN_SLOTS, panel_copies)

    @pl.when(b < meta_ref[0])
    def _():
        o_ref[...] = _mm(h_ref[...], stage[slot])

    @pl.when(b >= meta_ref[0])
    def _():
        o_ref[...] = jnp.zeros_like(o_ref)


def _expert_down(hs, w_eo, layer, sched):
    cap = hs.shape[0]
    nb = cap // EXPERT_ROWS
    return pl.pallas_call(
        functools.partial(_expert_down_kernel, layer),
        grid_spec=pltpu.PrefetchScalarGridSpec(
            num_scalar_prefetch=4,
            grid=(nb,),
            in_specs=[
                pl.BlockSpec((EXPERT_ROWS, D_EXPERT), lambda b, fi, pi, pe, meta: (jnp.minimum(b, meta[0] - 1), 0)),
                pl.BlockSpec(memory_space=pl.ANY),
            ],
            out_specs=pl.BlockSpec((EXPERT_ROWS, D_MODEL), lambda b, fi, pi, pe, meta: (b, 0)),
            scratch_shapes=[pltpu.VMEM((DOWN_SLOTS, D_EXPERT, D_MODEL), F32), pltpu.SemaphoreType.DMA((DOWN_SLOTS,))],
        ),
        out_shape=jax.ShapeDtypeStruct((cap, D_MODEL), F32),
        compiler_params=_cparams(("arbitrary",)),
        name="expert_down",
    )(*sched, hs, w_eo)


def _gather_rows(idx_ref, idx_base, idx_stride, n_rows, src_hbm, dst, sem):
    def body(r, c):
        row = idx_ref[idx_base + r * idx_stride]
        pltpu.make_async_copy(src_hbm.at[pl.ds(row, 1)], dst.at[pl.ds(r, 1)], sem).start()
        return c

    lax.fori_loop(0, n_rows, body, 0, unroll=8)


def _wait_rows(n_rows, src_hbm, dst, sem):
    pltpu.make_async_copy(src_hbm.at[pl.ds(0, n_rows)], dst, sem).wait()


def _combine_kernel(tm, split, dest_ref, x_ref, gate_ref, g_ref, ys_hbm, *rest):
    if split is None:
        x2_ref, xn_ref, ybuf, sem = rest
    else:
        yp_ref, ysm_ref, ybuf, sem = rest
    i = pl.program_id(0)

    def issue(tile):
        slot = tile % 2
        for k in range(TOP_K):
            _gather_rows(dest_ref, tile * tm * TOP_K + k, TOP_K, tm, ys_hbm, ybuf.at[slot, k], sem.at[slot])

    @pl.when(i == 0)
    def _():
        issue(i)

    @pl.when(i + 1 < pl.num_programs(0))
    def _():
        issue(i + 1)

    slot = i % 2
    for k in range(TOP_K):
        _wait_rows(tm, ys_hbm, ybuf.at[slot, k], sem.at[slot])
    gates = gate_ref[...]
    x2 = x_ref[...] + (ybuf[slot, 0] * gates[:, 0:1] + ybuf[slot, 1] * gates[:, 1:2])
    if split is None:
        x2_ref[...] = x2
        xn_ref[...] = _rms(x2, g_ref[...]).astype(xn_ref.dtype)
    else:
        y = _rms(x2, g_ref[...])

        @pl.when(i < split)
        def _():
            yp_ref[...] = y

        @pl.when(i >= split)
        def _():
            ysm_ref[...] = y


def _combine(x1, gates, dest_flat, ys, g, final_split=None, tm=256):
    t = x1.shape[0]
    in_specs = [
        pl.BlockSpec((tm, D_MODEL), lambda i, d: (i, 0)),
        pl.BlockSpec((tm, LANES), lambda i, d: (i, 0)),
        pl.BlockSpec((1, D_MODEL), lambda i, d: (0, 0)),
        pl.BlockSpec(memory_space=pl.ANY),
    ]
    if final_split is None:
        split = None
        out_specs = [pl.BlockSpec((tm, D_MODEL), lambda i, d: (i, 0)),
                     pl.BlockSpec((tm, D_MODEL), lambda i, d: (i, 0))]
        out_shape = [jax.ShapeDtypeStruct((t, D_MODEL), F32), jax.ShapeDtypeStruct((t, D_MODEL), BF16)]
    else:
        split = final_split // tm
        out_specs = [pl.BlockSpec((tm, D_MODEL), lambda i, d: (jnp.minimum(i, split - 1), 0)),
                     pl.BlockSpec((tm, D_MODEL), lambda i, d: (jnp.maximum(i - split, 0), 0))]
        out_shape = [jax.ShapeDtypeStruct((final_split, D_MODEL), F32),
                     jax.ShapeDtypeStruct((t - final_split, D_MODEL), F32)]
    return pl.pallas_call(
        functools.partial(_combine_kernel, tm, split),
        grid_spec=pltpu.PrefetchScalarGridSpec(
            num_scalar_prefetch=1,
            grid=(t // tm,),
            in_specs=in_specs,
            out_specs=out_specs,
            scratch_shapes=[pltpu.VMEM((2, TOP_K, tm, D_MODEL), F32), pltpu.SemaphoreType.DMA((2,))],
        ),
        out_shape=out_shape,
        compiler_params=_cparams(("arbitrary",)),
        name="combine",
    )(dest_flat, x1, gates, g, ys)


def _routing_tables(eid, n_blocks):
    e_flat = eid.reshape(-1)
    onehot = (e_flat[:, None] == jnp.arange(N_EXPERTS, dtype=jnp.int32)[None, :]).astype(jnp.int32)
    csum = jnp.cumsum(onehot, axis=0)
    rank = jnp.sum(csum * onehot, axis=1) - 1
    counts = csum[-1]
    blocks = (counts + EXPERT_ROWS - 1) // EXPERT_ROWS
    blk_end = jnp.cumsum(blocks)
    blk_start = blk_end - blocks
    dest = (blk_start[e_flat] * EXPERT_ROWS + rank).astype(jnp.int32)
    n_used = blk_end[-1]

    has = blocks > 0
    order = jnp.cumsum(has.astype(jnp.int32)) - 1
    panel_expert = jnp.argsort(jnp.logical_not(has), stable=True)
    blk_ids = jnp.arange(n_blocks, dtype=jnp.int32)
    blk_e = jnp.minimum(jnp.searchsorted(blk_end, blk_ids, side="right"), N_EXPERTS - 1)
    used = blk_ids < n_used
    first = jnp.logical_and(used, blk_ids == blk_start[blk_e])
    pidx = jnp.where(used, order[blk_e], 0)
    meta = jnp.stack([n_used, jnp.sum(has.astype(jnp.int32))])
    i32 = lambda a: a.astype(jnp.int32)
    return dest, (i32(first), i32(pidx), i32(panel_expert), i32(meta))


def _gate_tables(w_spatial, b_spatial, sample_len):
    depth = w_spatial.shape[0]
    tril = jnp.tril(jnp.ones((CHUNK, CHUNK), dtype=bool))
    full = jnp.where(tril[None, None], w_spatial, 0)
    reps = CHUNK // sample_len
    small = jnp.where(tril[None, None, :sample_len, :sample_len], w_spatial[:, :, :sample_len, :sample_len], 0)
    eye = jnp.eye(reps, dtype=w_spatial.dtype)
    blockdiag = jnp.einsum("ab,lhts->lhatbs", eye, small).reshape(depth, SG_HEADS, CHUNK, CHUNK)
    mats = jnp.stack([full, blockdiag], axis=1).astype(BF16)
    bias_full = jnp.transpose(b_spatial, (0, 2, 1))
    bias_small = jnp.tile(bias_full[:, :sample_len, :], (1, reps, 1))
    bias = jnp.stack([bias_full, bias_small], axis=1)
    bias = jnp.repeat(bias, SG_HEAD_DIM, axis=-1)
    return mats, bias


def kernel(x_prompt, x_sample, state_pool, state_conv, norm_mix, w_in, w_pool_group, pool_scale, sg_norm_g, sg_norm_b, w_spatial, b_spatial, conv_w, w_branch, w_out, norm_ffn, w_router_group, b_router_group, w_router_expert, b_router_expert, w_expert_in, w_expert_out, norm_final):
    depth = w_in.shape[0]
    bp, lp, _ = x_prompt.shape
    bs, ls, _ = x_sample.shape
    tp, ts = bp * lp, bs * ls
    t = tp + ts
    assert lp % ROW_TILE == 0 and ROW_TILE % ls == 0 and bs % (ROW_TILE // ls) == 0 and ls >= CONV_STATE

    xp = x_prompt.reshape(tp, D_MODEL)
    xs = x_sample.reshape(ts, D_MODEL)

    gate_mats, gate_bias = _gate_tables(w_spatial, b_spatial, ls)
    wg_bf = w_pool_group.astype(BF16)
    vec = lambda a: a.reshape(depth, 1, -1)
    wr = jnp.concatenate([w_router_group, w_router_expert,
                          jnp.zeros((depth, D_MODEL, LANES - N_EXPERT_GROUPS - N_EXPERTS), F32)], axis=-1).astype(BF16)
    br = jnp.concatenate([b_router_group, b_router_expert,
                          jnp.zeros((depth, LANES - N_EXPERT_GROUPS - N_EXPERTS), F32)], axis=-1).reshape(depth, 1, LANES)
    pool_hist = jnp.pad(state_pool, ((0, 0), (0, 0), (POOL_HALO - POOL_STATE, 0), (0, 0))).reshape(
        depth, bs * POOL_HALO, MIX_WIDTH)
    conv_hist = jnp.pad(state_conv, ((0, 0), (0, 0), (CONV_HALO - CONV_STATE, 0), (0, 0))).reshape(
        depth, bs * CONV_HALO, MIX_WIDTH)

    n_assign = t * TOP_K
    n_blocks = n_assign // EXPERT_ROWS + N_EXPERTS

    pool_p, pool_s, conv_p, conv_s, v_out = [], [], [], [], []
    xs_sorted = jnp.zeros((n_blocks * EXPERT_ROWS, PACKED_WIDTH), jnp.uint32)
    x_parts = (xp, xs)
    xn = _norm_in(xp, xs, norm_mix[0].reshape(1, D_MODEL))
    y_p = y_s = None
    for l in range(depth):
        proj = _in_proj(xn, w_in, l)
        oa, ob, oc, z_tail, a_tail, v_s, z_s = _branches(
            proj, l, bp, lp, bs, ls, pool_hist, conv_hist, wg_bf, vec(pool_scale), vec(sg_norm_g),
            vec(sg_norm_b), gate_mats, gate_bias, conv_w)
        merged = _merge(oa, ob, oc, w_branch, proj, l)
        x1 = _out_proj(merged, w_out, l, x_parts)

        xn2, eid, gates = _router(x1, norm_ffn[l].reshape(1, D_MODEL), wr[l], br[l])
        dest, sched = _routing_tables(eid[:, :TOP_K], n_blocks)
        xs_sorted = _dispatch(xn2, dest, xs_sorted)
        hs = _expert_up(xs_sorted, w_expert_in, l, sched)
        ys = _expert_down(hs, w_expert_out, l, sched)
        if l + 1 < depth:
            x2, xn = _combine(x1, gates, dest, ys, norm_mix[l + 1].reshape(1, D_MODEL))
            x_parts = (x2,)
        else:
            y_p, y_s = _combine(x1, gates, dest, ys, norm_final.reshape(1, D_MODEL), final_split=tp)

        a_s = proj[tp:, :MIX_WIDTH].reshape(bs, ls, MIX_WIDTH)
        pool_p.append(a_tail[:, POOL_HALO - POOL_STATE:])
        pool_s.append(jnp.concatenate([state_pool[l], a_s], axis=1)[:, -POOL_STATE:])
        conv_p.append(z_tail[:, CONV_HALO - CONV_STATE:])
        conv_s.append(z_s.reshape(bs, ls, MIX_WIDTH)[:, ls - CONV_STATE:])
        v_out.append(v_s.reshape(bs, ls, MIX_WIDTH))

    return (y_p.reshape(bp, lp, D_MODEL), y_s.reshape(bs, ls, D_MODEL), jnp.stack(pool_p), jnp.stack(pool_s),
            jnp.stack(conv_p), jnp.stack(conv_s), jnp.stack(v_out))
```
